```python
import math
import jax
import jax.numpy as jnp
from jax import lax
import numpy as np

D_MODEL = 1024
BATCH = 16
SEQ = 4096
DEPTH = 1

N_HEADS_DIL = 6
N_HEADS_MOBA = 6
N_HEADS_MEM = 4
N_HEADS_MIX = N_HEADS_DIL + N_HEADS_MOBA + N_HEADS_MEM
HEAD_DIM = D_MODEL // N_HEADS_MIX
W_DIL = N_HEADS_DIL * HEAD_DIM
W_MOBA = N_HEADS_MOBA * HEAD_DIM
W_MEM = N_HEADS_MEM * HEAD_DIM
MIX_WIDTH = W_DIL + W_MOBA + W_MEM
IN_WIDTH = 3 * W_DIL + 3 * W_MOBA + W_MEM
DIL_CONFIGS = ((128, 1), (512, 4), (2048, 16))
BAND_BLOCK = 128
MOBA_BLOCK = 256
MOBA_TOPK = 3
MOBA_QBLOCK = 128
MEM_LEN = 256
PEER_HEADS = 8
PEER_NKEYS = 128
PEER_EXPERTS = PEER_NKEYS * PEER_NKEYS
PEER_TOPK = 16
PEER_DKEY = 256
PEER_CHUNK = 128
RMS_EPS = 1e-6
NEG_INF = -1e30

kernel_name = 'hymba_style_dilated_moba_mem_peer_layer'


def rms_norm(a, g):
    af = a.astype(jnp.float32)
    y = af * lax.rsqrt(jnp.mean(af * af, axis=-1, keepdims=True) + RMS_EPS)
    return (y * g.astype(jnp.float32)).astype(a.dtype)


def split_heads(a, n_heads):
    b, s, _ = a.shape
    return a.reshape(b, s, n_heads, HEAD_DIM).transpose(0, 2, 1, 3)


def merge_heads(a):
    b, h, s, dh = a.shape
    return a.transpose(0, 2, 1, 3).reshape(b, s, h * dh)


def alibi_slopes(n):
    return jnp.exp2(-8.0 * jnp.arange(1, n + 1, dtype=jnp.float32) / n)


def dilated_branch(q, k, v, slopes, window, dilation):
    B, H, S, Dh = q.shape
    L = S // dilation
    n_blk = -(-L // BAND_BLOCK)
    Lp = n_blk * BAND_BLOCK
    reach = window // dilation

    def split(a):
        a = a.reshape(B, H, L, dilation, Dh).transpose(0, 1, 3, 2, 4)
        a = jnp.pad(a, ((0, 0), (0, 0), (0, 0), (0, Lp - L), (0, 0)))
        return a.reshape(B, H, dilation, n_blk, BAND_BLOCK, Dh)

    def with_prev(a):
        prev = jnp.pad(a, ((0, 0), (0, 0), (0, 0), (1, 0), (0, 0), (0, 0)))[:, :, :, :-1]
        return jnp.concatenate([prev, a], axis=4)

    qb = split(q)
    kc = with_prev(split(k))
    vc = with_prev(split(v))
    s = jnp.einsum('bhrnqd,bhrnkd->bhrnqk', qb, kc).astype(jnp.float32) * (1.0 / math.sqrt(Dh))
    ql = jnp.arange(BAND_BLOCK)[:, None]
    kl = jnp.arange(2 * BAND_BLOCK)[None, :]
    delta = BAND_BLOCK + ql - kl
    blk = jnp.arange(n_blk)[:, None, None]
    valid = (delta >= 0) & (delta <= reach) & (blk * BAND_BLOCK - BAND_BLOCK + kl[None] >= 0)
    bias = -slopes[:, None, None, None, None] * (delta * dilation).astype(jnp.float32)
    s = jnp.where(valid, s + bias, NEG_INF)
    m = jnp.max(s, axis=-1, keepdims=True)
    p = jnp.exp(s - m)
    den = jnp.sum(p, axis=-1, keepdims=True)
    lse = m[..., 0] + jnp.log(den[..., 0])
    o = jnp.einsum('bhrnqk,bhrnkd->bhrnqd', p.astype(v.dtype), vc) / den.astype(v.dtype)

    def merge(a):
        tail = a.shape[5:]
        a = a.reshape((B, H, dilation, Lp) + tail)[:, :, :, :L]
        a = jnp.moveaxis(a, 2, 3)
        return a.reshape((B, H, S) + tail)

    return merge(o), merge(lse)


def dilated_mixture(q, k, v, slopes):
    outs, lses = [], []
    for window, dilation in DIL_CONFIGS:
        o, l = dilated_branch(q, k, v, slopes, window, dilation)
        outs.append(o)
        lses.append(l)
    wts = jax.nn.softmax(jnp.stack(lses, axis=0), axis=0).astype(q.dtype)
    return jnp.einsum('cbhs,cbhsd->bhsd', wts, jnp.stack(outs, axis=0))


def moba_attention(q, k, v, slopes):
    B, H, S, Dh = q.shape
    n_blk = -(-S // MOBA_BLOCK)
    Sp = n_blk * MOBA_BLOCK
    pad = ((0, 0), (0, 0), (0, Sp - S), (0, 0))
    kb = jnp.pad(k, pad).reshape(B, H, n_blk, MOBA_BLOCK, Dh)
    vb = jnp.pad(v, pad).reshape(B, H, n_blk, MOBA_BLOCK, Dh)
    k_mean = jnp.mean(kb.astype(jnp.float32), axis=3).astype(k.dtype)
    gate = jnp.einsum('bhsd,bhnd->bhsn', q, k_mean).astype(jnp.float32)
    n_past = jnp.arange(S) // MOBA_BLOCK
    past = jnp.arange(n_blk)[None, :] < n_past[:, None]
    gate = jnp.where(past, gate, NEG_INF)
    k_sel = min(MOBA_TOPK, n_blk)
    _, sel = lax.top_k(gate, k_sel)

    n_qb = S // MOBA_QBLOCK
    qq = q.reshape(B, H, n_qb, MOBA_QBLOCK, Dh).transpose(0, 2, 1, 3, 4).reshape(B * n_qb, H, MOBA_QBLOCK, Dh)
    ss = sel.reshape(B, H, n_qb, MOBA_QBLOCK, k_sel).transpose(0, 2, 1, 3, 4).reshape(B * n_qb, H, MOBA_QBLOCK, k_sel)
    b_ids = jnp.repeat(jnp.arange(B, dtype=jnp.int32), n_qb)
    n_ids = jnp.tile(jnp.arange(n_qb, dtype=jnp.int32), B)
    head = jnp.arange(H)[:, None, None]
    scale = 1.0 / math.sqrt(Dh)
    n_sel_keys = k_sel * MOBA_BLOCK

    def step(args):
        q_blk, sel_blk, b, n = args
        kb_b = lax.dynamic_index_in_dim(kb, b, axis=0, keepdims=False)
        vb_b = lax.dynamic_index_in_dim(vb, b, axis=0, keepdims=False)
        t_q = n * MOBA_QBLOCK + jnp.arange(MOBA_QBLOCK)
        k_g = kb_b[head, sel_blk]
        v_g = vb_b[head, sel_blk]
        s_sel = jnp.einsum('hqd,hqjkd->hqjk', q_blk, k_g).astype(jnp.float32) * scale
        s_pos = sel_blk[..., None] * MOBA_BLOCK + jnp.arange(MOBA_BLOCK)
        ok = jnp.arange(k_sel)[None, :] < (t_q // MOBA_BLOCK)[:, None]
        dist_sel = (t_q[None, :, None, None] - s_pos).astype(jnp.float32)
        s_sel = jnp.where(ok[None, :, :, None], s_sel - slopes[:, None, None, None] * dist_sel, NEG_INF)
        own = (n * MOBA_QBLOCK) // MOBA_BLOCK
        k_own = lax.dynamic_index_in_dim(kb_b, own, axis=1, keepdims=False)
        v_own = lax.dynamic_index_in_dim(vb_b, own, axis=1, keepdims=False)
        o_pos = own * MOBA_BLOCK + jnp.arange(MOBA_BLOCK)
        dist_own = t_q[:, None] - o_pos[None, :]
        s_own = jnp.einsum('hqd,hkd->hqk', q_blk, k_own).astype(jnp.float32) * scale
        s_own = jnp.where(dist_own >= 0, s_own - slopes[:, None, None] * dist_own.astype(jnp.float32), NEG_INF)
        s_all = jnp.concatenate([s_sel.reshape(H, MOBA_QBLOCK, n_sel_keys), s_own], axis=-1)
        p = jax.nn.softmax(s_all, axis=-1).astype(v.dtype)
        p_sel = p[..., :n_sel_keys].reshape(H, MOBA_QBLOCK, k_sel, MOBA_BLOCK)
        p_own = p[..., n_sel_keys:]
        return (jnp.einsum('hqjk,hqjkd->hqd', p_sel, v_g)
                + jnp.einsum('hqk,hkd->hqd', p_own, v_own))

    out = lax.map(step, (qq, ss, b_ids, n_ids))
    return out.reshape(B, n_qb, H, MOBA_QBLOCK, Dh).transpose(0, 2, 1, 3, 4).reshape(B, H, S, Dh)


def memory_attention(q, k, v):
    s = jnp.einsum('bhsd,bhmd->bhsm', q, k).astype(jnp.float32) * (1.0 / math.sqrt(q.shape[-1]))
    p = jax.nn.softmax(s, axis=-1).astype(v.dtype)
    return jnp.einsum('bhsm,bhmd->bhsd', p, v)


def peer_ffn(h, w_q, sub1, sub2, u, v):
    B, S, D = h.shape
    half = PEER_DKEY // 2
    qry = (h @ w_q).reshape(B, S, PEER_HEADS, PEER_DKEY)
    s1 = jnp.einsum('bshd,hkd->bshk', qry[..., :half], sub1).astype(jnp.float32)
    s2 = jnp.einsum('bshd,hkd->bshk', qry[..., half:], sub2).astype(jnp.float32)
    v1, i1 = lax.top_k(s1, PEER_TOPK)
    v2, i2 = lax.top_k(s2, PEER_TOPK)
    cand = (v1[..., :, None] + v2[..., None, :]).reshape(B, S, PEER_HEADS, PEER_TOPK * PEER_TOPK)
    cand_idx = (i1[..., :, None] * PEER_NKEYS + i2[..., None, :]).reshape(B, S, PEER_HEADS, PEER_TOPK * PEER_TOPK)
    top_s, pos = lax.top_k(cand, PEER_TOPK)
    expert = jnp.take_along_axis(cand_idx, pos, axis=-1)
    gate = jax.nn.softmax(top_s, axis=-1)
    T = B * S
    E = PEER_HEADS * PEER_TOPK
    n_chunk = T // PEER_CHUNK
    hx = h.reshape(n_chunk, PEER_CHUNK, D)
    ex = expert.reshape(n_chunk, PEER_CHUNK, E)
    gx = gate.reshape(n_chunk, PEER_CHUNK, E).astype(h.dtype)

    def step(args):
        x_c, e_c, g_c = args
        a = jnp.einsum('cd,ced->ce', x_c, u[e_c])
        w = jax.nn.gelu(a.astype(jnp.float32), approximate=False).astype(x_c.dtype) * g_c
        return jnp.einsum('ce,ced->cd', w, v[e_c])

    return lax.map(step, (hx, ex, gx)).reshape(B, S, D)


def hybrid_layer(x, mem, g_mix, w_in, qg_dil, kg_dil, qg_moba, kg_moba, qg_mem, kg_mem, g_memtok, w_mem_kv,
                 og_dil, og_moba, og_mem, w_out, g_ffn, w_peer_q, sub1, sub2, peer_u, peer_v):
    slopes = alibi_slopes(N_HEADS_DIL + N_HEADS_MOBA)
    sl_dil, sl_moba = slopes[0::2], slopes[1::2]
    hn = rms_norm(x, g_mix)
    proj = hn @ w_in
    cuts = [W_DIL, 2 * W_DIL, 3 * W_DIL, 3 * W_DIL + W_MOBA, 3 * W_DIL + 2 * W_MOBA, 3 * W_DIL + 3 * W_MOBA]
    q_d, k_d, v_d, q_b, k_b, v_b, q_m = jnp.split(proj, cuts, axis=-1)
    q_d = rms_norm(split_heads(q_d, N_HEADS_DIL), qg_dil)
    k_d = rms_norm(split_heads(k_d, N_HEADS_DIL), kg_dil)
    o_dil = dilated_mixture(q_d, k_d, split_heads(v_d, N_HEADS_DIL), sl_dil)
    q_b = rms_norm(split_heads(q_b, N_HEADS_MOBA), qg_moba)
    k_b = rms_norm(split_heads(k_b, N_HEADS_MOBA), kg_moba)
    o_moba = moba_attention(q_b, k_b, split_heads(v_b, N_HEADS_MOBA), sl_moba)
    kv = rms_norm(mem, g_memtok) @ w_mem_kv
    k_m, v_m = jnp.split(kv, 2, axis=-1)
    q_m = rms_norm(split_heads(q_m, N_HEADS_MEM), qg_mem)
    k_m = rms_norm(split_heads(k_m, N_HEADS_MEM), kg_mem)
    o_mem = memory_attention(q_m, k_m, split_heads(v_m, N_HEADS_MEM))
    y = jnp.concatenate([rms_norm(merge_heads(o_dil), og_dil),
                         rms_norm(merge_heads(o_moba), og_moba),
                         rms_norm(merge_heads(o_mem), og_mem)], axis=-1) @ w_out
    x = x + y
    x = x + peer_ffn(rms_norm(x, g_ffn), w_peer_q, sub1, sub2, peer_u, peer_v)
    return x


def setup_inputs(seed: int = 0) -> dict:
    key = jax.random.key(seed)
    ks = jax.random.split(key, 24)

    def nrm(k, shape, scale):
        return jax.random.normal(k, shape, jnp.float32) * scale

    def gain(k, n):
        return 1.0 + 0.02 * jax.random.normal(k, (DEPTH, n), jnp.float32)

    return {
        'x': nrm(ks[0], (BATCH, SEQ, D_MODEL), 1.0),
        'mem': nrm(ks[1], (BATCH, MEM_LEN, D_MODEL), 1.0),
        'g_mix': gain(ks[2], D_MODEL),
        'w_in': nrm(ks[3], (DEPTH, D_MODEL, IN_WIDTH), D_MODEL ** -0.5),
        'qg_dil': gain(ks[4], HEAD_DIM),
        'kg_dil': gain(ks[5], HEAD_DIM),
        'qg_moba': gain(ks[6], HEAD_DIM),
        'kg_moba': gain(ks[7], HEAD_DIM),
        'qg_mem': gain(ks[8], HEAD_DIM),
        'kg_mem': gain(ks[9], HEAD_DIM),
        'g_memtok': gain(ks[10], D_MODEL),
        'w_mem_kv': nrm(ks[11], (DEPTH, D_MODEL, 2 * W_MEM), D_MODEL ** -0.5),
        'og_dil': gain(ks[12], W_DIL),
        'og_moba': gain(ks[13], W_MOBA),
        'og_mem': gain(ks[14], W_MEM),
        'w_out': nrm(ks[15], (DEPTH, MIX_WIDTH, D_MODEL), MIX_WIDTH ** -0.5),
        'g_ffn': gain(ks[16], D_MODEL),
        'w_peer_q': nrm(ks[17], (DEPTH, D_MODEL, PEER_HEADS * PEER_DKEY), D_MODEL ** -0.5),
        'peer_subkeys_1': nrm(ks[18], (DEPTH, PEER_HEADS, PEER_NKEYS, PEER_DKEY // 2), (PEER_DKEY // 2) ** -0.5),
        'peer_subkeys_2': nrm(ks[19], (DEPTH, PEER_HEADS, PEER_NKEYS, PEER_DKEY // 2), (PEER_DKEY // 2) ** -0.5),
        'peer_u': nrm(ks[20], (DEPTH, PEER_EXPERTS, D_MODEL), D_MODEL ** -0.5),
        'peer_v': nrm(ks[21], (DEPTH, PEER_EXPERTS, D_MODEL), PEER_HEADS ** -0.5),
    }


def reference(x, mem, g_mix, w_in, qg_dil, kg_dil, qg_moba, kg_moba, qg_mem, kg_mem, g_memtok, w_mem_kv,
              og_dil, og_moba, og_mem, w_out, g_ffn, w_peer_q, peer_subkeys_1, peer_subkeys_2, peer_u, peer_v):
    h = x
    for layer in range(DEPTH):
        h = hybrid_layer(h, mem, g_mix[layer], w_in[layer], qg_dil[layer], kg_dil[layer], qg_moba[layer],
                         kg_moba[layer], qg_mem[layer], kg_mem[layer], g_memtok[layer], w_mem_kv[layer],
                         og_dil[layer], og_moba[layer], og_mem[layer], w_out[layer], g_ffn[layer],
                         w_peer_q[layer], peer_subkeys_1[layer], peer_subkeys_2[layer], peer_u[layer],
                         peer_v[layer])
    return h
```

```python
import functools
import math

import jax
import jax.numpy as jnp
from jax import lax
from jax.experimental import pallas as pl
from jax.experimental.pallas import tpu as pltpu

F32 = jnp.float32
BF16 = jnp.bfloat16
I32 = jnp.int32

LANES = 128
D_MODEL = 1024
N_HEADS_DIL = 6
N_HEADS_MOBA = 6
N_HEADS_MEM = 4
HEAD_DIM = 64
W_DIL = N_HEADS_DIL * HEAD_DIM
W_MOBA = N_HEADS_MOBA * HEAD_DIM
W_MEM = N_HEADS_MEM * HEAD_DIM
IN_WIDTH = 3 * W_DIL + 3 * W_MOBA + W_MEM
DIL_CONFIGS = ((128, 1), (512, 4), (2048, 16))
BAND_BLOCK = 128
MOBA_BLOCK = 256
MOBA_TOPK = 3
MEM_LEN = 256
PEER_HEADS = 8
PEER_NKEYS = 128
PEER_EXPERTS = PEER_NKEYS * PEER_NKEYS
PEER_TOPK = 16
PEER_DKEY = 256
PEER_SLOTS = PEER_HEADS * PEER_TOPK
RMS_EPS = 1e-6
NEG_INF = -1e30
SCALE = 1.0 / math.sqrt(HEAD_DIM)
SQRT_HALF = math.sqrt(0.5)

VMEM_LIMIT = 56 * 1024 * 1024

_NT = (((1,), (1,)), ((), ()))


def _shr(a, bits):
    return lax.shift_right_logical(a, jnp.full(a.shape, bits, a.dtype))


def _cparams(sem):
    return pltpu.CompilerParams(dimension_semantics=sem, vmem_limit_bytes=VMEM_LIMIT)


def _row_rms(a, gain):
    return a * lax.rsqrt(jnp.mean(a * a, axis=-1, keepdims=True) + RMS_EPS) * gain


def _group_mean_sq(p, group):
    w = p.shape[-1]
    bits = group.bit_length() - 1
    gi = _shr(lax.broadcasted_iota(I32, (w, w), 0), bits)
    gj = _shr(lax.broadcasted_iota(I32, (w, w), 1), bits)
    ones_bd = jnp.where(gi == gj, 1.0, 0.0).astype(BF16)
    p2 = p * p
    hi = p2.astype(BF16)
    lo = (p2 - hi.astype(F32)).astype(BF16)
    ss = (jnp.dot(hi, ones_bd, preferred_element_type=F32)
          + jnp.dot(lo, ones_bd, preferred_element_type=F32))
    return ss * (1.0 / group)


def _head_rms(p, gain):
    return p * lax.rsqrt(_group_mean_sq(p, HEAD_DIM) + RMS_EPS) * gain


def _memkv_kernel(mem_ref, g_ref, w_ref, kg_ref, k_ref, v_ref):
    hn = _row_rms(mem_ref[...], g_ref[...]).astype(BF16)
    kv = jnp.dot(hn, w_ref[...], preferred_element_type=F32)
    k_ref[...] = _head_rms(kv[:, :W_MEM], kg_ref[...])
    v_ref[...] = kv[:, W_MEM:]


def _memkv(mem, g_memtok, w_kv, kg_mem):
    b = mem.shape[0]
    return pl.pallas_call(
        _memkv_kernel,
        grid=(b,),
        in_specs=[
            pl.BlockSpec((None, MEM_LEN, D_MODEL), lambda i: (i, 0, 0)),
            pl.BlockSpec((1, D_MODEL), lambda i: (0, 0)),
            pl.BlockSpec((D_MODEL, 2 * W_MEM), lambda i: (0, 0)),
            pl.BlockSpec((1, W_MEM), lambda i: (0, 0)),
        ],
        out_specs=[pl.BlockSpec((None, MEM_LEN, W_MEM), lambda i: (i, 0, 0))] * 2,
        out_shape=[jax.ShapeDtypeStruct((b, MEM_LEN, W_MEM), F32)] * 2,
        compiler_params=_cparams(("parallel",)),
        name="memkv",
    )(mem, g_memtok, w_kv, kg_mem)


INPROJ_TM = 512


def _inproj_kernel(x_ref, g_ref, w_ref, gqd, gkd, gqb, gkb, gqm, qd, kd, vd, qb, kb, vb, qm):
    hb = _row_rms(x_ref[...], g_ref[...]).astype(BF16)

    def seg(lo, width):
        return jnp.dot(hb, w_ref[:, lo:lo + width], preferred_element_type=F32)

    qd[...] = _head_rms(seg(0, W_DIL), gqd[...])
    kd[...] = _head_rms(seg(W_DIL, W_DIL), gkd[...])
    vd[...] = seg(2 * W_DIL, W_DIL)
    base = 3 * W_DIL
    qb[...] = _head_rms(seg(base, W_MOBA), gqb[...])
    kb[...] = _head_rms(seg(base + W_MOBA, W_MOBA), gkb[...])
    vb[...] = seg(base + 2 * W_MOBA, W_MOBA)
    qm[...] = _head_rms(seg(base + 3 * W_MOBA, W_MEM), gqm[...])


def _inproj(x2, g_mix, w_in, gqd, gkd, gqb, gkb, gqm):
    t = x2.shape[0]
    tm = INPROJ_TM
    row = lambda w: pl.BlockSpec((1, w), lambda i: (0, 0))
    tile = lambda w: pl.BlockSpec((tm, w), lambda i: (i, 0))
    widths = (W_DIL, W_DIL, W_DIL, W_MOBA, W_MOBA, W_MOBA, W_MEM)
    return pl.pallas_call(
        _inproj_kernel,
        grid=(t // tm,),
        in_specs=[tile(D_MODEL), row(D_MODEL), pl.BlockSpec((D_MODEL, IN_WIDTH), lambda i: (0, 0)),
                  row(W_DIL), row(W_DIL), row(W_MOBA), row(W_MOBA), row(W_MEM)],
        out_specs=[tile(w) for w in widths],
        out_shape=[jax.ShapeDtypeStruct((t, w), F32) for w in widths],
        compiler_params=_cparams(("parallel",)),
        name="inproj",
    )(x2, g_mix, w_in, gqd, gkd, gqb, gkb, gqm)


DIL_PAD = BAND_BLOCK * max(d for _, d in DIL_CONFIGS)
DIL_MIX_ROWS = 256


def _dil_kernel(q_ref, k_ref, v_ref, ns_ref, o_ref, kpad, vpad, ob0, ob1, ob2, lb0, lb1, lb2):
    s_len = q_ref.shape[0]
    zeros = jnp.zeros((DIL_PAD, LANES), F32)
    kpad[0:DIL_PAD, :] = zeros
    vpad[0:DIL_PAD, :] = zeros
    kpad[DIL_PAD:, :] = k_ref[...]
    vpad[DIL_PAD:, :] = v_ref[...]

    blk = BAND_BLOCK
    head0 = lax.broadcasted_iota(I32, (blk, LANES), 1) < HEAD_DIM
    ql = lax.broadcasted_iota(I32, (blk, 2 * blk), 0)
    kl = lax.broadcasted_iota(I32, (blk, 2 * blk), 1)
    delta = blk + ql - kl
    obs = (ob0, ob1, ob2)
    lbs = (lb0, lb1, lb2)

    for c, (window, dil) in enumerate(DIL_CONFIGS):
        reach = window // dil
        n_blk = s_len // dil // blk
        in_band = (delta >= 0) & (delta <= reach)
        dist = (delta * dil).astype(F32)
        ob, lb = obs[c], lbs[c]

        def rows(start, size, dil=dil):
            return pl.ds(start, size) if dil == 1 else pl.ds(start, size, stride=dil)

        def block_body(n, r, dil=dil, in_band=in_band, dist=dist, ob=ob, lb=lb, rows=rows):
            q_start = r + dil * blk * n
            k_start = DIL_PAD + r + dil * blk * (n - 1)
            qv = q_ref[rows(q_start, blk), :]
            kv = kpad[rows(k_start, 2 * blk), :].astype(BF16)
            vv = vpad[rows(k_start, 2 * blk), :].astype(BF16)
            valid = in_band & (kl >= jnp.where(n == 0, blk, 0))
            outs, lses = [], []
            for h in range(2):
                qh = jnp.where(head0 if h == 0 else ~head0, qv, 0.0).astype(BF16)
                s = lax.dot_general(qh, kv, _NT, preferred_element_type=F32) * SCALE
                s = jnp.where(valid, s + ns_ref[h] * dist, NEG_INF)
                m = jnp.max(s, axis=-1, keepdims=True)
                p = jnp.exp(s - m)
                den = jnp.sum(p, axis=-1, keepdims=True)
                lses.append(jnp.broadcast_to(m + jnp.log(den), (blk, LANES)))
                outs.append(jnp.dot(p.astype(BF16), vv, preferred_element_type=F32) / den)
            ob[rows(q_start, blk), :] = jnp.where(head0, outs[0], outs[1])
            lb[rows(q_start, blk), :] = jnp.where(head0, lses[0], lses[1])

        def residue_body(r, carry, n_blk=n_blk, block_body=block_body):
            def inner(n, c2):
                block_body(n, r)
                return c2
            return lax.fori_loop(0, n_blk, inner, carry)

        lax.fori_loop(0, dil, residue_body, 0)

    def mix(i, carry):
        sl = pl.ds(pl.multiple_of(i * DIL_MIX_ROWS, DIL_MIX_ROWS), DIL_MIX_ROWS)
        l0, l1, l2 = lb0[sl, :], lb1[sl, :], lb2[sl, :]
        mx = jnp.maximum(jnp.maximum(l0, l1), l2)
        e0, e1, e2 = jnp.exp(l0 - mx), jnp.exp(l1 - mx), jnp.exp(l2 - mx)
        tot = e0 + e1 + e2
        o_ref[sl, :] = (e0 / tot) * ob0[sl, :] + (e1 / tot) * ob1[sl, :] + (e2 / tot) * ob2[sl, :]
        return carry

    lax.fori_loop(0, s_len // DIL_MIX_ROWS, mix, 0)


def _dilated(q, k, v, neg_slopes):
    b, s_len, w = q.shape
    n_pair = w // LANES
    slab = pl.BlockSpec((None, s_len, LANES), lambda i, p: (i, 0, p))
    buf = lambda rows: pltpu.VMEM((rows, LANES), F32)
    return pl.pallas_call(
        _dil_kernel,
        grid=(b, n_pair),
        in_specs=[slab, slab, slab,
                  pl.BlockSpec((None, 2, 1, 2 * BAND_BLOCK), lambda i, p: (p, 0, 0, 0))],
        out_specs=slab,
        out_shape=jax.ShapeDtypeStruct((b, s_len, w), F32),
        scratch_shapes=[buf(DIL_PAD + s_len), buf(DIL_PAD + s_len)] + [buf(s_len)] * 6,
        compiler_params=_cparams(("parallel", "parallel")),
        name="dilated",
    )(q, k, v, neg_slopes)


def _split_bf16(a):
    hi = a.astype(BF16)
    lo = (a - hi.astype(F32)).astype(BF16)
    return hi, lo


def _moba_kernel(q_ref, k_ref, v_ref, ns_ref, o_ref, qa_ref, kb_ref, vb_ref, m_ref, l_ref, acc_ref):
    s_len = q_ref.shape[0]
    blk = MOBA_BLOCK
    n_blk = s_len // blk
    lane = lax.broadcasted_iota(I32, (blk, LANES), 1)
    head0 = lane < HEAD_DIM

    vb_ref[...] = v_ref[...].astype(BF16)
    for j in range(n_blk):
        kb_ref[j * blk:(j + 1) * blk, 0:LANES] = k_ref[j * blk:(j + 1) * blk, :].astype(BF16)
        kb_ref[j * blk:(j + 1) * blk, LANES:] = jnp.where(lane == j, NEG_INF, 0.0).astype(BF16)

    k_mean = jnp.concatenate(
        [jnp.sum(k_ref[j * blk:(j + 1) * blk, :], axis=0, keepdims=True) for j in range(n_blk)],
        axis=0) * (1.0 / blk)
    q_hi, q_lo = _split_bf16(q_ref[...])

    blk_id = lax.broadcasted_iota(I32, (n_blk, s_len), 0)
    n_past = _shr(lax.broadcasted_iota(I32, (n_blk, s_len), 1), blk.bit_length() - 1)
    past = blk_id < n_past
    blk_f = blk_id.astype(F32)
    pad_rows = LANES - n_blk

    for h in range(2):
        hm = (lax.broadcasted_iota(I32, (n_blk, LANES), 1) < HEAD_DIM) == (h == 0)
        km_hi, km_lo = _split_bf16(jnp.where(hm, k_mean, 0.0))
        gate = (lax.dot_general(km_hi, q_hi, _NT, preferred_element_type=F32)
                + lax.dot_general(km_hi, q_lo, _NT, preferred_element_type=F32)
                + lax.dot_general(km_lo, q_hi, _NT, preferred_element_type=F32))
        g = jnp.where(past, gate, NEG_INF)
        sel = jnp.zeros((n_blk, s_len), F32)
        for _ in range(MOBA_TOPK):
            m = jnp.max(g, axis=0, keepdims=True)
            first = jnp.min(jnp.where(g == m, blk_f, float(n_blk)), axis=0, keepdims=True)
            pick = blk_f == first
            sel = jnp.where(pick, 1.0, sel)
            g = jnp.where(pick, -jnp.inf, g)
        not_sel = jnp.where(past, 1.0 - sel, 1.0)
        not_sel = jnp.concatenate([not_sel, jnp.zeros((pad_rows, s_len), F32)], axis=0)
        for i in range(n_blk):
            rows = slice(i * blk, (i + 1) * blk)
            qh = jnp.where(head0 if h == 0 else ~head0, q_ref[rows, :], 0.0)
            qa_ref[h, rows, 0:LANES] = qh.astype(BF16)
            qa_ref[h, rows, LANES:] = not_sel[:, rows].T.astype(BF16)

    row_i = lax.broadcasted_iota(I32, (blk, blk), 0)
    col_i = lax.broadcasted_iota(I32, (blk, blk), 1)
    rel = (row_i - col_i).astype(F32)
    causal = row_i >= col_i

    for h in range(2):
        ns = ns_ref[h]
        bias0 = ns * rel

        def q_tile(i, carry, h=h, ns=ns, bias0=bias0):
            r0 = pl.multiple_of(i * blk, blk)
            qh = qa_ref[h, pl.ds(r0, blk), 0:LANES]
            qa = jnp.concatenate([qh, qa_ref[h, pl.ds(r0, blk), LANES:]], axis=1)
            k_own = kb_ref[pl.ds(r0, blk), 0:LANES]
            v_own = vb_ref[pl.ds(r0, blk), :]
            s = lax.dot_general(qh, k_own, _NT, preferred_element_type=F32) * SCALE
            s = jnp.where(causal, s + bias0, NEG_INF)
            m0 = jnp.max(s, axis=-1, keepdims=True)
            p = jnp.exp(s - m0)
            m_ref[...] = m0
            l_ref[...] = jnp.sum(p, axis=-1, keepdims=True)
            acc_ref[...] = jnp.dot(p.astype(BF16), v_own, preferred_element_type=F32)

            def past_block(j, carry2):
                m, l = m_ref[...], l_ref[...]
                c0 = pl.multiple_of(j * blk, blk)
                k_aug = jnp.concatenate([kb_ref[pl.ds(c0, blk), 0:LANES],
                                         kb_ref[pl.ds(c0, blk), LANES:]], axis=1)
                sj = lax.dot_general(qa, k_aug, _NT, preferred_element_type=F32) * SCALE
                sj = sj + (bias0 + ns[:, 0:1] * ((i - j) * blk).astype(F32))
                m_new = jnp.maximum(m, jnp.max(sj, axis=-1, keepdims=True))
                alpha = jnp.exp(m - m_new)
                pj = jnp.exp(sj - m_new)
                m_ref[...] = m_new
                l_ref[...] = alpha * l + jnp.sum(pj, axis=-1, keepdims=True)
                acc_ref[...] = alpha * acc_ref[...] + jnp.dot(
                    pj.astype(BF16), vb_ref[pl.ds(c0, blk), :], preferred_element_type=F32)
                return carry2

            lax.fori_loop(0, i, past_block, 0)
            out = acc_ref[...] / l_ref[...]
            if h == 0:
                o_ref[pl.ds(r0, blk), :] = out
            else:
                o_ref[pl.ds(r0, blk), :] = jnp.where(head0, o_ref[pl.ds(r0, blk), :], out)
            return carry

        lax.fori_loop(0, n_blk, q_tile, 0)


def _moba(q, k, v, neg_slopes):
    b, s_len, w = q.shape
    n_pair = w // LANES
    slab = pl.BlockSpec((None, s_len, LANES), lambda i, p: (i, 0, p))
    return pl.pallas_call(
        _moba_kernel,
        grid=(b, n_pair),
        in_specs=[slab, slab, slab,
                  pl.BlockSpec((None, 2, 1, MOBA_BLOCK), lambda i, p: (p, 0, 0, 0))],
        out_specs=slab,
        out_shape=jax.ShapeDtypeStruct((b, s_len, w), F32),
        scratch_shapes=[pltpu.VMEM((2, s_len, 2 * LANES), BF16),
                        pltpu.VMEM((s_len, 2 * LANES), BF16),
                        pltpu.VMEM((s_len, LANES), BF16),
                        pltpu.VMEM((MOBA_BLOCK, 1), F32),
                        pltpu.VMEM((MOBA_BLOCK, 1), F32),
                        pltpu.VMEM((MOBA_BLOCK, LANES), F32)],
        compiler_params=_cparams(("parallel", "parallel")),
        name="moba",
    )(q, k, v, neg_slopes)


MEMATTN_TQ = 512


def _memattn_kernel(q_ref, k_ref, v_ref, o_ref):
    q = q_ref[...]
    kb = k_ref[...].astype(BF16)
    vb = v_ref[...].astype(BF16)
    head = _shr(lax.broadcasted_iota(I32, q.shape, 1), HEAD_DIM.bit_length() - 1)
    out = jnp.zeros(q.shape, F32)
    for h in range(N_HEADS_MEM):
        qh = jnp.where(head == h, q, 0.0).astype(BF16)
        s = lax.dot_general(qh, kb, _NT, preferred_element_type=F32) * SCALE
        m = jnp.max(s, axis=-1, keepdims=True)
        p = jnp.exp(s - m)
        den = jnp.sum(p, axis=-1, keepdims=True)
        oh = jnp.dot(p.astype(BF16), vb, preferred_element_type=F32) / den
        out = jnp.where(head == h, oh, out)
    o_ref[...] = out


def _memattn(q, k, v):
    b, s_len, w = q.shape
    tq = MEMATTN_TQ
    return pl.pallas_call(
        _memattn_kernel,
        grid=(b, s_len // tq),
        in_specs=[pl.BlockSpec((None, tq, w), lambda i, j: (i, j, 0)),
                  pl.BlockSpec((None, MEM_LEN, w), lambda i, j: (i, 0, 0)),
                  pl.BlockSpec((None, MEM_LEN, w), lambda i, j: (i, 0, 0))],
        out_specs=pl.BlockSpec((None, tq, w), lambda i, j: (i, j, 0)),
        out_shape=jax.ShapeDtypeStruct((b, s_len, w), F32),
        compiler_params=_cparams(("parallel", "parallel")),
        name="memattn",
    )(q, k, v)


OUTPROJ_TM = 512


def _outproj_kernel(od_ref, ob_ref, om_ref, x_ref, gd, gb, gm, wo_ref, gf, wq_ref, k1_ref, k2_ref,
                    x1_ref, hn_ref, s1_ref, s2_ref):
    y = jnp.dot(_row_rms(od_ref[...], gd[...]).astype(BF16), wo_ref[0:W_DIL, :],
                preferred_element_type=F32)
    y += jnp.dot(_row_rms(ob_ref[...], gb[...]).astype(BF16), wo_ref[W_DIL:W_DIL + W_MOBA, :],
                 preferred_element_type=F32)
    y += jnp.dot(_row_rms(om_ref[...], gm[...]).astype(BF16), wo_ref[W_DIL + W_MOBA:, :],
                 preferred_element_type=F32)
    x1 = x_ref[...] + y
    x1_ref[...] = x1
    hb = _row_rms(x1, gf[...]).astype(BF16)
    hn_ref[...] = hb
    half = PEER_DKEY // 2
    for h in range(PEER_HEADS):
        qh = jnp.dot(hb, wq_ref[:, h * PEER_DKEY:(h + 1) * PEER_DKEY],
                     preferred_element_type=F32).astype(BF16)
        s1_ref[h] = lax.dot_general(k1_ref[h], qh[:, :half], _NT, preferred_element_type=F32)
        s2_ref[h] = lax.dot_general(k2_ref[h], qh[:, half:], _NT, preferred_element_type=F32)


def _outproj(o_dil, o_moba, o_mem, x2, og_dil, og_moba, og_mem, w_out, g_ffn, w_q, sub1, sub2):
    t = x2.shape[0]
    tm = OUTPROJ_TM
    row = lambda w: pl.BlockSpec((1, w), lambda i: (0, 0))
    tile = lambda w: pl.BlockSpec((tm, w), lambda i: (i, 0))
    full = lambda shape: pl.BlockSpec(shape, lambda i: (0,) * len(shape))
    score = pl.BlockSpec((PEER_HEADS, PEER_NKEYS, tm), lambda i: (0, 0, i))
    half = PEER_DKEY // 2
    return pl.pallas_call(
        _outproj_kernel,
        grid=(t // tm,),
        in_specs=[tile(W_DIL), tile(W_MOBA), tile(W_MEM), tile(D_MODEL),
                  row(W_DIL), row(W_MOBA), row(W_MEM), full((D_MODEL, D_MODEL)), row(D_MODEL),
                  full((D_MODEL, PEER_HEADS * PEER_DKEY)),
                  full((PEER_HEADS, PEER_NKEYS, half)), full((PEER_HEADS, PEER_NKEYS, half))],
        out_specs=[tile(D_MODEL), tile(D_MODEL), score, score],
        out_shape=[jax.ShapeDtypeStruct((t, D_MODEL), F32),
                   jax.ShapeDtypeStruct((t, D_MODEL), BF16),
                   jax.ShapeDtypeStruct((PEER_HEADS, PEER_NKEYS, t), F32),
                   jax.ShapeDtypeStruct((PEER_HEADS, PEER_NKEYS, t), F32)],
        compiler_params=_cparams(("parallel",)),
        name="outproj",
    )(o_dil, o_moba, o_mem, x2, og_dil, og_moba, og_mem, w_out, g_ffn, w_q, sub1, sub2)


TOPK_TM = 512


def _top_rows(s, k):
    n_rows = s.shape[0]
    rid = lax.broadcasted_iota(I32, s.shape, 0).astype(F32)
    vals, idxs = [], []
    for _ in range(k):
        m = jnp.max(s, axis=0, keepdims=True)
        first = jnp.min(jnp.where(s == m, rid, float(n_rows)), axis=0, keepdims=True)
        vals.append(m)
        idxs.append(first)
        s = jnp.where(rid == first, -jnp.inf, s)
    return jnp.concatenate(vals, axis=0), jnp.concatenate(idxs, axis=0)


def _pick_rows(table, sel):
    out = jnp.zeros(sel.shape, table.dtype)
    for j in range(table.shape[0]):
        out = jnp.where(sel == j, table[j:j + 1, :], out)
    return out.astype(I32)


def _peertopk_kernel(s1_ref, s2_ref, i1_ref, i2_ref, g_ref):
    n_col = s1_ref.shape[2] // LANES

    def head_body(h, carry):
        for c in range(n_col):
            cols = slice(c * LANES, (c + 1) * LANES)
            v1, i1 = _top_rows(s1_ref[h, :, cols], PEER_TOPK)
            v2, i2 = _top_rows(s2_ref[h, :, cols], PEER_TOPK)
            cand = jnp.concatenate([v1[j:j + 1, :] + v2 for j in range(PEER_TOPK)], axis=0)
            top_s, pos = _top_rows(cand, PEER_TOPK)
            e = jnp.exp(top_s - top_s[0:1, :])
            g_ref[h, :, cols] = e / jnp.sum(e, axis=0, keepdims=True)
            pos = pos.astype(I32)
            bits = PEER_TOPK.bit_length() - 1
            i1_ref[h, :, cols] = _pick_rows(i1, _shr(pos, bits))
            i2_ref[h, :, cols] = _pick_rows(i2, pos & (PEER_TOPK - 1))
        return carry

    lax.fori_loop(0, PEER_HEADS, head_body, 0)


def _peertopk(s1, s2):
    t = s1.shape[2]
    tm = TOPK_TM
    score = pl.BlockSpec((PEER_HEADS, PEER_NKEYS, tm), lambda i: (0, 0, i))
    slot = pl.BlockSpec((PEER_HEADS, PEER_TOPK, tm), lambda i: (0, 0, i))
    return pl.pallas_call(
        _peertopk_kernel,
        grid=(t // tm,),
        in_specs=[score, score],
        out_specs=[slot, slot, slot],
        out_shape=[jax.ShapeDtypeStruct((PEER_HEADS, PEER_TOPK, t), I32),
                   jax.ShapeDtypeStruct((PEER_HEADS, PEER_TOPK, t), I32),
                   jax.ShapeDtypeStruct((PEER_HEADS, PEER_TOPK, t), F32)],
        compiler_params=_cparams(("parallel",)),
        name="peertopk",
    )(s1, s2)


PEER_TM = 256
PEER_TE = 1024
PEER_PITCH = PEER_TM + 8


def _peerffn_kernel(h_ref, x1_ref, i1_ref, i2_ref, g_ref, u_ref, v_ref, o_ref, gate_ref, acc_ref):
    c = pl.program_id(1)
    n_chunk = pl.num_programs(1)
    tm = h_ref.shape[0]
    slabs = PEER_TE // PEER_NKEYS

    @pl.when(c == 0)
    def _build_gate_matrix():
        key_id = lax.broadcasted_iota(I32, (PEER_NKEYS, PEER_SLOTS), 0)

        def token(t, carry):
            i1 = jnp.broadcast_to(i1_ref[pl.ds(t, 1), :], key_id.shape)
            i2 = jnp.broadcast_to(i2_ref[pl.ds(t, 1), :], key_id.shape)
            g = jnp.broadcast_to(g_ref[pl.ds(t, 1), :], key_id.shape)
            lhs = jnp.where(key_id == i1, g, 0.0).astype(BF16)
            rhs = jnp.where(key_id == i2, 1.0, 0.0).astype(BF16)
            gt = lax.dot_general(lhs, rhs, _NT, preferred_element_type=F32)
            gate_ref[pl.ds(t, PEER_NKEYS, stride=PEER_PITCH), :] = gt
            return carry

        lax.fori_loop(0, tm, token, 0)
        acc_ref[...] = jnp.zeros(acc_ref.shape, F32)

    a = lax.dot_general(h_ref[...], u_ref[...], _NT, preferred_element_type=F32)
    ws = []
    for j in range(slabs):
        start = pl.multiple_of((c * slabs + j) * PEER_PITCH, 8)
        gj = gate_ref[pl.ds(start, tm), :]
        aj = a[:, j * PEER_NKEYS:(j + 1) * PEER_NKEYS]
        ws.append((0.5 * aj * (1.0 + lax.erf(aj * SQRT_HALF)) * gj).astype(BF16))
    w = jnp.concatenate(ws, axis=1)
    acc_ref[...] += jnp.dot(w, v_ref[...], preferred_element_type=F32)

    @pl.when(c == n_chunk - 1)
    def _finish():
        o_ref[...] = x1_ref[...] + acc_ref[...]


def _peerffn(hn, x1, i1, i2, gate, u, v):
    t = hn.shape[0]
    tm, te = PEER_TM, PEER_TE
    tile = lambda w: pl.BlockSpec((tm, w), lambda i, c: (i, 0))
    chunk = pl.BlockSpec((te, D_MODEL), lambda i, c: (c, 0))
    return pl.pallas_call(
        _peerffn_kernel,
        grid=(t // tm, PEER_EXPERTS // te),
        in_specs=[tile(D_MODEL), tile(D_MODEL), tile(PEER_SLOTS), tile(PEER_SLOTS), tile(PEER_SLOTS),
                  chunk, chunk],
        out_specs=tile(D_MODEL),
        out_shape=jax.ShapeDtypeStruct((t, D_MODEL), F32),
        scratch_shapes=[pltpu.VMEM((PEER_NKEYS * PEER_PITCH, LANES), F32),
                        pltpu.VMEM((tm, D_MODEL), F32)],
        compiler_params=_cparams(("parallel", "arbitrary")),
        name="peerffn",
    )(hn, x1, i1, i2, gate, u, v)


def _neg_slope_rows(slopes, width):
    n = slopes.shape[0]
    return jnp.broadcast_to((-slopes).reshape(n // 2, 2, 1, 1), (n // 2, 2, 1, width))


def _tiled_gain(g, reps):
    return jnp.tile(g, reps)[None, :]


def _layer(x, mem, g_mix, w_in, qg_dil, kg_dil, qg_moba, kg_moba, qg_mem, kg_mem, g_memtok, w_mem_kv,
           og_dil, og_moba, og_mem, w_out, g_ffn, w_peer_q, sub1, sub2, peer_u, peer_v):
    b, s_len, d = x.shape
    t = b * s_len
    n_mix = N_HEADS_DIL + N_HEADS_MOBA
    slopes = jnp.exp2(-8.0 * jnp.arange(1, n_mix + 1, dtype=F32) / n_mix)
    ns_dil = _neg_slope_rows(slopes[0::2], 2 * BAND_BLOCK)
    ns_moba = _neg_slope_rows(slopes[1::2], MOBA_BLOCK)

    x2 = x.reshape(t, d)
    k_m, v_m = _memkv(mem, g_memtok[None, :], w_mem_kv.astype(BF16), _tiled_gain(kg_mem, N_HEADS_MEM))
    q_d, k_d, v_d, q_b, k_b, v_b, q_m = _inproj(
        x2, g_mix[None, :], w_in.astype(BF16),
        _tiled_gain(qg_dil, N_HEADS_DIL), _tiled_gain(kg_dil, N_HEADS_DIL),
        _tiled_gain(qg_moba, N_HEADS_MOBA), _tiled_gain(kg_moba, N_HEADS_MOBA),
        _tiled_gain(qg_mem, N_HEADS_MEM))
    seq = lambda a: a.reshape(b, s_len, a.shape[-1])
    o_dil = _dilated(seq(q_d), seq(k_d), seq(v_d), ns_dil)
    o_moba = _moba(seq(q_b), seq(k_b), seq(v_b), ns_moba)
    o_mem = _memattn(seq(q_m), k_m, v_m)
    x1, hn, s1, s2 = _outproj(
        o_dil.reshape(t, W_DIL), o_moba.reshape(t, W_MOBA), o_mem.reshape(t, W_MEM), x2,
        og_dil[None, :], og_moba[None, :], og_mem[None, :], w_out.astype(BF16), g_ffn[None, :],
        w_peer_q.astype(BF16), sub1.astype(BF16), sub2.astype(BF16))
    i1, i2, gate = _peertopk(s1, s2)
    slots = lambda a: a.reshape(PEER_SLOTS, t).T
    out = _peerffn(hn, x1, slots(i1), slots(i2), slots(gate), peer_u.astype(BF16), peer_v.astype(BF16))
    return out.reshape(b, s_len, d)


def kernel(x, mem, g_mix, w_in, qg_dil, kg_dil, qg_moba, kg_moba, qg_mem, kg_mem, g_memtok, w_mem_kv,
           og_dil, og_moba, og_mem, w_out, g_ffn, w_peer_q, peer_subkeys_1, peer_subkeys_2, peer_u,
           peer_v):
    h = x
    for layer in range(g_mix.shape[0]):
        h = _layer(h, mem, g_mix[layer], w_in[layer], qg_dil[layer], kg_dil[layer], qg_moba[layer],
                   kg_moba[layer], qg_mem[layer], kg_mem[layer], g_memtok[layer], w_mem_kv[layer],
                   og_dil[layer], og_moba[layer], og_mem[layer], w_out[layer], g_ffn[layer],
                   w_peer_q[layer], peer_subkeys_1[layer], peer_subkeys_2[layer], peer_u[layer],
                   peer_v[layer])
    return h
```

```python
import functools
import math

import jax
import jax.numpy as jnp
from jax import lax
from jax.experimental import pallas as pl
from jax.experimental.pallas import tpu as pltpu

F32 = jnp.float32
BF16 = jnp.bfloat16
I32 = jnp.int32
U32 = jnp.uint32

LANES = 128
D_MODEL = 1024
N_HEADS_DIL = 6
N_HEADS_MOBA = 6
N_HEADS_MEM = 4
HEAD_DIM = 64
W_DIL = N_HEADS_DIL * HEAD_DIM
W_MOBA = N_HEADS_MOBA * HEAD_DIM
W_MEM = N_HEADS_MEM * HEAD_DIM
IN_WIDTH = 3 * W_DIL + 3 * W_MOBA + W_MEM
DIL_CONFIGS = ((128, 1), (512, 4), (2048, 16))
BAND_BLOCK = 128
MOBA_BLOCK = 256
MOBA_TOPK = 3
MEM_LEN = 256
PEER_HEADS = 8
PEER_NKEYS = 128
PEER_EXPERTS = PEER_NKEYS * PEER_NKEYS
PEER_TOPK = 16
PEER_DKEY = 256
PEER_SLOTS = PEER_HEADS * PEER_TOPK
RMS_EPS = 1e-6
NEG_INF = -1e30
SCALE = 1.0 / math.sqrt(HEAD_DIM)
SQRT_HALF = math.sqrt(0.5)

VMEM_LIMIT = 56 * 1024 * 1024

_NT = (((1,), (1,)), ((), ()))


def _shr(a, bits):
    return lax.shift_right_logical(a, jnp.full(a.shape, bits, a.dtype))


def _cparams(sem):
    return pltpu.CompilerParams(dimension_semantics=sem, vmem_limit_bytes=VMEM_LIMIT)


def _row_rms(a, gain):
    return a * lax.rsqrt(jnp.mean(a * a, axis=-1, keepdims=True) + RMS_EPS) * gain


def _group_mean_sq(p, group):
    w = p.shape[-1]
    bits = group.bit_length() - 1
    gi = _shr(lax.broadcasted_iota(I32, (w, w), 0), bits)
    gj = _shr(lax.broadcasted_iota(I32, (w, w), 1), bits)
    ones_bd = jnp.where(gi == gj, 1.0, 0.0).astype(BF16)
    p2 = p * p
    hi = p2.astype(BF16)
    lo = (p2 - hi.astype(F32)).astype(BF16)
    ss = (jnp.dot(hi, ones_bd, preferred_element_type=F32)
          + jnp.dot(lo, ones_bd, preferred_element_type=F32))
    return ss * (1.0 / group)


def _head_rms(p, gain):
    return p * lax.rsqrt(_group_mean_sq(p, HEAD_DIM) + RMS_EPS) * gain


def _memkv_kernel(mem_ref, g_ref, w_ref, kg_ref, k_ref, v_ref):
    hn = _row_rms(mem_ref[...], g_ref[...]).astype(BF16)
    kv = jnp.dot(hn, w_ref[...], preferred_element_type=F32)
    k_ref[...] = _head_rms(kv[:, :W_MEM], kg_ref[...])
    v_ref[...] = kv[:, W_MEM:]


def _memkv(mem, g_memtok, w_kv, kg_mem):
    b = mem.shape[0]
    return pl.pallas_call(
        _memkv_kernel,
        grid=(b,),
        in_specs=[
            pl.BlockSpec((None, MEM_LEN, D_MODEL), lambda i: (i, 0, 0)),
            pl.BlockSpec((1, D_MODEL), lambda i: (0, 0)),
            pl.BlockSpec((D_MODEL, 2 * W_MEM), lambda i: (0, 0)),
            pl.BlockSpec((1, W_MEM), lambda i: (0, 0)),
        ],
        out_specs=[pl.BlockSpec((None, MEM_LEN, W_MEM), lambda i: (i, 0, 0))] * 2,
        out_shape=[jax.ShapeDtypeStruct((b, MEM_LEN, W_MEM), F32)] * 2,
        compiler_params=_cparams(("parallel",)),
        name="memkv",
    )(mem, g_memtok, w_kv, kg_mem)


INPROJ_TM = 512


def _inproj_kernel(x_ref, g_ref, w_ref, gqd, gkd, gqb, gkb, gqm, qd, kd, vd, qb, kb, vb, qm):
    hb = _row_rms(x_ref[...], g_ref[...]).astype(BF16)

    def seg(lo, width):
        return jnp.dot(hb, w_ref[:, lo:lo + width], preferred_element_type=F32)

    qd[...] = _head_rms(seg(0, W_DIL), gqd[...])
    kd[...] = _head_rms(seg(W_DIL, W_DIL), gkd[...])
    vd[...] = seg(2 * W_DIL, W_DIL)
    base = 3 * W_DIL
    qb[...] = _head_rms(seg(base, W_MOBA), gqb[...])
    kb[...] = _head_rms(seg(base + W_MOBA, W_MOBA), gkb[...])
    vb[...] = seg(base + 2 * W_MOBA, W_MOBA)
    qm[...] = _head_rms(seg(base + 3 * W_MOBA, W_MEM), gqm[...])


def _inproj(x2, g_mix, w_in, gqd, gkd, gqb, gkb, gqm):
    t = x2.shape[0]
    tm = INPROJ_TM
    row = lambda w: pl.BlockSpec((1, w), lambda i: (0, 0))
    tile = lambda w: pl.BlockSpec((tm, w), lambda i: (i, 0))
    widths = (W_DIL, W_DIL, W_DIL, W_MOBA, W_MOBA, W_MOBA, W_MEM)
    return pl.pallas_call(
        _inproj_kernel,
        grid=(t // tm,),
        in_specs=[tile(D_MODEL), row(D_MODEL), pl.BlockSpec((D_MODEL, IN_WIDTH), lambda i: (0, 0)),
                  row(W_DIL), row(W_DIL), row(W_MOBA), row(W_MOBA), row(W_MEM)],
        out_specs=[tile(w) for w in widths],
        out_shape=[jax.ShapeDtypeStruct((t, w), F32) for w in widths],
        compiler_params=_cparams(("parallel",)),
        name="inproj",
    )(x2, g_mix, w_in, gqd, gkd, gqb, gkb, gqm)


DIL_PAD = BAND_BLOCK * max(d for _, d in DIL_CONFIGS)
DIL_MIX_ROWS = 256


def _dil_kernel(q_ref, k_ref, v_ref, ns_ref, o_ref, kpad, vpad, ob0, ob1, ob2, lb0, lb1, lb2):
    s_len = q_ref.shape[0]
    zeros = jnp.zeros((DIL_PAD, LANES), F32)
    kpad[0:DIL_PAD, :] = zeros
    vpad[0:DIL_PAD, :] = zeros
    kpad[DIL_PAD:, :] = k_ref[...]
    vpad[DIL_PAD:, :] = v_ref[...]

    blk = BAND_BLOCK
    head0 = lax.broadcasted_iota(I32, (blk, LANES), 1) < HEAD_DIM
    ql = lax.broadcasted_iota(I32, (blk, 2 * blk), 0)
    kl = lax.broadcasted_iota(I32, (blk, 2 * blk), 1)
    delta = blk + ql - kl
    obs = (ob0, ob1, ob2)
    lbs = (lb0, lb1, lb2)

    for c, (window, dil) in enumerate(DIL_CONFIGS):
        reach = window // dil
        n_blk = s_len // dil // blk
        in_band = (delta >= 0) & (delta <= reach)
        dist = (delta * dil).astype(F32)
        ob, lb = obs[c], lbs[c]

        def rows(start, size, dil=dil):
            return pl.ds(start, size) if dil == 1 else pl.ds(start, size, stride=dil)

        def block_body(n, r, dil=dil, in_band=in_band, dist=dist, ob=ob, lb=lb, rows=rows):
            q_start = r + dil * blk * n
            k_start = DIL_PAD + r + dil * blk * (n - 1)
            qv = q_ref[rows(q_start, blk), :]
            kv = kpad[rows(k_start, 2 * blk), :].astype(BF16)
            vv = vpad[rows(k_start, 2 * blk), :].astype(BF16)
            valid = in_band & (kl >= jnp.where(n == 0, blk, 0))
            outs, lses = [], []
            for h in range(2):
                qh = jnp.where(head0 if h == 0 else ~head0, qv, 0.0).astype(BF16)
                s = lax.dot_general(qh, kv, _NT, preferred_element_type=F32) * SCALE
                s = jnp.where(valid, s + ns_ref[h] * dist, NEG_INF)
                m = jnp.max(s, axis=-1, keepdims=True)
                p = jnp.exp(s - m)
                den = jnp.sum(p, axis=-1, keepdims=True)
                lses.append(jnp.broadcast_to(m + jnp.log(den), (blk, LANES)))
                outs.append(jnp.dot(p.astype(BF16), vv, preferred_element_type=F32) / den)
            ob[rows(q_start, blk), :] = jnp.where(head0, outs[0], outs[1])
            lb[rows(q_start, blk), :] = jnp.where(head0, lses[0], lses[1])

        def residue_body(r, carry, n_blk=n_blk, block_body=block_body):
            def inner(n, c2):
                block_body(n, r)
                return c2
            return lax.fori_loop(0, n_blk, inner, carry)

        lax.fori_loop(0, dil, residue_body, 0)

    def mix(i, carry):
        sl = pl.ds(pl.multiple_of(i * DIL_MIX_ROWS, DIL_MIX_ROWS), DIL_MIX_ROWS)
        l0, l1, l2 = lb0[sl, :], lb1[sl, :], lb2[sl, :]
        mx = jnp.maximum(jnp.maximum(l0, l1), l2)
        e0, e1, e2 = jnp.exp(l0 - mx), jnp.exp(l1 - mx), jnp.exp(l2 - mx)
        tot = e0 + e1 + e2
        o_ref[sl, :] = (e0 / tot) * ob0[sl, :] + (e1 / tot) * ob1[sl, :] + (e2 / tot) * ob2[sl, :]
        return carry

    lax.fori_loop(0, s_len // DIL_MIX_ROWS, mix, 0)


def _dilated(q, k, v, neg_slopes):
    b, s_len, w = q.shape
    n_pair = w // LANES
    slab = pl.BlockSpec((None, s_len, LANES), lambda i, p: (i, 0, p))
    buf = lambda rows: pltpu.VMEM((rows, LANES), F32)
    return pl.pallas_call(
        _dil_kernel,
        grid=(b, n_pair),
        in_specs=[slab, slab, slab,
                  pl.BlockSpec((None, 2, 1, 2 * BAND_BLOCK), lambda i, p: (p, 0, 0, 0))],
        out_specs=slab,
        out_shape=jax.ShapeDtypeStruct((b, s_len, w), F32),
        scratch_shapes=[buf(DIL_PAD + s_len), buf(DIL_PAD + s_len)] + [buf(s_len)] * 6,
        compiler_params=_cparams(("parallel", "parallel")),
        name="dilated",
    )(q, k, v, neg_slopes)


def _split_bf16(a):
    hi = a.astype(BF16)
    lo = (a - hi.astype(F32)).astype(BF16)
    return hi, lo


def _split3_bf16(a):
    t0 = a.astype(BF16).astype(F32)
    t1 = (a - t0).astype(BF16).astype(F32)
    t2 = (a - t0 - t1).astype(BF16).astype(F32)
    return t0, t1, t2


MOBA_QB = 16
MOBA_KB = 19


def _moba_kernel(q_ref, k_ref, v_ref, ns_ref, o_ref, qa_ref, kb_ref, kx_ref, vt_ref, acc_ref):
    s_len = q_ref.shape[0]
    blk = MOBA_BLOCK
    n_blk = s_len // blk
    lane = lax.broadcasted_iota(I32, (blk, LANES), 1)
    head0 = lane < HEAD_DIM
    pos0 = lax.broadcasted_iota(I32, (blk, LANES), 0).astype(F32)

    kb_ref[...] = k_ref[...].astype(BF16)

    k_mean = jnp.concatenate(
        [jnp.sum(k_ref[j * blk:(j + 1) * blk, :], axis=0, keepdims=True) for j in range(n_blk)],
        axis=0) * (1.0 / blk)
    q_hi, q_lo = _split_bf16(q_ref[...])

    blk_id = lax.broadcasted_iota(I32, (n_blk, s_len), 0)
    n_past = _shr(lax.broadcasted_iota(I32, (n_blk, s_len), 1), blk.bit_length() - 1)
    past = blk_id < n_past
    blk_f = blk_id.astype(F32)
    pad_rows = LANES - n_blk

    for h in range(2):
        hm = (lax.broadcasted_iota(I32, (n_blk, LANES), 1) < HEAD_DIM) == (h == 0)
        km_hi, km_lo = _split_bf16(jnp.where(hm, k_mean, 0.0))
        gate = (lax.dot_general(km_hi, q_hi, _NT, preferred_element_type=F32)
                + lax.dot_general(km_hi, q_lo, _NT, preferred_element_type=F32)
                + lax.dot_general(km_lo, q_hi, _NT, preferred_element_type=F32))
        g = jnp.where(past, gate, NEG_INF)
        sel = jnp.zeros((n_blk, s_len), F32)
        for _ in range(MOBA_TOPK):
            m = jnp.max(g, axis=0, keepdims=True)
            first = jnp.min(jnp.where(g == m, blk_f, float(n_blk)), axis=0, keepdims=True)
            pick = blk_f == first
            sel = jnp.where(pick, 1.0, sel)
            g = jnp.where(pick, -jnp.inf, g)
        not_sel = jnp.where(past, 1.0 - sel, jnp.where(blk_id == n_past, 0.0, 1.0))
        not_sel = jnp.concatenate([not_sel, jnp.zeros((pad_rows, s_len), F32)], axis=0)
        ns = ns_ref[h][:, 0:LANES]
        for i in range(n_blk):
            rows = slice(i * blk, (i + 1) * blk)
            qh = jnp.where(head0 if h == 0 else ~head0, q_ref[rows, :], 0.0) * SCALE
            qa_ref[h, rows, 0:LANES] = qh.astype(BF16)
            t_pos = pos0 + float(i * blk)
            q0, q1, q2 = _split3_bf16(ns * t_pos)
            k0, k1, k2 = _split3_bf16(-ns * t_pos)
            qx = not_sel[:, rows].T
            kx = jnp.where(lane == i, NEG_INF, 0.0)
            for d, (qt, kt) in enumerate(((q0, k0), (q1, k1), (q2, k2))):
                qx = jnp.where(lane == MOBA_QB + d, qt, jnp.where(lane == MOBA_KB + d, 1.0, qx))
                kx = jnp.where(lane == MOBA_QB + d, 1.0, jnp.where(lane == MOBA_KB + d, kt, kx))
            qa_ref[h, rows, LANES:] = qx.astype(BF16)
            kx_ref[h, rows, :] = kx.astype(BF16)

    top_rows = lax.broadcasted_iota(I32, (LANES, blk), 0) < HEAD_DIM
    for j in range(n_blk):
        vt = v_ref[j * blk:(j + 1) * blk, :].T
        vt_ref[0, :, j * blk:(j + 1) * blk] = jnp.where(top_rows, vt, 1.0).astype(BF16)
        vt_ref[1, :, j * blk:(j + 1) * blk] = jnp.where(top_rows, 1.0, vt).astype(BF16)

    key_i = lax.broadcasted_iota(I32, (blk, blk), 0)
    qry_i = lax.broadcasted_iota(I32, (blk, blk), 1)
    causal = key_i <= qry_i

    def q_tile(i, carry):
        r0 = pl.multiple_of(i * blk, blk)
        qas = [jnp.concatenate([qa_ref[h, pl.ds(r0, blk), 0:LANES], qa_ref[h, pl.ds(r0, blk), LANES:]],
                               axis=1) for h in range(2)]

        def scores(c0, h):
            k_aug = jnp.concatenate([kb_ref[pl.ds(c0, blk), :], kx_ref[h, pl.ds(c0, blk), :]], axis=1)
            return lax.dot_general(k_aug, qas[h], _NT, preferred_element_type=F32)

        ms = []
        for h in range(2):
            s = jnp.where(causal, scores(r0, h), NEG_INF)
            m0 = jnp.max(s, axis=0, keepdims=True)
            p = jnp.exp(s - m0)
            acc_ref[h] = jnp.dot(vt_ref[h, :, pl.ds(r0, blk)], p.astype(BF16),
                                 preferred_element_type=F32)
            ms.append(m0)

        def past_block(j, ms):
            c0 = pl.multiple_of(j * blk, blk)
            new_ms = []
            for h in range(2):
                sj = scores(c0, h)
                m_new = jnp.maximum(ms[h], jnp.max(sj, axis=0, keepdims=True))
                alpha = jnp.exp(ms[h] - m_new)
                pj = jnp.exp(sj - m_new)
                acc_ref[h] = alpha * acc_ref[h] + jnp.dot(vt_ref[h, :, pl.ds(c0, blk)], pj.astype(BF16),
                                                          preferred_element_type=F32)
                new_ms.append(m_new)
            return tuple(new_ms)

        def two_past_blocks(jj, ms):
            return past_block(2 * jj + 1, past_block(2 * jj, ms))

        n_two = lax.shift_right_logical(i, 1)
        ms = lax.fori_loop(0, n_two, two_past_blocks, tuple(ms))
        lax.fori_loop(2 * n_two, i, past_block, ms)
        a0, a1 = acc_ref[0], acc_ref[1]
        out_t = jnp.where(top_rows, a0 / a0[HEAD_DIM:HEAD_DIM + 1, :], a1 / a1[0:1, :])
        o_ref[pl.ds(r0, blk), :] = out_t.T
        return carry

    lax.fori_loop(0, n_blk, q_tile, 0)


def _moba(q, k, v, neg_slopes):
    b, s_len, w = q.shape
    n_pair = w // LANES
    slab = pl.BlockSpec((None, s_len, LANES), lambda i, p: (i, 0, p))
    return pl.pallas_call(
        _moba_kernel,
        grid=(b, n_pair),
        in_specs=[slab, slab, slab,
                  pl.BlockSpec((None, 2, 1, MOBA_BLOCK), lambda i, p: (p, 0, 0, 0))],
        out_specs=slab,
        out_shape=jax.ShapeDtypeStruct((b, s_len, w), F32),
        scratch_shapes=[pltpu.VMEM((2, s_len, 2 * LANES), BF16),
                        pltpu.VMEM((s_len, LANES), BF16),
                        pltpu.VMEM((2, s_len, LANES), BF16),
                        pltpu.VMEM((2, LANES, s_len), BF16),
                        pltpu.VMEM((2, LANES, MOBA_BLOCK), F32)],
        compiler_params=_cparams(("parallel", "parallel")),
        name="moba",
    )(q, k, v, neg_slopes)


MEMATTN_TQ = 512


def _memattn_kernel(q_ref, k_ref, v_ref, o_ref):
    q = q_ref[...]
    kb = k_ref[...].astype(BF16)
    vb = v_ref[...].astype(BF16)
    head = _shr(lax.broadcasted_iota(I32, q.shape, 1), HEAD_DIM.bit_length() - 1)
    out = jnp.zeros(q.shape, F32)
    for h in range(N_HEADS_MEM):
        qh = jnp.where(head == h, q, 0.0).astype(BF16)
        s = lax.dot_general(qh, kb, _NT, preferred_element_type=F32) * SCALE
        m = jnp.max(s, axis=-1, keepdims=True)
        p = jnp.exp(s - m)
        den = jnp.sum(p, axis=-1, keepdims=True)
        oh = jnp.dot(p.astype(BF16), vb, preferred_element_type=F32) / den
        out = jnp.where(head == h, oh, out)
    o_ref[...] = out


def _memattn(q, k, v):
    b, s_len, w = q.shape
    tq = MEMATTN_TQ
    return pl.pallas_call(
        _memattn_kernel,
        grid=(b, s_len // tq),
        in_specs=[pl.BlockSpec((None, tq, w), lambda i, j: (i, j, 0)),
                  pl.BlockSpec((None, MEM_LEN, w), lambda i, j: (i, 0, 0)),
                  pl.BlockSpec((None, MEM_LEN, w), lambda i, j: (i, 0, 0))],
        out_specs=pl.BlockSpec((None, tq, w), lambda i, j: (i, j, 0)),
        out_shape=jax.ShapeDtypeStruct((b, s_len, w), F32),
        compiler_params=_cparams(("parallel", "parallel")),
        name="memattn",
    )(q, k, v)


OUTPROJ_TM = 512


def _outproj_kernel(od_ref, ob_ref, om_ref, x_ref, gd, gb, gm, wo_ref, gf, wq_ref, k1_ref, k2_ref,
                    x1_ref, hn_ref, s1_ref, s2_ref):
    y = jnp.dot(_row_rms(od_ref[...], gd[...]).astype(BF16), wo_ref[0:W_DIL, :],
                preferred_element_type=F32)
    y += jnp.dot(_row_rms(ob_ref[...], gb[...]).astype(BF16), wo_ref[W_DIL:W_DIL + W_MOBA, :],
                 preferred_element_type=F32)
    y += jnp.dot(_row_rms(om_ref[...], gm[...]).astype(BF16), wo_ref[W_DIL + W_MOBA:, :],
                 preferred_element_type=F32)
    x1 = x_ref[...] + y
    x1_ref[...] = x1
    hb = _row_rms(x1, gf[...]).astype(BF16)
    hn_ref[...] = hb
    half = PEER_DKEY // 2
    for h in range(PEER_HEADS):
        qh = jnp.dot(hb, wq_ref[:, h * PEER_DKEY:(h + 1) * PEER_DKEY],
                     preferred_element_type=F32).astype(BF16)
        s1_ref[h] = lax.dot_general(k1_ref[h], qh[:, :half], _NT, preferred_element_type=F32)
        s2_ref[h] = lax.dot_general(k2_ref[h], qh[:, half:], _NT, preferred_element_type=F32)


def _outproj(o_dil, o_moba, o_mem, x2, og_dil, og_moba, og_mem, w_out, g_ffn, w_q, sub1, sub2):
    t = x2.shape[0]
    tm = OUTPROJ_TM
    row = lambda w: pl.BlockSpec((1, w), lambda i: (0, 0))
    tile = lambda w: pl.BlockSpec((tm, w), lambda i: (i, 0))
    full = lambda shape: pl.BlockSpec(shape, lambda i: (0,) * len(shape))
    score = pl.BlockSpec((PEER_HEADS, PEER_NKEYS, tm), lambda i: (0, 0, i))
    half = PEER_DKEY // 2
    return pl.pallas_call(
        _outproj_kernel,
        grid=(t // tm,),
        in_specs=[tile(W_DIL), tile(W_MOBA), tile(W_MEM), tile(D_MODEL),
                  row(W_DIL), row(W_MOBA), row(W_MEM), full((D_MODEL, D_MODEL)), row(D_MODEL),
                  full((D_MODEL, PEER_HEADS * PEER_DKEY)),
                  full((PEER_HEADS, PEER_NKEYS, half)), full((PEER_HEADS, PEER_NKEYS, half))],
        out_specs=[tile(D_MODEL), tile(D_MODEL), score, score],
        out_shape=[jax.ShapeDtypeStruct((t, D_MODEL), F32),
                   jax.ShapeDtypeStruct((t, D_MODEL), BF16),
                   jax.ShapeDtypeStruct((PEER_HEADS, PEER_NKEYS, t), F32),
                   jax.ShapeDtypeStruct((PEER_HEADS, PEER_NKEYS, t), F32)],
        compiler_params=_cparams(("parallel",)),
        name="outproj",
    )(o_dil, o_moba, o_mem, x2, og_dil, og_moba, og_mem, w_out, g_ffn, w_q, sub1, sub2)


TOPK_TM = 512


def _top_rows(s, k):
    n_rows = s.shape[0]
    rid = lax.broadcasted_iota(I32, s.shape, 0).astype(F32)
    vals, idxs = [], []
    for _ in range(k):
        m = jnp.max(s, axis=0, keepdims=True)
        first = jnp.min(jnp.where(s == m, rid, float(n_rows)), axis=0, keepdims=True)
        vals.append(m)
        idxs.append(first)
        s = jnp.where(rid == first, -jnp.inf, s)
    return jnp.concatenate(vals, axis=0), jnp.concatenate(idxs, axis=0)


def _pick_rows(table, sel):
    out = jnp.zeros(sel.shape, table.dtype)
    for j in range(table.shape[0]):
        out = jnp.where(sel == j, table[j:j + 1, :], out)
    return out.astype(I32)


def _candidates(v1, v2):
    sub = lax.broadcasted_iota(I32, (8, v1.shape[1]), 0)
    pieces = [v1[0:1, :] + v2]
    for j1 in range(1, 8):
        pieces.append(jnp.where(sub < PEER_TOPK // (j1 + 1), v1[j1:j1 + 1, :] + v2[0:8, :], -jnp.inf))
    pieces.append(v1[8:16, :] + v2[0:1, :])
    return jnp.concatenate(pieces, axis=0)


def _candidate_coords(row):
    low = row & 7
    j1 = jnp.where(row < 16, 0, jnp.where(row >= 72, 8 + low, _shr(row, 3) - 1))
    j2 = jnp.where(row < 16, row, jnp.where(row >= 72, 0, low))
    return j1, j2


def _peertopk_kernel(s1_ref, s2_ref, i1_ref, i2_ref, g_ref):
    n_col = s1_ref.shape[2] // LANES

    def head_body(h, carry):
        for c in range(n_col):
            cols = slice(c * LANES, (c + 1) * LANES)
            v1, i1 = _top_rows(s1_ref[h, :, cols], PEER_TOPK)
            v2, i2 = _top_rows(s2_ref[h, :, cols], PEER_TOPK)
            top_s, pos = _top_rows(_candidates(v1, v2), PEER_TOPK)
            e = jnp.exp(top_s - top_s[0:1, :])
            g_ref[h, :, cols] = e / jnp.sum(e, axis=0, keepdims=True)
            j1, j2 = _candidate_coords(pos.astype(I32))
            i1_ref[h, :, cols] = _pick_rows(i1, j1)
            i2_ref[h, :, cols] = _pick_rows(i2, j2)
        return carry

    lax.fori_loop(0, PEER_HEADS, head_body, 0)


def _peertopk(s1, s2):
    t = s1.shape[2]
    tm = TOPK_TM
    score = pl.BlockSpec((PEER_HEADS, PEER_NKEYS, tm), lambda i: (0, 0, i))
    slot = pl.BlockSpec((PEER_HEADS, PEER_TOPK, tm), lambda i: (0, 0, i))
    return pl.pallas_call(
        _peertopk_kernel,
        grid=(t // tm,),
        in_specs=[score, score],
        out_specs=[slot, slot, slot],
        out_shape=[jax.ShapeDtypeStruct((PEER_HEADS, PEER_TOPK, t), I32),
                   jax.ShapeDtypeStruct((PEER_HEADS, PEER_TOPK, t), I32),
                   jax.ShapeDtypeStruct((PEER_HEADS, PEER_TOPK, t), F32)],
        compiler_params=_cparams(("parallel",)),
        name="peertopk",
    )(s1, s2)


PEER_TM = 512
PEER_TE = 1024
PEER_HALF = PEER_TM // 2
PEER_PITCH = PEER_HALF + 8
PEER_BUILD_UNROLL = 8


def _bf16_bits(a):
    return lax.bitcast_convert_type(a.astype(BF16).astype(F32), U32)


def _peerffn_kernel(h_ref, x1_ref, i1_ref, i2_ref, g_ref, u_ref, v_ref, o_ref, gate_ref):
    c = pl.program_id(1)
    slabs = PEER_TE // PEER_NKEYS

    @pl.when(c == 0)
    def _build_gate_matrix():
        key_id = lax.broadcasted_iota(I32, (PEER_NKEYS, PEER_SLOTS), 0)

        def gate_matrix(t):
            i1 = jnp.broadcast_to(i1_ref[pl.ds(t, 1), :], key_id.shape)
            i2 = jnp.broadcast_to(i2_ref[pl.ds(t, 1), :], key_id.shape)
            g = jnp.broadcast_to(g_ref[pl.ds(t, 1), :], key_id.shape)
            lhs = jnp.where(key_id == i1, g, 0.0).astype(BF16)
            rhs = jnp.where(key_id == i2, 1.0, 0.0).astype(BF16)
            return lax.dot_general(lhs, rhs, _NT, preferred_element_type=F32)

        def token_pair(t, carry):
            lo = lax.shift_right_logical(_bf16_bits(gate_matrix(t)), jnp.uint32(16))
            hi = _bf16_bits(gate_matrix(t + PEER_HALF))
            gate_ref[pl.ds(t, PEER_NKEYS, stride=PEER_PITCH), :] = hi | lo
            return carry

        lax.fori_loop(0, PEER_HALF, token_pair, 0, unroll=PEER_BUILD_UNROLL)

    a = lax.dot_general(h_ref[...], u_ref[...], _NT, preferred_element_type=F32)
    ws = []
    for j in range(slabs):
        start = pl.multiple_of((c * slabs + j) * PEER_PITCH, 8)
        word = gate_ref[pl.ds(start, PEER_HALF), :]
        g_lo = lax.bitcast_convert_type(lax.shift_left(word, jnp.uint32(16)), F32)
        g_hi = lax.bitcast_convert_type(word & jnp.uint32(0xFFFF0000), F32)
        gj = jnp.concatenate([g_lo, g_hi], axis=0)
        aj = a[:, j * PEER_NKEYS:(j + 1) * PEER_NKEYS]
        ws.append((0.5 * aj * (1.0 + lax.erf(aj * SQRT_HALF)) * gj).astype(BF16))
    y = jnp.dot(jnp.concatenate(ws, axis=1), v_ref[...], preferred_element_type=F32)

    @pl.when(c == 0)
    def _first():
        o_ref[...] = x1_ref[...] + y

    @pl.when(c > 0)
    def _rest():
        o_ref[...] += y


def _peerffn(hn, x1, i1, i2, gate, u, v):
    t = hn.shape[0]
    tm, te = PEER_TM, PEER_TE
    tile = lambda w: pl.BlockSpec((tm, w), lambda i, c: (i, 0))
    chunk = pl.BlockSpec((te, D_MODEL), lambda i, c: (c, 0))
    return pl.pallas_call(
        _peerffn_kernel,
        grid=(t // tm, PEER_EXPERTS // te),
        in_specs=[tile(D_MODEL), tile(D_MODEL), tile(PEER_SLOTS), tile(PEER_SLOTS), tile(PEER_SLOTS),
                  chunk, chunk],
        out_specs=tile(D_MODEL),
        out_shape=jax.ShapeDtypeStruct((t, D_MODEL), F32),
        scratch_shapes=[pltpu.VMEM((PEER_NKEYS * PEER_PITCH, LANES), U32)],
        compiler_params=_cparams(("parallel", "arbitrary")),
        name="peerffn",
    )(hn, x1, i1, i2, gate, u, v)


def _neg_slope_rows(slopes, width):
    n = slopes.shape[0]
    return jnp.broadcast_to((-slopes).reshape(n // 2, 2, 1, 1), (n // 2, 2, 1, width))


def _tiled_gain(g, reps):
    return jnp.tile(g, reps)[None, :]


def _layer(x, mem, g_mix, w_in, qg_dil, kg_dil, qg_moba, kg_moba, qg_mem, kg_mem, g_memtok, w_mem_kv,
           og_dil, og_moba, og_mem, w_out, g_ffn, w_peer_q, sub1, sub2, peer_u, peer_v):
    b, s_len, d = x.shape
    t = b * s_len
    n_mix = N_HEADS_DIL + N_HEADS_MOBA
    slopes = jnp.exp2(-8.0 * jnp.arange(1, n_mix + 1, dtype=F32) / n_mix)
    ns_dil = _neg_slope_rows(slopes[0::2], 2 * BAND_BLOCK)
    ns_moba = _neg_slope_rows(slopes[1::2], MOBA_BLOCK)

    x2 = x.reshape(t, d)
    k_m, v_m = _memkv(mem, g_memtok[None, :], w_mem_kv.astype(BF16), _tiled_gain(kg_mem, N_HEADS_MEM))
    q_d, k_d, v_d, q_b, k_b, v_b, q_m = _inproj(
        x2, g_mix[None, :], w_in.astype(BF16),
        _tiled_gain(qg_dil, N_HEADS_DIL), _tiled_gain(kg_dil, N_HEADS_DIL),
        _tiled_gain(qg_moba, N_HEADS_MOBA), _tiled_gain(kg_moba, N_HEADS_MOBA),
        _tiled_gain(qg_mem, N_HEADS_MEM))
    seq = lambda a: a.reshape(b, s_len, a.shape[-1])
    o_dil = _dilated(seq(q_d), seq(k_d), seq(v_d), ns_dil)
    o_moba = _moba(seq(q_b), seq(k_b), seq(v_b), ns_moba)
    o_mem = _memattn(seq(q_m), k_m, v_m)
    x1, hn, s1, s2 = _outproj(
        o_dil.reshape(t, W_DIL), o_moba.reshape(t, W_MOBA), o_mem.reshape(t, W_MEM), x2,
        og_dil[None, :], og_moba[None, :], og_mem[None, :], w_out.astype(BF16), g_ffn[None, :],
        w_peer_q.astype(BF16), sub1.astype(BF16), sub2.astype(BF16))
    i1, i2, gate = _peertopk(s1, s2)
    slots = lambda a: a.reshape(PEER_SLOTS, t).T
    out = _peerffn(hn, x1, slots(i1), slots(i2), slots(gate), peer_u.astype(BF16), peer_v.astype(BF16))
    return out.reshape(b, s_len, d)


def kernel(x, mem, g_mix, w_in, qg_dil, kg_dil, qg_moba, kg_moba, qg_mem, kg_mem, g_memtok, w_mem_kv,
           og_dil, og_moba, og_mem, w_out, g_ffn, w_peer_q, peer_subkeys_1, peer_subkeys_2, peer_u,
           peer_v):
    h = x
    for layer in range(g_mix.shape[0]):
        h = _layer(h, mem, g_mix[layer], w_in[layer], qg_dil[layer], kg_dil[layer], qg_moba[layer],
                   kg_moba[layer], qg_mem[layer], kg_mem[layer], g_memtok[layer], w_mem_kv[layer],
                   og_dil[layer], og_moba[layer], og_mem[layer], w_out[layer], g_ffn[layer],
                   w_peer_q[layer], peer_subkeys_1[layer], peer_subkeys_2[layer], peer_u[layer],
                   peer_v[layer])
    return h
```

```python
import functools
import math

import jax
import jax.numpy as jnp
from jax import lax
from jax.experimental import pallas as pl
from jax.experimental.pallas import tpu as pltpu

F32 = jnp.float32
BF16 = jnp.bfloat16
I32 = jnp.int32
U32 = jnp.uint32

LANES = 128
D_MODEL = 1024
N_HEADS_DIL = 6
N_HEADS_MOBA = 6
N_HEADS_MEM = 4
HEAD_DIM = 64
W_DIL = N_HEADS_DIL * HEAD_DIM
W_MOBA = N_HEADS_MOBA * HEAD_DIM
W_MEM = N_HEADS_MEM * HEAD_DIM
IN_WIDTH = 3 * W_DIL + 3 * W_MOBA + W_MEM
DIL_CONFIGS = ((128, 1), (512, 4), (2048, 16))
BAND_BLOCK = 128
MOBA_BLOCK = 256
MOBA_TOPK = 3
MEM_LEN = 256
PEER_HEADS = 8
PEER_NKEYS = 128
PEER_EXPERTS = PEER_NKEYS * PEER_NKEYS
PEER_TOPK = 16
PEER_DKEY = 256
PEER_SLOTS = PEER_HEADS * PEER_TOPK
RMS_EPS = 1e-6
NEG_INF = -1e30
SCALE = 1.0 / math.sqrt(HEAD_DIM)
SQRT_HALF = math.sqrt(0.5)

VMEM_LIMIT = 56 * 1024 * 1024

_NT = (((1,), (1,)), ((), ()))


def _shr(a, bits):
    return lax.shift_right_logical(a, jnp.full(a.shape, bits, a.dtype))


def _shr_scalar(a, bits):
    return lax.shift_right_logical(a, jnp.int32(bits))


def _cparams(sem):
    return pltpu.CompilerParams(dimension_semantics=sem, vmem_limit_bytes=VMEM_LIMIT)


def _row_rms(a, gain):
    return a * lax.rsqrt(jnp.mean(a * a, axis=-1, keepdims=True) + RMS_EPS) * gain


def _group_mean_sq(p, group):
    w = p.shape[-1]
    bits = group.bit_length() - 1
    gi = _shr(lax.broadcasted_iota(I32, (w, w), 0), bits)
    gj = _shr(lax.broadcasted_iota(I32, (w, w), 1), bits)
    ones_bd = jnp.where(gi == gj, 1.0, 0.0).astype(BF16)
    p2 = p * p
    hi = p2.astype(BF16)
    lo = (p2 - hi.astype(F32)).astype(BF16)
    ss = (jnp.dot(hi, ones_bd, preferred_element_type=F32)
          + jnp.dot(lo, ones_bd, preferred_element_type=F32))
    return ss * (1.0 / group)


def _head_rms(p, gain):
    return p * lax.rsqrt(_group_mean_sq(p, HEAD_DIM) + RMS_EPS) * gain


def _memkv_kernel(mem_ref, g_ref, w_ref, kg_ref, k_ref, v_ref):
    hn = _row_rms(mem_ref[...], g_ref[...]).astype(BF16)
    kv = jnp.dot(hn, w_ref[...], preferred_element_type=F32)
    k_ref[...] = _head_rms(kv[:, :W_MEM], kg_ref[...])
    v_ref[...] = kv[:, W_MEM:]


def _memkv(mem, g_memtok, w_kv, kg_mem):
    b = mem.shape[0]
    return pl.pallas_call(
        _memkv_kernel,
        grid=(b,),
        in_specs=[
            pl.BlockSpec((None, MEM_LEN, D_MODEL), lambda i: (i, 0, 0)),
            pl.BlockSpec((1, D_MODEL), lambda i: (0, 0)),
            pl.BlockSpec((D_MODEL, 2 * W_MEM), lambda i: (0, 0)),
            pl.BlockSpec((1, W_MEM), lambda i: (0, 0)),
        ],
        out_specs=[pl.BlockSpec((None, MEM_LEN, W_MEM), lambda i: (i, 0, 0))] * 2,
        out_shape=[jax.ShapeDtypeStruct((b, MEM_LEN, W_MEM), F32)] * 2,
        compiler_params=_cparams(("parallel",)),
        name="memkv",
    )(mem, g_memtok, w_kv, kg_mem)


INPROJ_TM = 512


def _inproj_kernel(x_ref, g_ref, w_ref, gqd, gkd, gqb, gkb, gqm, qd, kd, vd, qb, kb, vb, qm):
    hb = _row_rms(x_ref[...], g_ref[...]).astype(BF16)

    def seg(lo, width):
        return jnp.dot(hb, w_ref[:, lo:lo + width], preferred_element_type=F32)

    qd[...] = _head_rms(seg(0, W_DIL), gqd[...])
    kd[...] = _head_rms(seg(W_DIL, W_DIL), gkd[...])
    vd[...] = seg(2 * W_DIL, W_DIL)
    base = 3 * W_DIL
    qb[...] = _head_rms(seg(base, W_MOBA), gqb[...])
    kb[...] = _head_rms(seg(base + W_MOBA, W_MOBA), gkb[...])
    vb[...] = seg(base + 2 * W_MOBA, W_MOBA)
    qm[...] = _head_rms(seg(base + 3 * W_MOBA, W_MEM), gqm[...])


def _inproj(x2, g_mix, w_in, gqd, gkd, gqb, gkb, gqm):
    t = x2.shape[0]
    tm = INPROJ_TM
    row = lambda w: pl.BlockSpec((1, w), lambda i: (0, 0))
    tile = lambda w: pl.BlockSpec((tm, w), lambda i: (i, 0))
    widths = (W_DIL, W_DIL, W_DIL, W_MOBA, W_MOBA, W_MOBA, W_MEM)
    return pl.pallas_call(
        _inproj_kernel,
        grid=(t // tm,),
        in_specs=[tile(D_MODEL), row(D_MODEL), pl.BlockSpec((D_MODEL, IN_WIDTH), lambda i: (0, 0)),
                  row(W_DIL), row(W_DIL), row(W_MOBA), row(W_MOBA), row(W_MEM)],
        out_specs=[tile(w) for w in widths],
        out_shape=[jax.ShapeDtypeStruct((t, w), F32) for w in widths],
        compiler_params=_cparams(("parallel",)),
        name="inproj",
    )(x2, g_mix, w_in, gqd, gkd, gqb, gkb, gqm)


DIL_PAD = BAND_BLOCK * max(d for _, d in DIL_CONFIGS)
DIL_MIX_ROWS = 256
DIL_UNROLL = 4


def _dil_kernel(q_ref, k_ref, v_ref, ns_ref, o_ref, kpad, vpad, ob0, ob1, ob2, lb0, lb1, lb2):
    s_len = q_ref.shape[0]
    zeros = jnp.zeros((DIL_PAD, LANES), F32)
    kpad[0:DIL_PAD, :] = zeros
    vpad[0:DIL_PAD, :] = zeros
    kpad[DIL_PAD:, :] = k_ref[...]
    vpad[DIL_PAD:, :] = v_ref[...]

    blk = BAND_BLOCK
    head0 = lax.broadcasted_iota(I32, (blk, LANES), 1) < HEAD_DIM
    ql = lax.broadcasted_iota(I32, (blk, 2 * blk), 0)
    kl = lax.broadcasted_iota(I32, (blk, 2 * blk), 1)
    delta = blk + ql - kl
    obs = (ob0, ob1, ob2)
    lbs = (lb0, lb1, lb2)

    for c, (window, dil) in enumerate(DIL_CONFIGS):
        reach = window // dil
        n_blk = s_len // dil // blk
        in_band = (delta >= 0) & (delta <= reach)
        dist = (delta * dil).astype(F32)
        ob, lb = obs[c], lbs[c]

        def rows(start, size, dil=dil):
            return pl.ds(start, size) if dil == 1 else pl.ds(start, size, stride=dil)

        def block_body(n, r, dil=dil, in_band=in_band, dist=dist, ob=ob, lb=lb, rows=rows):
            q_start = r + dil * blk * n
            k_start = DIL_PAD + r + dil * blk * (n - 1)
            qv = q_ref[rows(q_start, blk), :]
            kv = kpad[rows(k_start, 2 * blk), :].astype(BF16)
            vv = vpad[rows(k_start, 2 * blk), :].astype(BF16)
            valid = in_band & (kl >= jnp.where(n == 0, blk, 0))
            outs, lses = [], []
            for h in range(2):
                qh = jnp.where(head0 if h == 0 else ~head0, qv, 0.0).astype(BF16)
                s = lax.dot_general(qh, kv, _NT, preferred_element_type=F32) * SCALE
                s = jnp.where(valid, s + ns_ref[h] * dist, NEG_INF)
                m = jnp.max(s, axis=-1, keepdims=True)
                p = jnp.exp(s - m)
                den = jnp.sum(p, axis=-1, keepdims=True)
                lses.append(jnp.broadcast_to(m + jnp.log(den), (blk, LANES)))
                outs.append(jnp.dot(p.astype(BF16), vv, preferred_element_type=F32) / den)
            ob[rows(q_start, blk), :] = jnp.where(head0, outs[0], outs[1])
            lb[rows(q_start, blk), :] = jnp.where(head0, lses[0], lses[1])

        def flat_body(idx, carry, n_blk=n_blk, block_body=block_body):
            block_body(idx & (n_blk - 1), _shr_scalar(idx, n_blk.bit_length() - 1))
            return carry

        lax.fori_loop(0, dil * n_blk, flat_body, 0, unroll=DIL_UNROLL)

    def mix(i, carry):
        sl = pl.ds(pl.multiple_of(i * DIL_MIX_ROWS, DIL_MIX_ROWS), DIL_MIX_ROWS)
        l0, l1, l2 = lb0[sl, :], lb1[sl, :], lb2[sl, :]
        mx = jnp.maximum(jnp.maximum(l0, l1), l2)
        e0, e1, e2 = jnp.exp(l0 - mx), jnp.exp(l1 - mx), jnp.exp(l2 - mx)
        tot = e0 + e1 + e2
        o_ref[sl, :] = (e0 / tot) * ob0[sl, :] + (e1 / tot) * ob1[sl, :] + (e2 / tot) * ob2[sl, :]
        return carry

    lax.fori_loop(0, s_len // DIL_MIX_ROWS, mix, 0)


def _dilated(q, k, v, neg_slopes):
    b, s_len, w = q.shape
    n_pair = w // LANES
    slab = pl.BlockSpec((None, s_len, LANES), lambda i, p: (i, 0, p))
    buf = lambda rows: pltpu.VMEM((rows, LANES), F32)
    return pl.pallas_call(
        _dil_kernel,
        grid=(b, n_pair),
        in_specs=[slab, slab, slab,
                  pl.BlockSpec((None, 2, 1, 2 * BAND_BLOCK), lambda i, p: (p, 0, 0, 0))],
        out_specs=slab,
        out_shape=jax.ShapeDtypeStruct((b, s_len, w), F32),
        scratch_shapes=[buf(DIL_PAD + s_len), buf(DIL_PAD + s_len)] + [buf(s_len)] * 6,
        compiler_params=_cparams(("parallel", "parallel")),
        name="dilated",
    )(q, k, v, neg_slopes)


def _split_bf16(a):
    hi = a.astype(BF16)
    lo = (a - hi.astype(F32)).astype(BF16)
    return hi, lo


def _split3_bf16(a):
    t0 = a.astype(BF16).astype(F32)
    t1 = (a - t0).astype(BF16).astype(F32)
    t2 = (a - t0 - t1).astype(BF16).astype(F32)
    return t0, t1, t2


MOBA_QB = 16
MOBA_KB = 19


def _moba_kernel(q_ref, k_ref, v_ref, ns_ref, o_ref, qa_ref, kb_ref, kx_ref, vt_ref, acc_ref):
    s_len = q_ref.shape[0]
    blk = MOBA_BLOCK
    n_blk = s_len // blk
    lane = lax.broadcasted_iota(I32, (blk, LANES), 1)
    head0 = lane < HEAD_DIM
    pos0 = lax.broadcasted_iota(I32, (blk, LANES), 0).astype(F32)

    kb_ref[...] = k_ref[...].astype(BF16)

    k_mean = jnp.concatenate(
        [jnp.sum(k_ref[j * blk:(j + 1) * blk, :], axis=0, keepdims=True) for j in range(n_blk)],
        axis=0) * (1.0 / blk)
    q_hi, q_lo = _split_bf16(q_ref[...])

    blk_id = lax.broadcasted_iota(I32, (n_blk, s_len), 0)
    n_past = _shr(lax.broadcasted_iota(I32, (n_blk, s_len), 1), blk.bit_length() - 1)
    past = blk_id < n_past
    blk_f = blk_id.astype(F32)
    pad_rows = LANES - n_blk

    for h in range(2):
        hm = (lax.broadcasted_iota(I32, (n_blk, LANES), 1) < HEAD_DIM) == (h == 0)
        km_hi, km_lo = _split_bf16(jnp.where(hm, k_mean, 0.0))
        gate = (lax.dot_general(km_hi, q_hi, _NT, preferred_element_type=F32)
                + lax.dot_general(km_hi, q_lo, _NT, preferred_element_type=F32)
                + lax.dot_general(km_lo, q_hi, _NT, preferred_element_type=F32))
        g = jnp.where(past, gate, NEG_INF)
        sel = jnp.zeros((n_blk, s_len), F32)
        for _ in range(MOBA_TOPK):
            m = jnp.max(g, axis=0, keepdims=True)
            first = jnp.min(jnp.where(g == m, blk_f, float(n_blk)), axis=0, keepdims=True)
            pick = blk_f == first
            sel = jnp.where(pick, 1.0, sel)
            g = jnp.where(pick, -jnp.inf, g)
        not_sel = jnp.where(past, 1.0 - sel, jnp.where(blk_id == n_past, 0.0, 1.0))
        not_sel = jnp.concatenate([not_sel, jnp.zeros((pad_rows, s_len), F32)], axis=0)
        ns = ns_ref[h][:, 0:LANES]
        for i in range(n_blk):
            rows = slice(i * blk, (i + 1) * blk)
            qh = jnp.where(head0 if h == 0 else ~head0, q_ref[rows, :], 0.0) * SCALE
            qa_ref[h, rows, 0:LANES] = qh.astype(BF16)
            t_pos = pos0 + float(i * blk)
            q0, q1, q2 = _split3_bf16(ns * t_pos)
            k0, k1, k2 = _split3_bf16(-ns * t_pos)
            qx = not_sel[:, rows].T
            kx = jnp.where(lane == i, NEG_INF, 0.0)
            for d, (qt, kt) in enumerate(((q0, k0), (q1, k1), (q2, k2))):
                qx = jnp.where(lane == MOBA_QB + d, qt, jnp.where(lane == MOBA_KB + d, 1.0, qx))
                kx = jnp.where(lane == MOBA_QB + d, 1.0, jnp.where(lane == MOBA_KB + d, kt, kx))
            qa_ref[h, rows, LANES:] = qx.astype(BF16)
            kx_ref[h, rows, :] = kx.astype(BF16)

    top_rows = lax.broadcasted_iota(I32, (LANES, blk), 0) < HEAD_DIM
    for j in range(n_blk):
        vt = v_ref[j * blk:(j + 1) * blk, :].T
        vt_ref[0, :, j * blk:(j + 1) * blk] = jnp.where(top_rows, vt, 1.0).astype(BF16)
        vt_ref[1, :, j * blk:(j + 1) * blk] = jnp.where(top_rows, 1.0, vt).astype(BF16)

    key_i = lax.broadcasted_iota(I32, (blk, blk), 0)
    qry_i = lax.broadcasted_iota(I32, (blk, blk), 1)
    causal = key_i <= qry_i

    def q_tile(i, carry):
        r0 = pl.multiple_of(i * blk, blk)
        qas = [jnp.concatenate([qa_ref[h, pl.ds(r0, blk), 0:LANES], qa_ref[h, pl.ds(r0, blk), LANES:]],
                               axis=1) for h in range(2)]

        def scores(c0, h):
            k_aug = jnp.concatenate([kb_ref[pl.ds(c0, blk), :], kx_ref[h, pl.ds(c0, blk), :]], axis=1)
            return lax.dot_general(k_aug, qas[h], _NT, preferred_element_type=F32)

        ms = []
        for h in range(2):
            s = jnp.where(causal, scores(r0, h), NEG_INF)
            m0 = jnp.max(s, axis=0, keepdims=True)
            p = jnp.exp(s - m0)
            acc_ref[h] = jnp.dot(vt_ref[h, :, pl.ds(r0, blk)], p.astype(BF16),
                                 preferred_element_type=F32)
            ms.append(m0)

        def past_block(j, ms):
            c0 = pl.multiple_of(j * blk, blk)
            new_ms = []
            for h in range(2):
                sj = scores(c0, h)
                m_new = jnp.maximum(ms[h], jnp.max(sj, axis=0, keepdims=True))
                alpha = jnp.exp(ms[h] - m_new)
                pj = jnp.exp(sj - m_new)
                acc_ref[h] = alpha * acc_ref[h] + jnp.dot(vt_ref[h, :, pl.ds(c0, blk)], pj.astype(BF16),
                                                          preferred_element_type=F32)
                new_ms.append(m_new)
            return tuple(new_ms)

        def four_past_blocks(jj, ms):
            for d in range(4):
                ms = past_block(4 * jj + d, ms)
            return ms

        def two_past_blocks(jj, ms):
            return past_block(2 * jj + 1, past_block(2 * jj, ms))

        n_four = lax.shift_right_logical(i, 2)
        n_two = lax.shift_right_logical(i, 1)
        ms = lax.fori_loop(0, n_four, four_past_blocks, tuple(ms))
        ms = lax.fori_loop(2 * n_four, n_two, two_past_blocks, ms)
        lax.fori_loop(2 * n_two, i, past_block, ms)
        a0, a1 = acc_ref[0], acc_ref[1]
        out_t = jnp.where(top_rows, a0 / a0[HEAD_DIM:HEAD_DIM + 1, :], a1 / a1[0:1, :])
        o_ref[pl.ds(r0, blk), :] = out_t.T
        return carry

    lax.fori_loop(0, n_blk, q_tile, 0)


def _moba(q, k, v, neg_slopes):
    b, s_len, w = q.shape
    n_pair = w // LANES
    slab = pl.BlockSpec((None, s_len, LANES), lambda i, p: (i, 0, p))
    return pl.pallas_call(
        _moba_kernel,
        grid=(b, n_pair),
        in_specs=[slab, slab, slab,
                  pl.BlockSpec((None, 2, 1, MOBA_BLOCK), lambda i, p: (p, 0, 0, 0))],
        out_specs=slab,
        out_shape=jax.ShapeDtypeStruct((b, s_len, w), F32),
        scratch_shapes=[pltpu.VMEM((2, s_len, 2 * LANES), BF16),
                        pltpu.VMEM((s_len, LANES), BF16),
                        pltpu.VMEM((2, s_len, LANES), BF16),
                        pltpu.VMEM((2, LANES, s_len), BF16),
                        pltpu.VMEM((2, LANES, MOBA_BLOCK), F32)],
        compiler_params=_cparams(("parallel", "parallel")),
        name="moba",
    )(q, k, v, neg_slopes)


MEMATTN_TQ = 512


def _memattn_kernel(q_ref, k_ref, v_ref, o_ref):
    q = q_ref[...]
    kb = k_ref[...].astype(BF16)
    vb = v_ref[...].astype(BF16)
    head = _shr(lax.broadcasted_iota(I32, q.shape, 1), HEAD_DIM.bit_length() - 1)
    out = jnp.zeros(q.shape, F32)
    for h in range(N_HEADS_MEM):
        qh = jnp.where(head == h, q, 0.0).astype(BF16)
        s = lax.dot_general(qh, kb, _NT, preferred_element_type=F32) * SCALE
        m = jnp.max(s, axis=-1, keepdims=True)
        p = jnp.exp(s - m)
        den = jnp.sum(p, axis=-1, keepdims=True)
        oh = jnp.dot(p.astype(BF16), vb, preferred_element_type=F32) / den
        out = jnp.where(head == h, oh, out)
    o_ref[...] = out


def _memattn(q, k, v):
    b, s_len, w = q.shape
    tq = MEMATTN_TQ
    return pl.pallas_call(
        _memattn_kernel,
        grid=(b, s_len // tq),
        in_specs=[pl.BlockSpec((None, tq, w), lambda i, j: (i, j, 0)),
                  pl.BlockSpec((None, MEM_LEN, w), lambda i, j: (i, 0, 0)),
                  pl.BlockSpec((None, MEM_LEN, w), lambda i, j: (i, 0, 0))],
        out_specs=pl.BlockSpec((None, tq, w), lambda i, j: (i, j, 0)),
        out_shape=jax.ShapeDtypeStruct((b, s_len, w), F32),
        compiler_params=_cparams(("parallel", "parallel")),
        name="memattn",
    )(q, k, v)


OUTPROJ_TM = 512


def _outproj_kernel(od_ref, ob_ref, om_ref, x_ref, gd, gb, gm, wo_ref, gf, wq_ref, k1_ref, k2_ref,
                    x1_ref, hn_ref, s1_ref, s2_ref):
    y = jnp.dot(_row_rms(od_ref[...], gd[...]).astype(BF16), wo_ref[0:W_DIL, :],
                preferred_element_type=F32)
    y += jnp.dot(_row_rms(ob_ref[...], gb[...]).astype(BF16), wo_ref[W_DIL:W_DIL + W_MOBA, :],
                 preferred_element_type=F32)
    y += jnp.dot(_row_rms(om_ref[...], gm[...]).astype(BF16), wo_ref[W_DIL + W_MOBA:, :],
                 preferred_element_type=F32)
    x1 = x_ref[...] + y
    x1_ref[...] = x1
    hb = _row_rms(x1, gf[...]).astype(BF16)
    hn_ref[...] = hb
    half = PEER_DKEY // 2
    for h in range(PEER_HEADS):
        qh = jnp.dot(hb, wq_ref[:, h * PEER_DKEY:(h + 1) * PEER_DKEY],
                     preferred_element_type=F32).astype(BF16)
        s1_ref[h] = lax.dot_general(k1_ref[h], qh[:, :half], _NT, preferred_element_type=F32)
        s2_ref[h] = lax.dot_general(k2_ref[h], qh[:, half:], _NT, preferred_element_type=F32)


def _outproj(o_dil, o_moba, o_mem, x2, og_dil, og_moba, og_mem, w_out, g_ffn, w_q, sub1, sub2):
    t = x2.shape[0]
    tm = OUTPROJ_TM
    row = lambda w: pl.BlockSpec((1, w), lambda i: (0, 0))
    tile = lambda w: pl.BlockSpec((tm, w), lambda i: (i, 0))
    full = lambda shape: pl.BlockSpec(shape, lambda i: (0,) * len(shape))
    score = pl.BlockSpec((PEER_HEADS, PEER_NKEYS, tm), lambda i: (0, 0, i))
    half = PEER_DKEY // 2
    return pl.pallas_call(
        _outproj_kernel,
        grid=(t // tm,),
        in_specs=[tile(W_DIL), tile(W_MOBA), tile(W_MEM), tile(D_MODEL),
                  row(W_DIL), row(W_MOBA), row(W_MEM), full((D_MODEL, D_MODEL)), row(D_MODEL),
                  full((D_MODEL, PEER_HEADS * PEER_DKEY)),
                  full((PEER_HEADS, PEER_NKEYS, half)), full((PEER_HEADS, PEER_NKEYS, half))],
        out_specs=[tile(D_MODEL), tile(D_MODEL), score, score],
        out_shape=[jax.ShapeDtypeStruct((t, D_MODEL), F32),
                   jax.ShapeDtypeStruct((t, D_MODEL), BF16),
                   jax.ShapeDtypeStruct((PEER_HEADS, PEER_NKEYS, t), F32),
                   jax.ShapeDtypeStruct((PEER_HEADS, PEER_NKEYS, t), F32)],
        compiler_params=_cparams(("parallel",)),
        name="outproj",
    )(o_dil, o_moba, o_mem, x2, og_dil, og_moba, og_mem, w_out, g_ffn, w_q, sub1, sub2)


TOPK_TM = 512


def _top_rows(s, k):
    n_rows = s.shape[0]
    rid = lax.broadcasted_iota(I32, s.shape, 0).astype(F32)
    vals, idxs = [], []
    for _ in range(k):
        m = jnp.max(s, axis=0, keepdims=True)
        first = jnp.min(jnp.where(s == m, rid, float(n_rows)), axis=0, keepdims=True)
        vals.append(m)
        idxs.append(first)
        s = jnp.where(rid == first, -jnp.inf, s)
    return jnp.concatenate(vals, axis=0), jnp.concatenate(idxs, axis=0)


def _pick_rows(table, sel):
    out = jnp.zeros(sel.shape, table.dtype)
    for j in range(table.shape[0]):
        out = jnp.where(sel == j, table[j:j + 1, :], out)
    return out.astype(I32)


def _candidates(v1, v2):
    sub = lax.broadcasted_iota(I32, (8, v1.shape[1]), 0)
    pieces = [v1[0:1, :] + v2]
    for j1 in range(1, 8):
        pieces.append(jnp.where(sub < PEER_TOPK // (j1 + 1), v1[j1:j1 + 1, :] + v2[0:8, :], -jnp.inf))
    pieces.append(v1[8:16, :] + v2[0:1, :])
    return jnp.concatenate(pieces, axis=0)


def _candidate_coords(row):
    low = row & 7
    j1 = jnp.where(row < 16, 0, jnp.where(row >= 72, 8 + low, _shr(row, 3) - 1))
    j2 = jnp.where(row < 16, row, jnp.where(row >= 72, 0, low))
    return j1, j2


def _peertopk_kernel(s1_ref, s2_ref, i1_ref, i2_ref, g_ref):
    n_col = s1_ref.shape[2] // LANES

    def head_body(h, carry):
        for c in range(n_col):
            cols = slice(c * LANES, (c + 1) * LANES)
            v1, i1 = _top_rows(s1_ref[h, :, cols], PEER_TOPK)
            v2, i2 = _top_rows(s2_ref[h, :, cols], PEER_TOPK)
            top_s, pos = _top_rows(_candidates(v1, v2), PEER_TOPK)
            e = jnp.exp(top_s - top_s[0:1, :])
            g_ref[h, :, cols] = e / jnp.sum(e, axis=0, keepdims=True)
            j1, j2 = _candidate_coords(pos.astype(I32))
            i1_ref[h, :, cols] = _pick_rows(i1, j1)
            i2_ref[h, :, cols] = _pick_rows(i2, j2)
        return carry

    lax.fori_loop(0, PEER_HEADS, head_body, 0)


def _peertopk(s1, s2):
    t = s1.shape[2]
    tm = TOPK_TM
    score = pl.BlockSpec((PEER_HEADS, PEER_NKEYS, tm), lambda i: (0, 0, i))
    slot = pl.BlockSpec((PEER_HEADS, PEER_TOPK, tm), lambda i: (0, 0, i))
    return pl.pallas_call(
        _peertopk_kernel,
        grid=(t // tm,),
        in_specs=[score, score],
        out_specs=[slot, slot, slot],
        out_shape=[jax.ShapeDtypeStruct((PEER_HEADS, PEER_TOPK, t), I32),
                   jax.ShapeDtypeStruct((PEER_HEADS, PEER_TOPK, t), I32),
                   jax.ShapeDtypeStruct((PEER_HEADS, PEER_TOPK, t), F32)],
        compiler_params=_cparams(("parallel",)),
        name="peertopk",
    )(s1, s2)


PEER_TM = 512
PEER_TE = 2048
PEER_HALF = PEER_TM // 2
PEER_PITCH = PEER_HALF + 8
PEER_BUILD_UNROLL = 16


def _bf16_bits(a):
    return lax.bitcast_convert_type(a.astype(BF16).astype(F32), U32)


def _peerffn_kernel(h_ref, x1_ref, i1_ref, i2_ref, g_ref, u_ref, v_ref, o_ref, gate_ref):
    c = pl.program_id(1)
    slabs = PEER_TE // PEER_NKEYS

    @pl.when(c == 0)
    def _build_gate_matrix():
        key_id = lax.broadcasted_iota(I32, (PEER_NKEYS, PEER_SLOTS), 0)

        def gate_matrix(t):
            i1 = jnp.broadcast_to(i1_ref[pl.ds(t, 1), :], key_id.shape)
            i2 = jnp.broadcast_to(i2_ref[pl.ds(t, 1), :], key_id.shape)
            g = jnp.broadcast_to(g_ref[pl.ds(t, 1), :], key_id.shape)
            lhs = jnp.where(key_id == i1, g, 0.0).astype(BF16)
            rhs = jnp.where(key_id == i2, 1.0, 0.0).astype(BF16)
            return lax.dot_general(lhs, rhs, _NT, preferred_element_type=F32)

        def token_pair(t, carry):
            lo = lax.shift_right_logical(_bf16_bits(gate_matrix(t)), jnp.uint32(16))
            hi = _bf16_bits(gate_matrix(t + PEER_HALF))
            gate_ref[pl.ds(t, PEER_NKEYS, stride=PEER_PITCH), :] = hi | lo
            return carry

        lax.fori_loop(0, PEER_HALF, token_pair, 0, unroll=PEER_BUILD_UNROLL)

    a = lax.dot_general(h_ref[...], u_ref[...], _NT, preferred_element_type=F32)
    ws = []
    for j in range(slabs):
        start = pl.multiple_of((c * slabs + j) * PEER_PITCH, 8)
        word = gate_ref[pl.ds(start, PEER_HALF), :]
        g_lo = lax.bitcast_convert_type(lax.shift_left(word, jnp.uint32(16)), F32)
        g_hi = lax.bitcast_convert_type(word & jnp.uint32(0xFFFF0000), F32)
        gj = jnp.concatenate([g_lo, g_hi], axis=0)
        aj = a[:, j * PEER_NKEYS:(j + 1) * PEER_NKEYS]
        ws.append((0.5 * aj * (1.0 + lax.erf(aj * SQRT_HALF)) * gj).astype(BF16))
    y = jnp.dot(jnp.concatenate(ws, axis=1), v_ref[...], preferred_element_type=F32)

    @pl.when(c == 0)
    def _first():
        o_ref[...] = x1_ref[...] + y

    @pl.when(c > 0)
    def _rest():
        o_ref[...] += y


def _peerffn(hn, x1, i1, i2, gate, u, v):
    t = hn.shape[0]
    tm, te = PEER_TM, PEER_TE
    tile = lambda w: pl.BlockSpec((tm, w), lambda i, c: (i, 0))
    chunk = pl.BlockSpec((te, D_MODEL), lambda i, c: (c, 0))
    return pl.pallas_call(
        _peerffn_kernel,
        grid=(t // tm, PEER_EXPERTS // te),
        in_specs=[tile(D_MODEL), tile(D_MODEL), tile(PEER_SLOTS), tile(PEER_SLOTS), tile(PEER_SLOTS),
                  chunk, chunk],
        out_specs=tile(D_MODEL),
        out_shape=jax.ShapeDtypeStruct((t, D_MODEL), F32),
        scratch_shapes=[pltpu.VMEM((PEER_NKEYS * PEER_PITCH, LANES), U32)],
        compiler_params=_cparams(("parallel", "arbitrary")),
        name="peerffn",
    )(hn, x1, i1, i2, gate, u, v)


def _neg_slope_rows(slopes, width):
    n = slopes.shape[0]
    return jnp.broadcast_to((-slopes).reshape(n // 2, 2, 1, 1), (n // 2, 2, 1, width))


def _tiled_gain(g, reps):
    return jnp.tile(g, reps)[None, :]


def _layer(x, mem, g_mix, w_in, qg_dil, kg_dil, qg_moba, kg_moba, qg_mem, kg_mem, g_memtok, w_mem_kv,
           og_dil, og_moba, og_mem, w_out, g_ffn, w_peer_q, sub1, sub2, peer_u, peer_v):
    b, s_len, d = x.shape
    t = b * s_len
    n_mix = N_HEADS_DIL + N_HEADS_MOBA
    slopes = jnp.exp2(-8.0 * jnp.arange(1, n_mix + 1, dtype=F32) / n_mix)
    ns_dil = _neg_slope_rows(slopes[0::2], 2 * BAND_BLOCK)
    ns_moba = _neg_slope_rows(slopes[1::2], MOBA_BLOCK)

    x2 = x.reshape(t, d)
    k_m, v_m = _memkv(mem, g_memtok[None, :], w_mem_kv.astype(BF16), _tiled_gain(kg_mem, N_HEADS_MEM))
    q_d, k_d, v_d, q_b, k_b, v_b, q_m = _inproj(
        x2, g_mix[None, :], w_in.astype(BF16),
        _tiled_gain(qg_dil, N_HEADS_DIL), _tiled_gain(kg_dil, N_HEADS_DIL),
        _tiled_gain(qg_moba, N_HEADS_MOBA), _tiled_gain(kg_moba, N_HEADS_MOBA),
        _tiled_gain(qg_mem, N_HEADS_MEM))
    seq = lambda a: a.reshape(b, s_len, a.shape[-1])
    o_dil = _dilated(seq(q_d), seq(k_d), seq(v_d), ns_dil)
    o_moba = _moba(seq(q_b), seq(k_b), seq(v_b), ns_moba)
    o_mem = _memattn(seq(q_m), k_m, v_m)
    x1, hn, s1, s2 = _outproj(
        o_dil.reshape(t, W_DIL), o_moba.reshape(t, W_MOBA), o_mem.reshape(t, W_MEM), x2,
        og_dil[None, :], og_moba[None, :], og_mem[None, :], w_out.astype(BF16), g_ffn[None, :],
        w_peer_q.astype(BF16), sub1.astype(BF16), sub2.astype(BF16))
    i1, i2, gate = _peertopk(s1, s2)
    slots = lambda a: a.reshape(PEER_SLOTS, t).T
    out = _peerffn(hn, x1, slots(i1), slots(i2), slots(gate), peer_u.astype(BF16), peer_v.astype(BF16))
    return out.reshape(b, s_len, d)


def kernel(x, mem, g_mix, w_in, qg_dil, kg_dil, qg_moba, kg_moba, qg_mem, kg_mem, g_memtok, w_mem_kv,
           og_dil, og_moba, og_mem, w_out, g_ffn, w_peer_q, peer_subkeys_1, peer_subkeys_2, peer_u,
           peer_v):
    h = x
    for layer in range(g_mix.shape[0]):
        h = _layer(h, mem, g_mix[layer], w_in[layer], qg_dil[layer], kg_dil[layer], qg_moba[layer],
                   kg_moba[layer], qg_mem[layer], kg_mem[layer], g_memtok[layer], w_mem_kv[layer],
                   og_dil[layer], og_moba[layer], og_mem[layer], w_out[layer], g_ffn[layer],
                   w_peer_q[layer], peer_subkeys_1[layer], peer_subkeys_2[layer], peer_u[layer],
                   peer_v[layer])
    return h
```

```python
import functools
import math

import jax
import jax.numpy as jnp
from jax import lax
from jax.experimental import pallas as pl
from jax.experimental.pallas import tpu as pltpu

F32 = jnp.float32
BF16 = jnp.bfloat16
I32 = jnp.int32
U32 = jnp.uint32

LANES = 128
D_MODEL = 1024
N_HEADS_DIL = 6
N_HEADS_MOBA = 6
N_HEADS_MEM = 4
HEAD_DIM = 64
W_DIL = N_HEADS_DIL * HEAD_DIM
W_MOBA = N_HEADS_MOBA * HEAD_DIM
W_MEM = N_HEADS_MEM * HEAD_DIM
IN_WIDTH = 3 * W_DIL + 3 * W_MOBA + W_MEM
DIL_CONFIGS = ((128, 1), (512, 4), (2048, 16))
BAND_BLOCK = 128
MOBA_BLOCK = 256
MOBA_TOPK = 3
MEM_LEN = 256
PEER_HEADS = 8
PEER_NKEYS = 128
PEER_EXPERTS = PEER_NKEYS * PEER_NKEYS
PEER_TOPK = 16
PEER_DKEY = 256
PEER_SLOTS = PEER_HEADS * PEER_TOPK
RMS_EPS = 1e-6
NEG_INF = -1e30
SCALE = 1.0 / math.sqrt(HEAD_DIM)
SQRT_HALF = math.sqrt(0.5)

VMEM_LIMIT = 56 * 1024 * 1024

_NT = (((1,), (1,)), ((), ()))


def _shr(a, bits):
    return lax.shift_right_logical(a, jnp.full(a.shape, bits, a.dtype))


def _shr_scalar(a, bits):
    return lax.shift_right_logical(a, jnp.int32(bits))


def _cparams(sem):
    return pltpu.CompilerParams(dimension_semantics=sem, vmem_limit_bytes=VMEM_LIMIT)


def _row_rms(a, gain):
    return a * lax.rsqrt(jnp.mean(a * a, axis=-1, keepdims=True) + RMS_EPS) * gain


def _group_mean_sq(p, group):
    w = p.shape[-1]
    bits = group.bit_length() - 1
    gi = _shr(lax.broadcasted_iota(I32, (w, w), 0), bits)
    gj = _shr(lax.broadcasted_iota(I32, (w, w), 1), bits)
    ones_bd = jnp.where(gi == gj, 1.0, 0.0).astype(BF16)
    p2 = p * p
    hi = p2.astype(BF16)
    lo = (p2 - hi.astype(F32)).astype(BF16)
    ss = (jnp.dot(hi, ones_bd, preferred_element_type=F32)
          + jnp.dot(lo, ones_bd, preferred_element_type=F32))
    return ss * (1.0 / group)


def _head_rms(p, gain):
    return p * lax.rsqrt(_group_mean_sq(p, HEAD_DIM) + RMS_EPS) * gain


def _memkv_kernel(mem_ref, g_ref, w_ref, kg_ref, k_ref, v_ref):
    hn = _row_rms(mem_ref[...], g_ref[...]).astype(BF16)
    kv = jnp.dot(hn, w_ref[...], preferred_element_type=F32)
    k_ref[...] = _head_rms(kv[:, :W_MEM], kg_ref[...])
    v_ref[...] = kv[:, W_MEM:]


def _memkv(mem, g_memtok, w_kv, kg_mem):
    b = mem.shape[0]
    return pl.pallas_call(
        _memkv_kernel,
        grid=(b,),
        in_specs=[
            pl.BlockSpec((None, MEM_LEN, D_MODEL), lambda i: (i, 0, 0)),
            pl.BlockSpec((1, D_MODEL), lambda i: (0, 0)),
            pl.BlockSpec((D_MODEL, 2 * W_MEM), lambda i: (0, 0)),
            pl.BlockSpec((1, W_MEM), lambda i: (0, 0)),
        ],
        out_specs=[pl.BlockSpec((None, MEM_LEN, W_MEM), lambda i: (i, 0, 0))] * 2,
        out_shape=[jax.ShapeDtypeStruct((b, MEM_LEN, W_MEM), F32)] * 2,
        compiler_params=_cparams(("parallel",)),
        name="memkv",
    )(mem, g_memtok, w_kv, kg_mem)


INPROJ_TM = 512


def _inproj_kernel(x_ref, g_ref, w_ref, gqd, gkd, gqb, gkb, gqm, qd, kd, vd, qb, kb, vb, qm):
    hb = _row_rms(x_ref[...], g_ref[...]).astype(BF16)

    def seg(lo, width):
        return jnp.dot(hb, w_ref[:, lo:lo + width], preferred_element_type=F32)

    qd[...] = _head_rms(seg(0, W_DIL), gqd[...])
    kd[...] = _head_rms(seg(W_DIL, W_DIL), gkd[...])
    vd[...] = seg(2 * W_DIL, W_DIL)
    base = 3 * W_DIL
    qb[...] = _head_rms(seg(base, W_MOBA), gqb[...])
    kb[...] = _head_rms(seg(base + W_MOBA, W_MOBA), gkb[...])
    vb[...] = seg(base + 2 * W_MOBA, W_MOBA)
    qm[...] = _head_rms(seg(base + 3 * W_MOBA, W_MEM), gqm[...])


def _inproj(x2, g_mix, w_in, gqd, gkd, gqb, gkb, gqm):
    t = x2.shape[0]
    tm = INPROJ_TM
    row = lambda w: pl.BlockSpec((1, w), lambda i: (0, 0))
    tile = lambda w: pl.BlockSpec((tm, w), lambda i: (i, 0))
    widths = (W_DIL, W_DIL, W_DIL, W_MOBA, W_MOBA, W_MOBA, W_MEM)
    return pl.pallas_call(
        _inproj_kernel,
        grid=(t // tm,),
        in_specs=[tile(D_MODEL), row(D_MODEL), pl.BlockSpec((D_MODEL, IN_WIDTH), lambda i: (0, 0)),
                  row(W_DIL), row(W_DIL), row(W_MOBA), row(W_MOBA), row(W_MEM)],
        out_specs=[tile(w) for w in widths],
        out_shape=[jax.ShapeDtypeStruct((t, w), F32) for w in widths],
        compiler_params=_cparams(("parallel",)),
        name="inproj",
    )(x2, g_mix, w_in, gqd, gkd, gqb, gkb, gqm)


DIL_PAD = BAND_BLOCK * max(d for _, d in DIL_CONFIGS)
DIL_MIX_ROWS = 256
DIL_UNROLL = 4


def _dil_kernel(q_ref, k_ref, v_ref, ns_ref, o_ref, kpad, vpad, ob0, ob1, ob2, lb0, lb1, lb2):
    s_len = q_ref.shape[0]
    zeros = jnp.zeros((DIL_PAD, LANES), F32)
    kpad[0:DIL_PAD, :] = zeros
    vpad[0:DIL_PAD, :] = zeros
    kpad[DIL_PAD:, :] = k_ref[...]
    vpad[DIL_PAD:, :] = v_ref[...]

    blk = BAND_BLOCK
    head0 = lax.broadcasted_iota(I32, (blk, LANES), 1) < HEAD_DIM
    ql = lax.broadcasted_iota(I32, (blk, 2 * blk), 0)
    kl = lax.broadcasted_iota(I32, (blk, 2 * blk), 1)
    delta = blk + ql - kl
    obs = (ob0, ob1, ob2)
    lbs = (lb0, lb1, lb2)

    for c, (window, dil) in enumerate(DIL_CONFIGS):
        reach = window // dil
        n_blk = s_len // dil // blk
        in_band = (delta >= 0) & (delta <= reach)
        dist = (delta * dil).astype(F32)
        ob, lb = obs[c], lbs[c]

        def rows(start, size, dil=dil):
            return pl.ds(start, size) if dil == 1 else pl.ds(start, size, stride=dil)

        def block_body(n, r, dil=dil, in_band=in_band, dist=dist, ob=ob, lb=lb, rows=rows):
            q_start = r + dil * blk * n
            k_start = DIL_PAD + r + dil * blk * (n - 1)
            qv = q_ref[rows(q_start, blk), :]
            kv = kpad[rows(k_start, 2 * blk), :].astype(BF16)
            vv = vpad[rows(k_start, 2 * blk), :].astype(BF16)
            valid = in_band & (kl >= jnp.where(n == 0, blk, 0))
            outs, lses = [], []
            for h in range(2):
                qh = jnp.where(head0 if h == 0 else ~head0, qv, 0.0).astype(BF16)
                s = lax.dot_general(qh, kv, _NT, preferred_element_type=F32) * SCALE
                s = jnp.where(valid, s + ns_ref[h] * dist, NEG_INF)
                m = jnp.max(s, axis=-1, keepdims=True)
                p = jnp.exp(s - m)
                den = jnp.sum(p, axis=-1, keepdims=True)
                lses.append(jnp.broadcast_to(m + jnp.log(den), (blk, LANES)))
                outs.append(jnp.dot(p.astype(BF16), vv, preferred_element_type=F32) / den)
            ob[rows(q_start, blk), :] = jnp.where(head0, outs[0], outs[1])
            lb[rows(q_start, blk), :] = jnp.where(head0, lses[0], lses[1])

        def flat_body(idx, carry, n_blk=n_blk, block_body=block_body):
            block_body(idx & (n_blk - 1), _shr_scalar(idx, n_blk.bit_length() - 1))
            return carry

        lax.fori_loop(0, dil * n_blk, flat_body, 0, unroll=DIL_UNROLL)

    def mix(i, carry):
        sl = pl.ds(pl.multiple_of(i * DIL_MIX_ROWS, DIL_MIX_ROWS), DIL_MIX_ROWS)
        l0, l1, l2 = lb0[sl, :], lb1[sl, :], lb2[sl, :]
        mx = jnp.maximum(jnp.maximum(l0, l1), l2)
        e0, e1, e2 = jnp.exp(l0 - mx), jnp.exp(l1 - mx), jnp.exp(l2 - mx)
        tot = e0 + e1 + e2
        o_ref[sl, :] = (e0 / tot) * ob0[sl, :] + (e1 / tot) * ob1[sl, :] + (e2 / tot) * ob2[sl, :]
        return carry

    lax.fori_loop(0, s_len // DIL_MIX_ROWS, mix, 0)


def _dilated(q, k, v, neg_slopes):
    b, s_len, w = q.shape
    n_pair = w // LANES
    slab = pl.BlockSpec((None, s_len, LANES), lambda i, p: (i, 0, p))
    buf = lambda rows: pltpu.VMEM((rows, LANES), F32)
    return pl.pallas_call(
        _dil_kernel,
        grid=(b, n_pair),
        in_specs=[slab, slab, slab,
                  pl.BlockSpec((None, 2, 1, 2 * BAND_BLOCK), lambda i, p: (p, 0, 0, 0))],
        out_specs=slab,
        out_shape=jax.ShapeDtypeStruct((b, s_len, w), F32),
        scratch_shapes=[buf(DIL_PAD + s_len), buf(DIL_PAD + s_len)] + [buf(s_len)] * 6,
        compiler_params=_cparams(("parallel", "parallel")),
        name="dilated",
    )(q, k, v, neg_slopes)


def _split_bf16(a):
    hi = a.astype(BF16)
    lo = (a - hi.astype(F32)).astype(BF16)
    return hi, lo


def _split3_bf16(a):
    t0 = a.astype(BF16).astype(F32)
    t1 = (a - t0).astype(BF16).astype(F32)
    t2 = (a - t0 - t1).astype(BF16).astype(F32)
    return t0, t1, t2


MOBA_QB = 16
MOBA_KB = 19


def _moba_kernel(q_ref, k_ref, v_ref, ns_ref, o_ref, qa_ref, kb_ref, kx_ref, vt_ref, m_ref, acc_ref):
    s_len = q_ref.shape[0]
    blk = MOBA_BLOCK
    n_blk = s_len // blk
    lane = lax.broadcasted_iota(I32, (blk, LANES), 1)
    head0 = lane < HEAD_DIM
    pos0 = lax.broadcasted_iota(I32, (blk, LANES), 0).astype(F32)

    kb_ref[...] = k_ref[...].astype(BF16)

    k_mean = jnp.concatenate(
        [jnp.sum(k_ref[j * blk:(j + 1) * blk, :], axis=0, keepdims=True) for j in range(n_blk)],
        axis=0) * (1.0 / blk)
    q_hi, q_lo = _split_bf16(q_ref[...])

    blk_id = lax.broadcasted_iota(I32, (n_blk, s_len), 0)
    n_past = _shr(lax.broadcasted_iota(I32, (n_blk, s_len), 1), blk.bit_length() - 1)
    past = blk_id < n_past
    blk_f = blk_id.astype(F32)
    pad_rows = LANES - n_blk

    for h in range(2):
        hm = (lax.broadcasted_iota(I32, (n_blk, LANES), 1) < HEAD_DIM) == (h == 0)
        km_hi, km_lo = _split_bf16(jnp.where(hm, k_mean, 0.0))
        gate = (lax.dot_general(km_hi, q_hi, _NT, preferred_element_type=F32)
                + lax.dot_general(km_hi, q_lo, _NT, preferred_element_type=F32)
                + lax.dot_general(km_lo, q_hi, _NT, preferred_element_type=F32))
        g = jnp.where(past, gate, NEG_INF)
        sel = jnp.zeros((n_blk, s_len), F32)
        for _ in range(MOBA_TOPK):
            m = jnp.max(g, axis=0, keepdims=True)
            first = jnp.min(jnp.where(g == m, blk_f, float(n_blk)), axis=0, keepdims=True)
            pick = blk_f == first
            sel = jnp.where(pick, 1.0, sel)
            g = jnp.where(pick, -jnp.inf, g)
        not_sel = jnp.where(past, 1.0 - sel, jnp.where(blk_id == n_past, 0.0, 1.0))
        not_sel = jnp.concatenate([not_sel, jnp.zeros((pad_rows, s_len), F32)], axis=0)
        ns = ns_ref[h][:, 0:LANES]
        for i in range(n_blk):
            rows = slice(i * blk, (i + 1) * blk)
            qh = jnp.where(head0 if h == 0 else ~head0, q_ref[rows, :], 0.0) * SCALE
            qa_ref[h, rows, 0:LANES] = qh.astype(BF16)
            t_pos = pos0 + float(i * blk)
            q0, q1, q2 = _split3_bf16(ns * t_pos)
            k0, k1, k2 = _split3_bf16(-ns * t_pos)
            qx = not_sel[:, rows].T
            kx = jnp.where(lane == i, NEG_INF, 0.0)
            for d, (qt, kt) in enumerate(((q0, k0), (q1, k1), (q2, k2))):
                qx = jnp.where(lane == MOBA_QB + d, qt, jnp.where(lane == MOBA_KB + d, 1.0, qx))
                kx = jnp.where(lane == MOBA_QB + d, 1.0, jnp.where(lane == MOBA_KB + d, kt, kx))
            qa_ref[h, rows, LANES:] = qx.astype(BF16)
            kx_ref[h, rows, :] = kx.astype(BF16)

    top_rows = lax.broadcasted_iota(I32, (LANES, blk), 0) < HEAD_DIM
    for j in range(n_blk):
        vt = v_ref[j * blk:(j + 1) * blk, :].T
        vt_ref[0, :, j * blk:(j + 1) * blk] = jnp.where(top_rows, vt, 1.0).astype(BF16)
        vt_ref[1, :, j * blk:(j + 1) * blk] = jnp.where(top_rows, 1.0, vt).astype(BF16)

    key_i = lax.broadcasted_iota(I32, (blk, blk), 0)
    qry_i = lax.broadcasted_iota(I32, (blk, blk), 1)
    causal = key_i <= qry_i

    def q_aug(i, h):
        r0 = pl.multiple_of(i * blk, blk)
        return jnp.concatenate([qa_ref[h, pl.ds(r0, blk), 0:LANES], qa_ref[h, pl.ds(r0, blk), LANES:]],
                               axis=1)

    def k_aug(j, h):
        c0 = pl.multiple_of(j * blk, blk)
        return jnp.concatenate([kb_ref[pl.ds(c0, blk), :], kx_ref[h, pl.ds(c0, blk), :]], axis=1)

    def v_t(j, h):
        return vt_ref[h, :, pl.ds(pl.multiple_of(j * blk, blk), blk)]

    def own_block(i, carry):
        for h in range(2):
            s = lax.dot_general(k_aug(i, h), q_aug(i, h), _NT, preferred_element_type=F32)
            s = jnp.where(causal, s, NEG_INF)
            m0 = jnp.max(s, axis=0, keepdims=True)
            m_ref[i, h] = m0
            acc_ref[i, h] = jnp.dot(v_t(i, h), jnp.exp(s - m0).astype(BF16), preferred_element_type=F32)
        return carry

    lax.fori_loop(0, n_blk, own_block, 0, unroll=2)

    def key_block(j, carry):
        ks = [k_aug(j, h) for h in range(2)]
        vs = [v_t(j, h) for h in range(2)]

        def update(i):
            for h in range(2):
                sj = lax.dot_general(ks[h], q_aug(i, h), _NT, preferred_element_type=F32)
                m = m_ref[i, h]
                m_new = jnp.maximum(m, jnp.max(sj, axis=0, keepdims=True))
                acc_ref[i, h] = jnp.exp(m - m_new) * acc_ref[i, h] + jnp.dot(
                    vs[h], jnp.exp(sj - m_new).astype(BF16), preferred_element_type=F32)
                m_ref[i, h] = m_new

        def tiles(first, count):
            def body(g, c2):
                for d in range(count):
                    update(first + count * g + d)
                return c2
            return body

        n_later = n_blk - 1 - j
        n_four = lax.shift_right_logical(n_later, 2)
        n_two = lax.shift_right_logical(n_later - 4 * n_four, 1)
        lax.fori_loop(0, n_four, tiles(j + 1, 4), 0)
        lax.fori_loop(0, n_two, tiles(j + 1 + 4 * n_four, 2), 0)
        lax.fori_loop(j + 1 + 4 * n_four + 2 * n_two, n_blk, tiles(0, 1), 0)
        return carry

    lax.fori_loop(0, n_blk - 1, key_block, 0)

    def finish(i, carry):
        a0, a1 = acc_ref[i, 0], acc_ref[i, 1]
        out_t = jnp.where(top_rows, a0 / a0[HEAD_DIM:HEAD_DIM + 1, :], a1 / a1[0:1, :])
        o_ref[pl.ds(pl.multiple_of(i * blk, blk), blk), :] = out_t.T
        return carry

    lax.fori_loop(0, n_blk, finish, 0, unroll=2)


def _moba(q, k, v, neg_slopes):
    b, s_len, w = q.shape
    n_pair = w // LANES
    slab = pl.BlockSpec((None, s_len, LANES), lambda i, p: (i, 0, p))
    return pl.pallas_call(
        _moba_kernel,
        grid=(b, n_pair),
        in_specs=[slab, slab, slab,
                  pl.BlockSpec((None, 2, 1, MOBA_BLOCK), lambda i, p: (p, 0, 0, 0))],
        out_specs=slab,
        out_shape=jax.ShapeDtypeStruct((b, s_len, w), F32),
        scratch_shapes=[pltpu.VMEM((2, s_len, 2 * LANES), BF16),
                        pltpu.VMEM((s_len, LANES), BF16),
                        pltpu.VMEM((2, s_len, LANES), BF16),
                        pltpu.VMEM((2, LANES, s_len), BF16),
                        pltpu.VMEM((s_len // MOBA_BLOCK, 2, 1, MOBA_BLOCK), F32),
                        pltpu.VMEM((s_len // MOBA_BLOCK, 2, LANES, MOBA_BLOCK), F32)],
        compiler_params=_cparams(("parallel", "parallel")),
        name="moba",
    )(q, k, v, neg_slopes)


MEMATTN_TQ = 512


def _memattn_kernel(q_ref, k_ref, v_ref, o_ref):
    q = q_ref[...]
    kb = k_ref[...].astype(BF16)
    vb = v_ref[...].astype(BF16)
    head = _shr(lax.broadcasted_iota(I32, q.shape, 1), HEAD_DIM.bit_length() - 1)
    out = jnp.zeros(q.shape, F32)
    for h in range(N_HEADS_MEM):
        qh = jnp.where(head == h, q, 0.0).astype(BF16)
        s = lax.dot_general(qh, kb, _NT, preferred_element_type=F32) * SCALE
        m = jnp.max(s, axis=-1, keepdims=True)
        p = jnp.exp(s - m)
        den = jnp.sum(p, axis=-1, keepdims=True)
        oh = jnp.dot(p.astype(BF16), vb, preferred_element_type=F32) / den
        out = jnp.where(head == h, oh, out)
    o_ref[...] = out


def _memattn(q, k, v):
    b, s_len, w = q.shape
    tq = MEMATTN_TQ
    return pl.pallas_call(
        _memattn_kernel,
        grid=(b, s_len // tq),
        in_specs=[pl.BlockSpec((None, tq, w), lambda i, j: (i, j, 0)),
                  pl.BlockSpec((None, MEM_LEN, w), lambda i, j: (i, 0, 0)),
                  pl.BlockSpec((None, MEM_LEN, w), lambda i, j: (i, 0, 0))],
        out_specs=pl.BlockSpec((None, tq, w), lambda i, j: (i, j, 0)),
        out_shape=jax.ShapeDtypeStruct((b, s_len, w), F32),
        compiler_params=_cparams(("parallel", "parallel")),
        name="memattn",
    )(q, k, v)


OUTPROJ_TM = 512


def _outproj_kernel(od_ref, ob_ref, om_ref, x_ref, gd, gb, gm, wo_ref, gf, wq_ref, k1_ref, k2_ref,
                    x1_ref, hn_ref, s1_ref, s2_ref):
    y = jnp.dot(_row_rms(od_ref[...], gd[...]).astype(BF16), wo_ref[0:W_DIL, :],
                preferred_element_type=F32)
    y += jnp.dot(_row_rms(ob_ref[...], gb[...]).astype(BF16), wo_ref[W_DIL:W_DIL + W_MOBA, :],
                 preferred_element_type=F32)
    y += jnp.dot(_row_rms(om_ref[...], gm[...]).astype(BF16), wo_ref[W_DIL + W_MOBA:, :],
                 preferred_element_type=F32)
    x1 = x_ref[...] + y
    x1_ref[...] = x1
    hb = _row_rms(x1, gf[...]).astype(BF16)
    hn_ref[...] = hb
    half = PEER_DKEY // 2
    for h in range(PEER_HEADS):
        qh = jnp.dot(hb, wq_ref[:, h * PEER_DKEY:(h + 1) * PEER_DKEY],
                     preferred_element_type=F32).astype(BF16)
        s1_ref[h] = lax.dot_general(k1_ref[h], qh[:, :half], _NT, preferred_element_type=F32)
        s2_ref[h] = lax.dot_general(k2_ref[h], qh[:, half:], _NT, preferred_element_type=F32)


def _outproj(o_dil, o_moba, o_mem, x2, og_dil, og_moba, og_mem, w_out, g_ffn, w_q, sub1, sub2):
    t = x2.shape[0]
    tm = OUTPROJ_TM
    row = lambda w: pl.BlockSpec((1, w), lambda i: (0, 0))
    tile = lambda w: pl.BlockSpec((tm, w), lambda i: (i, 0))
    full = lambda shape: pl.BlockSpec(shape, lambda i: (0,) * len(shape))
    score = pl.BlockSpec((PEER_HEADS, PEER_NKEYS, tm), lambda i: (0, 0, i))
    half = PEER_DKEY // 2
    return pl.pallas_call(
        _outproj_kernel,
        grid=(t // tm,),
        in_specs=[tile(W_DIL), tile(W_MOBA), tile(W_MEM), tile(D_MODEL),
                  row(W_DIL), row(W_MOBA), row(W_MEM), full((D_MODEL, D_MODEL)), row(D_MODEL),
                  full((D_MODEL, PEER_HEADS * PEER_DKEY)),
                  full((PEER_HEADS, PEER_NKEYS, half)), full((PEER_HEADS, PEER_NKEYS, half))],
        out_specs=[tile(D_MODEL), tile(D_MODEL), score, score],
        out_shape=[jax.ShapeDtypeStruct((t, D_MODEL), F32),
                   jax.ShapeDtypeStruct((t, D_MODEL), BF16),
                   jax.ShapeDtypeStruct((PEER_HEADS, PEER_NKEYS, t), F32),
                   jax.ShapeDtypeStruct((PEER_HEADS, PEER_NKEYS, t), F32)],
        compiler_params=_cparams(("parallel",)),
        name="outproj",
    )(o_dil, o_moba, o_mem, x2, og_dil, og_moba, og_mem, w_out, g_ffn, w_q, sub1, sub2)


TOPK_TM = 512


def _top_rows(s, k):
    n_rows = s.shape[0]
    rid = lax.broadcasted_iota(I32, s.shape, 0).astype(F32)
    vals, idxs = [], []
    for _ in range(k):
        m = jnp.max(s, axis=0, keepdims=True)
        first = jnp.min(jnp.where(s == m, rid, float(n_rows)), axis=0, keepdims=True)
        vals.append(m)
        idxs.append(first)
        s = jnp.where(rid == first, -jnp.inf, s)
    return jnp.concatenate(vals, axis=0), jnp.concatenate(idxs, axis=0)


def _pick_rows(table, sel):
    out = jnp.zeros(sel.shape, table.dtype)
    for j in range(table.shape[0]):
        out = jnp.where(sel == j, table[j:j + 1, :], out)
    return out


def _candidates(v1, v2):
    sub = lax.broadcasted_iota(I32, (8, v1.shape[1]), 0)
    pieces = [v1[0:1, :] + v2]
    for j1 in range(1, 8):
        pieces.append(jnp.where(sub < PEER_TOPK // (j1 + 1), v1[j1:j1 + 1, :] + v2[0:8, :], -jnp.inf))
    pieces.append(v1[8:16, :] + v2[0:1, :])
    return jnp.concatenate(pieces, axis=0)


def _candidate_coords(row):
    low = row & 7
    j1 = jnp.where(row < 16, 0, jnp.where(row >= 72, 8 + low, _shr(row, 3) - 1))
    j2 = jnp.where(row < 16, row, jnp.where(row >= 72, 0, low))
    return j1, j2


def _retrieve(s1, s2):
    v1, i1 = _top_rows(s1, PEER_TOPK)
    v2, i2 = _top_rows(s2, PEER_TOPK)
    top_s, pos = _top_rows(_candidates(v1, v2), PEER_TOPK)
    e = jnp.exp(top_s - top_s[0:1, :])
    j1, j2 = _candidate_coords(pos.astype(I32))
    return _pick_rows(i1, j1), _pick_rows(i2, j2), e / jnp.sum(e, axis=0, keepdims=True)


def _peertopk_kernel(s1_ref, s2_ref, i1_ref, i2_ref, g_ref):
    n_col = s1_ref.shape[2] // LANES

    def head_body(h, carry):
        for c in range(n_col):
            cols = slice(c * LANES, (c + 1) * LANES)
            i1_ref[h, :, cols], i2_ref[h, :, cols], g_ref[h, :, cols] = _retrieve(
                s1_ref[h, :, cols], s2_ref[h, :, cols])
        return carry

    lax.fori_loop(0, PEER_HEADS, head_body, 0)


def _peertopk(s1, s2):
    t = s1.shape[2]
    tm = TOPK_TM
    score = pl.BlockSpec((PEER_HEADS, PEER_NKEYS, tm), lambda i: (0, 0, i))
    slot = pl.BlockSpec((PEER_HEADS, PEER_TOPK, tm), lambda i: (0, 0, i))
    return pl.pallas_call(
        _peertopk_kernel,
        grid=(t // tm,),
        in_specs=[score, score],
        out_specs=[slot, slot, slot],
        out_shape=[jax.ShapeDtypeStruct((PEER_HEADS, PEER_TOPK, t), F32)] * 3,
        compiler_params=_cparams(("parallel",)),
        name="peertopk",
    )(s1, s2)


PEER_TM = 512
PEER_TE = 2048
PEER_KEY_PAIRS = PEER_NKEYS // 2
PEER_PITCH = PEER_TM + 8
PEER_BUILD_UNROLL = 32
BF16_ROWS = 16


def _rows_bf16(row, n_rows):
    tile = jnp.broadcast_to(row, (BF16_ROWS, row.shape[1])).astype(BF16)
    return jnp.concatenate([tile] * (n_rows // BF16_ROWS), axis=0)


def _peerffn_kernel(h_ref, x1_ref, i1_ref, i2_ref, g_ref, u_ref, v_ref, o_ref, gate_ref):
    c = pl.program_id(1)
    slabs = PEER_TE // PEER_NKEYS

    @pl.when(c == 0)
    def _build_gate_matrix():
        key_id = lax.broadcasted_iota(I32, (PEER_NKEYS, PEER_SLOTS), 0).astype(BF16)
        one = jnp.ones((PEER_NKEYS, PEER_SLOTS), BF16)
        zero = jnp.zeros((PEER_NKEYS, PEER_SLOTS), BF16)

        def token(t, carry):
            i1 = _rows_bf16(i1_ref[pl.ds(t, 1), :], PEER_NKEYS)
            i2 = _rows_bf16(i2_ref[pl.ds(t, 1), :], PEER_NKEYS)
            g = _rows_bf16(g_ref[pl.ds(t, 1), :], PEER_NKEYS)
            lhs = jnp.where(key_id == i1, g, zero)
            rhs = jnp.where(key_id == i2, one, zero)
            gt = lax.dot_general(lhs, rhs, _NT, preferred_element_type=F32)
            gate_ref[pl.ds(t, PEER_KEY_PAIRS, stride=PEER_PITCH), :] = pltpu.bitcast(gt.astype(BF16), U32)
            return carry

        lax.fori_loop(0, PEER_TM, token, 0, unroll=PEER_BUILD_UNROLL)

    a = lax.dot_general(h_ref[...], u_ref[...], _NT, preferred_element_type=F32)
    ws = []
    for r in range(slabs // 2):
        start = pl.multiple_of((c * (slabs // 2) + r) * PEER_PITCH, 8)
        word = gate_ref[pl.ds(start, PEER_TM), :]
        g_even = lax.bitcast_convert_type(lax.shift_left(word, jnp.uint32(16)), F32)
        g_odd = lax.bitcast_convert_type(word & jnp.uint32(0xFFFF0000), F32)
        for j, gj in ((2 * r, g_even), (2 * r + 1, g_odd)):
            aj = a[:, j * PEER_NKEYS:(j + 1) * PEER_NKEYS]
            ws.append((0.5 * aj * (1.0 + lax.erf(aj * SQRT_HALF)) * gj).astype(BF16))
    y = jnp.dot(jnp.concatenate(ws, axis=1), v_ref[...], preferred_element_type=F32)

    @pl.when(c == 0)
    def _first():
        o_ref[...] = x1_ref[...] + y

    @pl.when(c > 0)
    def _rest():
        o_ref[...] += y


def _peerffn(hn, x1, i1, i2, gate, u, v):
    t = hn.shape[0]
    tm, te = PEER_TM, PEER_TE
    tile = lambda w: pl.BlockSpec((tm, w), lambda i, c: (i, 0))
    chunk = pl.BlockSpec((te, D_MODEL), lambda i, c: (c, 0))
    return pl.pallas_call(
        _peerffn_kernel,
        grid=(t // tm, PEER_EXPERTS // te),
        in_specs=[tile(D_MODEL), tile(D_MODEL), tile(PEER_SLOTS), tile(PEER_SLOTS), tile(PEER_SLOTS),
                  chunk, chunk],
        out_specs=tile(D_MODEL),
        out_shape=jax.ShapeDtypeStruct((t, D_MODEL), F32),
        scratch_shapes=[pltpu.VMEM((PEER_KEY_PAIRS * PEER_PITCH, LANES), U32)],
        compiler_params=_cparams(("parallel", "arbitrary")),
        name="peerffn",
    )(hn, x1, i1, i2, gate, u, v)


def _neg_slope_rows(slopes, width):
    n = slopes.shape[0]
    return jnp.broadcast_to((-slopes).reshape(n // 2, 2, 1, 1), (n // 2, 2, 1, width))


def _tiled_gain(g, reps):
    return jnp.tile(g, reps)[None, :]


def _layer(x, mem, g_mix, w_in, qg_dil, kg_dil, qg_moba, kg_moba, qg_mem, kg_mem, g_memtok, w_mem_kv,
           og_dil, og_moba, og_mem, w_out, g_ffn, w_peer_q, sub1, sub2, peer_u, peer_v):
    b, s_len, d = x.shape
    t = b * s_len
    n_mix = N_HEADS_DIL + N_HEADS_MOBA
    slopes = jnp.exp2(-8.0 * jnp.arange(1, n_mix + 1, dtype=F32) / n_mix)
    ns_dil = _neg_slope_rows(slopes[0::2], 2 * BAND_BLOCK)
    ns_moba = _neg_slope_rows(slopes[1::2], MOBA_BLOCK)

    x2 = x.reshape(t, d)
    k_m, v_m = _memkv(mem, g_memtok[None, :], w_mem_kv.astype(BF16), _tiled_gain(kg_mem, N_HEADS_MEM))
    q_d, k_d, v_d, q_b, k_b, v_b, q_m = _inproj(
        x2, g_mix[None, :], w_in.astype(BF16),
        _tiled_gain(qg_dil, N_HEADS_DIL), _tiled_gain(kg_dil, N_HEADS_DIL),
        _tiled_gain(qg_moba, N_HEADS_MOBA), _tiled_gain(kg_moba, N_HEADS_MOBA),
        _tiled_gain(qg_mem, N_HEADS_MEM))
    seq = lambda a: a.reshape(b, s_len, a.shape[-1])
    o_dil = _dilated(seq(q_d), seq(k_d), seq(v_d), ns_dil)
    o_moba = _moba(seq(q_b), seq(k_b), seq(v_b), ns_moba)
    o_mem = _memattn(seq(q_m), k_m, v_m)
    x1, hn, s1, s2 = _outproj(
        o_dil.reshape(t, W_DIL), o_moba.reshape(t, W_MOBA), o_mem.reshape(t, W_MEM), x2,
        og_dil[None, :], og_moba[None, :], og_mem[None, :], w_out.astype(BF16), g_ffn[None, :],
        w_peer_q.astype(BF16), sub1.astype(BF16), sub2.astype(BF16))
    i1, i2, gate = _peertopk(s1, s2)
    slots = lambda a: a.reshape(PEER_SLOTS, t).T
    out = _peerffn(hn, x1, slots(i1), slots(i2), slots(gate), peer_u.astype(BF16), peer_v.astype(BF16))
    return out.reshape(b, s_len, d)


def kernel(x, mem, g_mix, w_in, qg_dil, kg_dil, qg_moba, kg_moba, qg_mem, kg_mem, g_memtok, w_mem_kv,
           og_dil, og_moba, og_mem, w_out, g_ffn, w_peer_q, peer_subkeys_1, peer_subkeys_2, peer_u,
           peer_v):
    h = x
    for layer in range(g_mix.shape[0]):
        h = _layer(h, mem, g_mix[layer], w_in[layer], qg_dil[layer], kg_dil[layer], qg_moba[layer],
                   kg_moba[layer], qg_mem[layer], kg_mem[layer], g_memtok[layer], w_mem_kv[layer],
                   og_dil[layer], og_moba[layer], og_mem[layer], w_out[layer], g_ffn[layer],
                   w_peer_q[layer], peer_subkeys_1[layer], peer_subkeys_2[layer], peer_u[layer],
                   peer_v[layer])
    return h
```

```python
import functools
import math

import jax
import jax.numpy as jnp
from jax import lax
from jax.experimental import pallas as pl
from jax.experimental.pallas import tpu as pltpu

F32 = jnp.float32
BF16 = jnp.bfloat16
I32 = jnp.int32
U32 = jnp.uint32

LANES = 128
D_MODEL = 1024
N_HEADS_DIL = 6
N_HEADS_MOBA = 6
N_HEADS_MEM = 4
HEAD_DIM = 64
W_DIL = N_HEADS_DIL * HEAD_DIM
W_MOBA = N_HEADS_MOBA * HEAD_DIM
W_MEM = N_HEADS_MEM * HEAD_DIM
IN_WIDTH = 3 * W_DIL + 3 * W_MOBA + W_MEM
DIL_CONFIGS = ((128, 1), (512, 4), (2048, 16))
BAND_BLOCK = 128
MOBA_BLOCK = 256
MOBA_TOPK = 3
MEM_LEN = 256
PEER_HEADS = 8
PEER_NKEYS = 128
PEER_EXPERTS = PEER_NKEYS * PEER_NKEYS
PEER_TOPK = 16
PEER_DKEY = 256
PEER_SLOTS = PEER_HEADS * PEER_TOPK
RMS_EPS = 1e-6
NEG_INF = -1e30
SCALE = 1.0 / math.sqrt(HEAD_DIM)
SQRT_HALF = math.sqrt(0.5)

VMEM_LIMIT = 56 * 1024 * 1024

_NT = (((1,), (1,)), ((), ()))


def _shr(a, bits):
    return lax.shift_right_logical(a, jnp.full(a.shape, bits, a.dtype))


def _shr_scalar(a, bits):
    return lax.shift_right_logical(a, jnp.int32(bits))


def _cparams(sem):
    return pltpu.CompilerParams(dimension_semantics=sem, vmem_limit_bytes=VMEM_LIMIT)


def _row_rms(a, gain):
    return a * lax.rsqrt(jnp.mean(a * a, axis=-1, keepdims=True) + RMS_EPS) * gain


def _group_mean_sq(p, group):
    w = p.shape[-1]
    bits = group.bit_length() - 1
    gi = _shr(lax.broadcasted_iota(I32, (w, w), 0), bits)
    gj = _shr(lax.broadcasted_iota(I32, (w, w), 1), bits)
    ones_bd = jnp.where(gi == gj, 1.0, 0.0).astype(BF16)
    p2 = p * p
    hi = p2.astype(BF16)
    lo = (p2 - hi.astype(F32)).astype(BF16)
    ss = (jnp.dot(hi, ones_bd, preferred_element_type=F32)
          + jnp.dot(lo, ones_bd, preferred_element_type=F32))
    return ss * (1.0 / group)


def _head_rms(p, gain):
    return p * lax.rsqrt(_group_mean_sq(p, HEAD_DIM) + RMS_EPS) * gain


def _memkv_kernel(mem_ref, g_ref, w_ref, kg_ref, k_ref, v_ref):
    hn = _row_rms(mem_ref[...], g_ref[...]).astype(BF16)
    kv = jnp.dot(hn, w_ref[...], preferred_element_type=F32)
    k_ref[...] = _head_rms(kv[:, :W_MEM], kg_ref[...])
    v_ref[...] = kv[:, W_MEM:]


def _memkv(mem, g_memtok, w_kv, kg_mem):
    b = mem.shape[0]
    return pl.pallas_call(
        _memkv_kernel,
        grid=(b,),
        in_specs=[
            pl.BlockSpec((None, MEM_LEN, D_MODEL), lambda i: (i, 0, 0)),
            pl.BlockSpec((1, D_MODEL), lambda i: (0, 0)),
            pl.BlockSpec((D_MODEL, 2 * W_MEM), lambda i: (0, 0)),
            pl.BlockSpec((1, W_MEM), lambda i: (0, 0)),
        ],
        out_specs=[pl.BlockSpec((None, MEM_LEN, W_MEM), lambda i: (i, 0, 0))] * 2,
        out_shape=[jax.ShapeDtypeStruct((b, MEM_LEN, W_MEM), F32)] * 2,
        compiler_params=_cparams(("parallel",)),
        name="memkv",
    )(mem, g_memtok, w_kv, kg_mem)


INPROJ_TM = 512


def _inproj_kernel(x_ref, g_ref, w_ref, gqd, gkd, gqb, gkb, gqm, qd, kd, vd, qb, kb, vb, qm):
    hb = _row_rms(x_ref[...], g_ref[...]).astype(BF16)

    def seg(lo, width):
        return jnp.dot(hb, w_ref[:, lo:lo + width], preferred_element_type=F32)

    qd[...] = _head_rms(seg(0, W_DIL), gqd[...])
    kd[...] = _head_rms(seg(W_DIL, W_DIL), gkd[...])
    vd[...] = seg(2 * W_DIL, W_DIL)
    base = 3 * W_DIL
    qb[...] = _head_rms(seg(base, W_MOBA), gqb[...])
    kb[...] = _head_rms(seg(base + W_MOBA, W_MOBA), gkb[...])
    vb[...] = seg(base + 2 * W_MOBA, W_MOBA)
    qm[...] = _head_rms(seg(base + 3 * W_MOBA, W_MEM), gqm[...])


def _inproj(x2, g_mix, w_in, gqd, gkd, gqb, gkb, gqm):
    t = x2.shape[0]
    tm = INPROJ_TM
    row = lambda w: pl.BlockSpec((1, w), lambda i: (0, 0))
    tile = lambda w: pl.BlockSpec((tm, w), lambda i: (i, 0))
    widths = (W_DIL, W_DIL, W_DIL, W_MOBA, W_MOBA, W_MOBA, W_MEM)
    return pl.pallas_call(
        _inproj_kernel,
        grid=(t // tm,),
        in_specs=[tile(D_MODEL), row(D_MODEL), pl.BlockSpec((D_MODEL, IN_WIDTH), lambda i: (0, 0)),
                  row(W_DIL), row(W_DIL), row(W_MOBA), row(W_MOBA), row(W_MEM)],
        out_specs=[tile(w) for w in widths],
        out_shape=[jax.ShapeDtypeStruct((t, w), F32) for w in widths],
        compiler_params=_cparams(("parallel",)),
        name="inproj",
    )(x2, g_mix, w_in, gqd, gkd, gqb, gkb, gqm)


DIL_PAD = BAND_BLOCK * max(d for _, d in DIL_CONFIGS)
DIL_MIX_ROWS = 256
DIL_UNROLL = 8


def _dil_kernel(q_ref, k_ref, v_ref, ns_ref, o_ref, kpad, vpad, ob0, ob1, ob2, lb0, lb1, lb2):
    s_len = q_ref.shape[0]
    zeros = jnp.zeros((DIL_PAD, LANES), F32)
    kpad[0:DIL_PAD, :] = zeros
    vpad[0:DIL_PAD, :] = zeros
    kpad[DIL_PAD:, :] = k_ref[...]
    vpad[DIL_PAD:, :] = v_ref[...]

    blk = BAND_BLOCK
    head0 = lax.broadcasted_iota(I32, (blk, LANES), 1) < HEAD_DIM
    ql = lax.broadcasted_iota(I32, (blk, 2 * blk), 0)
    kl = lax.broadcasted_iota(I32, (blk, 2 * blk), 1)
    delta = blk + ql - kl
    obs = (ob0, ob1, ob2)
    lbs = (lb0, lb1, lb2)

    for c, (window, dil) in enumerate(DIL_CONFIGS):
        reach = window // dil
        n_blk = s_len // dil // blk
        in_band = (delta >= 0) & (delta <= reach)
        dist = (delta * dil).astype(F32)
        ob, lb = obs[c], lbs[c]

        def rows(start, size, dil=dil):
            return pl.ds(start, size) if dil == 1 else pl.ds(start, size, stride=dil)

        def block_body(n, r, dil=dil, in_band=in_band, dist=dist, ob=ob, lb=lb, rows=rows):
            q_start = r + dil * blk * n
            k_start = DIL_PAD + r + dil * blk * (n - 1)
            qv = q_ref[rows(q_start, blk), :]
            kv = kpad[rows(k_start, 2 * blk), :].astype(BF16)
            vv = vpad[rows(k_start, 2 * blk), :].astype(BF16)
            valid = in_band & (kl >= jnp.where(n == 0, blk, 0))
            outs, lses = [], []
            for h in range(2):
                qh = jnp.where(head0 if h == 0 else ~head0, qv, 0.0).astype(BF16)
                s = lax.dot_general(qh, kv, _NT, preferred_element_type=F32) * SCALE
                s = jnp.where(valid, s + ns_ref[h] * dist, NEG_INF)
                m = jnp.max(s, axis=-1, keepdims=True)
                p = jnp.exp(s - m)
                den = jnp.sum(p, axis=-1, keepdims=True)
                lses.append(jnp.broadcast_to(m + jnp.log(den), (blk, LANES)))
                outs.append(jnp.dot(p.astype(BF16), vv, preferred_element_type=F32) / den)
            ob[rows(q_start, blk), :] = jnp.where(head0, outs[0], outs[1])
            lb[rows(q_start, blk), :] = jnp.where(head0, lses[0], lses[1])

        def flat_body(idx, carry, n_blk=n_blk, block_body=block_body):
            block_body(idx & (n_blk - 1), _shr_scalar(idx, n_blk.bit_length() - 1))
            return carry

        lax.fori_loop(0, dil * n_blk, flat_body, 0, unroll=DIL_UNROLL)

    def mix(i, carry):
        sl = pl.ds(pl.multiple_of(i * DIL_MIX_ROWS, DIL_MIX_ROWS), DIL_MIX_ROWS)
        l0, l1, l2 = lb0[sl, :], lb1[sl, :], lb2[sl, :]
        mx = jnp.maximum(jnp.maximum(l0, l1), l2)
        e0, e1, e2 = jnp.exp(l0 - mx), jnp.exp(l1 - mx), jnp.exp(l2 - mx)
        tot = e0 + e1 + e2
        o_ref[sl, :] = (e0 / tot) * ob0[sl, :] + (e1 / tot) * ob1[sl, :] + (e2 / tot) * ob2[sl, :]
        return carry

    lax.fori_loop(0, s_len // DIL_MIX_ROWS, mix, 0)


def _dilated(q, k, v, neg_slopes):
    b, s_len, w = q.shape
    n_pair = w // LANES
    slab = pl.BlockSpec((None, s_len, LANES), lambda i, p: (i, 0, p))
    buf = lambda rows: pltpu.VMEM((rows, LANES), F32)
    return pl.pallas_call(
        _dil_kernel,
        grid=(b, n_pair),
        in_specs=[slab, slab, slab,
                  pl.BlockSpec((None, 2, 1, 2 * BAND_BLOCK), lambda i, p: (p, 0, 0, 0))],
        out_specs=slab,
        out_shape=jax.ShapeDtypeStruct((b, s_len, w), F32),
        scratch_shapes=[buf(DIL_PAD + s_len), buf(DIL_PAD + s_len)] + [buf(s_len)] * 6,
        compiler_params=_cparams(("parallel", "parallel")),
        name="dilated",
    )(q, k, v, neg_slopes)


def _split_bf16(a):
    hi = a.astype(BF16)
    lo = (a - hi.astype(F32)).astype(BF16)
    return hi, lo


def _split3_bf16(a):
    t0 = a.astype(BF16).astype(F32)
    t1 = (a - t0).astype(BF16).astype(F32)
    t2 = (a - t0 - t1).astype(BF16).astype(F32)
    return t0, t1, t2


MOBA_QB = 16
MOBA_KB = 19


def _moba_kernel(q_ref, k_ref, v_ref, ns_ref, o_ref, qa_ref, kb_ref, kx_ref, vt_ref, m_ref, acc_ref):
    s_len = q_ref.shape[0]
    blk = MOBA_BLOCK
    n_blk = s_len // blk
    lane = lax.broadcasted_iota(I32, (blk, LANES), 1)
    head0 = lane < HEAD_DIM
    pos0 = lax.broadcasted_iota(I32, (blk, LANES), 0).astype(F32)

    kb_ref[...] = k_ref[...].astype(BF16)

    k_mean = jnp.concatenate(
        [jnp.sum(k_ref[j * blk:(j + 1) * blk, :], axis=0, keepdims=True) for j in range(n_blk)],
        axis=0) * (1.0 / blk)
    q_hi, q_lo = _split_bf16(q_ref[...])

    blk_id = lax.broadcasted_iota(I32, (n_blk, s_len), 0)
    n_past = _shr(lax.broadcasted_iota(I32, (n_blk, s_len), 1), blk.bit_length() - 1)
    past = blk_id < n_past
    blk_f = blk_id.astype(F32)
    pad_rows = LANES - n_blk

    for h in range(2):
        hm = (lax.broadcasted_iota(I32, (n_blk, LANES), 1) < HEAD_DIM) == (h == 0)
        km_hi, km_lo = _split_bf16(jnp.where(hm, k_mean, 0.0))
        gate = (lax.dot_general(km_hi, q_hi, _NT, preferred_element_type=F32)
                + lax.dot_general(km_hi, q_lo, _NT, preferred_element_type=F32)
                + lax.dot_general(km_lo, q_hi, _NT, preferred_element_type=F32))
        g = jnp.where(past, gate, NEG_INF)
        sel = jnp.zeros((n_blk, s_len), F32)
        for _ in range(MOBA_TOPK):
            m = jnp.max(g, axis=0, keepdims=True)
            first = jnp.min(jnp.where(g == m, blk_f, float(n_blk)), axis=0, keepdims=True)
            pick = blk_f == first
            sel = jnp.where(pick, 1.0, sel)
            g = jnp.where(pick, -jnp.inf, g)
        not_sel = jnp.where(past, 1.0 - sel, jnp.where(blk_id == n_past, 0.0, 1.0))
        not_sel = jnp.concatenate([not_sel, jnp.zeros((pad_rows, s_len), F32)], axis=0)
        ns = ns_ref[h][:, 0:LANES]
        for i in range(n_blk):
            rows = slice(i * blk, (i + 1) * blk)
            qh = jnp.where(head0 if h == 0 else ~head0, q_ref[rows, :], 0.0) * SCALE
            qa_ref[h, rows, 0:LANES] = qh.astype(BF16)
            t_pos = pos0 + float(i * blk)
            q0, q1, q2 = _split3_bf16(ns * t_pos)
            k0, k1, k2 = _split3_bf16(-ns * t_pos)
            qx = not_sel[:, rows].T
            kx = jnp.where(lane == i, NEG_INF, 0.0)
            for d, (qt, kt) in enumerate(((q0, k0), (q1, k1), (q2, k2))):
                qx = jnp.where(lane == MOBA_QB + d, qt, jnp.where(lane == MOBA_KB + d, 1.0, qx))
                kx = jnp.where(lane == MOBA_QB + d, 1.0, jnp.where(lane == MOBA_KB + d, kt, kx))
            qa_ref[h, rows, LANES:] = qx.astype(BF16)
            kx_ref[h, rows, :] = kx.astype(BF16)

    top_rows = lax.broadcasted_iota(I32, (LANES, blk), 0) < HEAD_DIM
    for j in range(n_blk):
        vt = v_ref[j * blk:(j + 1) * blk, :].T
        vt_ref[0, :, j * blk:(j + 1) * blk] = jnp.where(top_rows, vt, 1.0).astype(BF16)
        vt_ref[1, :, j * blk:(j + 1) * blk] = jnp.where(top_rows, 1.0, vt).astype(BF16)

    key_i = lax.broadcasted_iota(I32, (blk, blk), 0)
    qry_i = lax.broadcasted_iota(I32, (blk, blk), 1)
    causal = key_i <= qry_i

    def q_aug(i, h):
        r0 = pl.multiple_of(i * blk, blk)
        return jnp.concatenate([qa_ref[h, pl.ds(r0, blk), 0:LANES], qa_ref[h, pl.ds(r0, blk), LANES:]],
                               axis=1)

    def k_aug(j, h):
        c0 = pl.multiple_of(j * blk, blk)
        return jnp.concatenate([kb_ref[pl.ds(c0, blk), :], kx_ref[h, pl.ds(c0, blk), :]], axis=1)

    def v_t(j, h):
        return vt_ref[h, :, pl.ds(pl.multiple_of(j * blk, blk), blk)]

    def own_block(i, carry):
        for h in range(2):
            s = lax.dot_general(k_aug(i, h), q_aug(i, h), _NT, preferred_element_type=F32)
            s = jnp.where(causal, s, NEG_INF)
            m0 = jnp.max(s, axis=0, keepdims=True)
            m_ref[i, h] = m0
            acc_ref[i, h] = jnp.dot(v_t(i, h), jnp.exp(s - m0).astype(BF16), preferred_element_type=F32)
        return carry

    lax.fori_loop(0, n_blk, own_block, 0, unroll=2)

    def key_block(j, carry):
        ks = [k_aug(j, h) for h in range(2)]
        vs = [v_t(j, h) for h in range(2)]

        def update(i):
            for h in range(2):
                sj = lax.dot_general(ks[h], q_aug(i, h), _NT, preferred_element_type=F32)
                m = m_ref[i, h]
                m_new = jnp.maximum(m, jnp.max(sj, axis=0, keepdims=True))
                acc_ref[i, h] = jnp.exp(m - m_new) * acc_ref[i, h] + jnp.dot(
                    vs[h], jnp.exp(sj - m_new).astype(BF16), preferred_element_type=F32)
                m_ref[i, h] = m_new

        def tiles(first, count):
            def body(g, c2):
                for d in range(count):
                    update(first + count * g + d)
                return c2
            return body

        n_later = n_blk - 1 - j
        n_four = lax.shift_right_logical(n_later, 2)
        n_two = lax.shift_right_logical(n_later - 4 * n_four, 1)
        lax.fori_loop(0, n_four, tiles(j + 1, 4), 0)
        lax.fori_loop(0, n_two, tiles(j + 1 + 4 * n_four, 2), 0)
        lax.fori_loop(j + 1 + 4 * n_four + 2 * n_two, n_blk, tiles(0, 1), 0)
        return carry

    lax.fori_loop(0, n_blk - 1, key_block, 0)

    def finish(i, carry):
        a0, a1 = acc_ref[i, 0], acc_ref[i, 1]
        out_t = jnp.where(top_rows, a0 / a0[HEAD_DIM:HEAD_DIM + 1, :], a1 / a1[0:1, :])
        o_ref[pl.ds(pl.multiple_of(i * blk, blk), blk), :] = out_t.T
        return carry

    lax.fori_loop(0, n_blk, finish, 0, unroll=2)


def _moba(q, k, v, neg_slopes):
    b, s_len, w = q.shape
    n_pair = w // LANES
    slab = pl.BlockSpec((None, s_len, LANES), lambda i, p: (i, 0, p))
    return pl.pallas_call(
        _moba_kernel,
        grid=(b, n_pair),
        in_specs=[slab, slab, slab,
                  pl.BlockSpec((None, 2, 1, MOBA_BLOCK), lambda i, p: (p, 0, 0, 0))],
        out_specs=slab,
        out_shape=jax.ShapeDtypeStruct((b, s_len, w), F32),
        scratch_shapes=[pltpu.VMEM((2, s_len, 2 * LANES), BF16),
                        pltpu.VMEM((s_len, LANES), BF16),
                        pltpu.VMEM((2, s_len, LANES), BF16),
                        pltpu.VMEM((2, LANES, s_len), BF16),
                        pltpu.VMEM((s_len // MOBA_BLOCK, 2, 1, MOBA_BLOCK), F32),
                        pltpu.VMEM((s_len // MOBA_BLOCK, 2, LANES, MOBA_BLOCK), F32)],
        compiler_params=_cparams(("parallel", "parallel")),
        name="moba",
    )(q, k, v, neg_slopes)


MEMATTN_TQ = 512


def _memattn_kernel(q_ref, k_ref, v_ref, o_ref):
    q = q_ref[...]
    kb = k_ref[...].astype(BF16)
    vb = v_ref[...].astype(BF16)
    head = _shr(lax.broadcasted_iota(I32, q.shape, 1), HEAD_DIM.bit_length() - 1)
    out = jnp.zeros(q.shape, F32)
    for h in range(N_HEADS_MEM):
        qh = jnp.where(head == h, q, 0.0).astype(BF16)
        s = lax.dot_general(qh, kb, _NT, preferred_element_type=F32) * SCALE
        m = jnp.max(s, axis=-1, keepdims=True)
        p = jnp.exp(s - m)
        den = jnp.sum(p, axis=-1, keepdims=True)
        oh = jnp.dot(p.astype(BF16), vb, preferred_element_type=F32) / den
        out = jnp.where(head == h, oh, out)
    o_ref[...] = out


def _memattn(q, k, v):
    b, s_len, w = q.shape
    tq = MEMATTN_TQ
    return pl.pallas_call(
        _memattn_kernel,
        grid=(b, s_len // tq),
        in_specs=[pl.BlockSpec((None, tq, w), lambda i, j: (i, j, 0)),
                  pl.BlockSpec((None, MEM_LEN, w), lambda i, j: (i, 0, 0)),
                  pl.BlockSpec((None, MEM_LEN, w), lambda i, j: (i, 0, 0))],
        out_specs=pl.BlockSpec((None, tq, w), lambda i, j: (i, j, 0)),
        out_shape=jax.ShapeDtypeStruct((b, s_len, w), F32),
        compiler_params=_cparams(("parallel", "parallel")),
        name="memattn",
    )(q, k, v)


OUTPROJ_TM = 512


def _outproj_kernel(od_ref, ob_ref, om_ref, x_ref, gd, gb, gm, wo_ref, gf, wq_ref, k1_ref, k2_ref,
                    x1_ref, hn_ref, s1_ref, s2_ref):
    y = jnp.dot(_row_rms(od_ref[...], gd[...]).astype(BF16), wo_ref[0:W_DIL, :],
                preferred_element_type=F32)
    y += jnp.dot(_row_rms(ob_ref[...], gb[...]).astype(BF16), wo_ref[W_DIL:W_DIL + W_MOBA, :],
                 preferred_element_type=F32)
    y += jnp.dot(_row_rms(om_ref[...], gm[...]).astype(BF16), wo_ref[W_DIL + W_MOBA:, :],
                 preferred_element_type=F32)
    x1 = x_ref[...] + y
    x1_ref[...] = x1
    hb = _row_rms(x1, gf[...]).astype(BF16)
    hn_ref[...] = hb
    half = PEER_DKEY // 2
    for h in range(PEER_HEADS):
        qh = jnp.dot(hb, wq_ref[:, h * PEER_DKEY:(h + 1) * PEER_DKEY],
                     preferred_element_type=F32).astype(BF16)
        s1_ref[h] = lax.dot_general(k1_ref[h], qh[:, :half], _NT, preferred_element_type=F32)
        s2_ref[h] = lax.dot_general(k2_ref[h], qh[:, half:], _NT, preferred_element_type=F32)


def _outproj(o_dil, o_moba, o_mem, x2, og_dil, og_moba, og_mem, w_out, g_ffn, w_q, sub1, sub2):
    t = x2.shape[0]
    tm = OUTPROJ_TM
    row = lambda w: pl.BlockSpec((1, w), lambda i: (0, 0))
    tile = lambda w: pl.BlockSpec((tm, w), lambda i: (i, 0))
    full = lambda shape: pl.BlockSpec(shape, lambda i: (0,) * len(shape))
    score = pl.BlockSpec((PEER_HEADS, PEER_NKEYS, tm), lambda i: (0, 0, i))
    half = PEER_DKEY // 2
    return pl.pallas_call(
        _outproj_kernel,
        grid=(t // tm,),
        in_specs=[tile(W_DIL), tile(W_MOBA), tile(W_MEM), tile(D_MODEL),
                  row(W_DIL), row(W_MOBA), row(W_MEM), full((D_MODEL, D_MODEL)), row(D_MODEL),
                  full((D_MODEL, PEER_HEADS * PEER_DKEY)),
                  full((PEER_HEADS, PEER_NKEYS, half)), full((PEER_HEADS, PEER_NKEYS, half))],
        out_specs=[tile(D_MODEL), tile(D_MODEL), score, score],
        out_shape=[jax.ShapeDtypeStruct((t, D_MODEL), F32),
                   jax.ShapeDtypeStruct((t, D_MODEL), BF16),
                   jax.ShapeDtypeStruct((PEER_HEADS, PEER_NKEYS, t), F32),
                   jax.ShapeDtypeStruct((PEER_HEADS, PEER_NKEYS, t), F32)],
        compiler_params=_cparams(("parallel",)),
        name="outproj",
    )(o_dil, o_moba, o_mem, x2, og_dil, og_moba, og_mem, w_out, g_ffn, w_q, sub1, sub2)


TOPK_TM = 512
TOPK_COLS_PER_ITER = 8


def _top_rows(s, k):
    n_rows = s.shape[0]
    rid = lax.broadcasted_iota(I32, s.shape, 0).astype(F32)
    vals, idxs = [], []
    for _ in range(k):
        m = jnp.max(s, axis=0, keepdims=True)
        first = jnp.min(jnp.where(s == m, rid, float(n_rows)), axis=0, keepdims=True)
        vals.append(m)
        idxs.append(first)
        s = jnp.where(rid == first, -jnp.inf, s)
    return jnp.concatenate(vals, axis=0), jnp.concatenate(idxs, axis=0)


def _pick_rows(table, sel):
    out = jnp.zeros(sel.shape, table.dtype)
    for j in range(table.shape[0]):
        out = jnp.where(sel == j, table[j:j + 1, :], out)
    return out


def _candidates(v1, v2):
    sub = lax.broadcasted_iota(I32, (8, v1.shape[1]), 0)
    pieces = [v1[0:1, :] + v2]
    for j1 in range(1, 8):
        pieces.append(jnp.where(sub < PEER_TOPK // (j1 + 1), v1[j1:j1 + 1, :] + v2[0:8, :], -jnp.inf))
    pieces.append(v1[8:16, :] + v2[0:1, :])
    return jnp.concatenate(pieces, axis=0)


def _candidate_coords(row):
    low = row & 7
    j1 = jnp.where(row < 16, 0, jnp.where(row >= 72, 8 + low, _shr(row, 3) - 1))
    j2 = jnp.where(row < 16, row, jnp.where(row >= 72, 0, low))
    return j1, j2


def _retrieve(s1, s2):
    v1, i1 = _top_rows(s1, PEER_TOPK)
    v2, i2 = _top_rows(s2, PEER_TOPK)
    top_s, pos = _top_rows(_candidates(v1, v2), PEER_TOPK)
    e = jnp.exp(top_s - top_s[0:1, :])
    j1, j2 = _candidate_coords(pos.astype(I32))
    return _pick_rows(i1, j1), _pick_rows(i2, j2), e / jnp.sum(e, axis=0, keepdims=True)


def _peertopk_kernel(s1_ref, s2_ref, i1_ref, i2_ref, g_ref):
    n_col = s1_ref.shape[2] // LANES

    def body(i, carry):
        for c in range(TOPK_COLS_PER_ITER):
            item = i * TOPK_COLS_PER_ITER + c
            h = _shr_scalar(item, n_col.bit_length() - 1)
            cols = pl.ds(pl.multiple_of((item & (n_col - 1)) * LANES, LANES), LANES)
            i1_ref[h, :, cols], i2_ref[h, :, cols], g_ref[h, :, cols] = _retrieve(
                s1_ref[h, :, cols], s2_ref[h, :, cols])
        return carry

    lax.fori_loop(0, PEER_HEADS * n_col // TOPK_COLS_PER_ITER, body, 0)


def _peertopk(s1, s2):
    t = s1.shape[2]
    tm = TOPK_TM
    score = pl.BlockSpec((PEER_HEADS, PEER_NKEYS, tm), lambda i: (0, 0, i))
    slot = pl.BlockSpec((PEER_HEADS, PEER_TOPK, tm), lambda i: (0, 0, i))
    return pl.pallas_call(
        _peertopk_kernel,
        grid=(t // tm,),
        in_specs=[score, score],
        out_specs=[slot, slot, slot],
        out_shape=[jax.ShapeDtypeStruct((PEER_HEADS, PEER_TOPK, t), F32)] * 3,
        compiler_params=_cparams(("parallel",)),
        name="peertopk",
    )(s1, s2)


PEER_TM = 512
PEER_TE = 2048
PEER_KEY_PAIRS = PEER_NKEYS // 2
PEER_PITCH = PEER_TM + 8
PEER_BUILD_UNROLL = 32
BF16_ROWS = 16


def _rows_bf16(row, n_rows):
    tile = jnp.broadcast_to(row, (BF16_ROWS, row.shape[1])).astype(BF16)
    return jnp.concatenate([tile] * (n_rows // BF16_ROWS), axis=0)


def _peerffn_kernel(h_ref, x1_ref, i1_ref, i2_ref, g_ref, u_ref, v_ref, o_ref, gate_ref):
    c = pl.program_id(1)
    slabs = PEER_TE // PEER_NKEYS

    @pl.when(c == 0)
    def _build_gate_matrix():
        key_id = lax.broadcasted_iota(I32, (PEER_NKEYS, PEER_SLOTS), 0).astype(BF16)
        one = jnp.ones((PEER_NKEYS, PEER_SLOTS), BF16)
        zero = jnp.zeros((PEER_NKEYS, PEER_SLOTS), BF16)

        def token(t, carry):
            i1 = _rows_bf16(i1_ref[pl.ds(t, 1), :], PEER_NKEYS)
            i2 = _rows_bf16(i2_ref[pl.ds(t, 1), :], PEER_NKEYS)
            g = _rows_bf16(g_ref[pl.ds(t, 1), :], PEER_NKEYS)
            lhs = jnp.where(key_id == i1, g, zero)
            rhs = jnp.where(key_id == i2, one, zero)
            gt = lax.dot_general(lhs, rhs, _NT, preferred_element_type=F32)
            gate_ref[pl.ds(t, PEER_KEY_PAIRS, stride=PEER_PITCH), :] = pltpu.bitcast(gt.astype(BF16), U32)
            return carry

        lax.fori_loop(0, PEER_TM, token, 0, unroll=PEER_BUILD_UNROLL)

    a = lax.dot_general(h_ref[...], u_ref[...], _NT, preferred_element_type=F32)
    ws = []
    for r in range(slabs // 2):
        start = pl.multiple_of((c * (slabs // 2) + r) * PEER_PITCH, 8)
        word = gate_ref[pl.ds(start, PEER_TM), :]
        g_even = lax.bitcast_convert_type(lax.shift_left(word, jnp.uint32(16)), F32)
        g_odd = lax.bitcast_convert_type(word & jnp.uint32(0xFFFF0000), F32)
        for j, gj in ((2 * r, g_even), (2 * r + 1, g_odd)):
            aj = a[:, j * PEER_NKEYS:(j + 1) * PEER_NKEYS]
            ws.append((0.5 * aj * (1.0 + lax.erf(aj * SQRT_HALF)) * gj).astype(BF16))
    y = jnp.dot(jnp.concatenate(ws, axis=1), v_ref[...], preferred_element_type=F32)

    @pl.when(c == 0)
    def _first():
        o_ref[...] = x1_ref[...] + y

    @pl.when(c > 0)
    def _rest():
        o_ref[...] += y


def _peerffn(hn, x1, i1, i2, gate, u, v):
    t = hn.shape[0]
    tm, te = PEER_TM, PEER_TE
    tile = lambda w: pl.BlockSpec((tm, w), lambda i, c: (i, 0))
    chunk = pl.BlockSpec((te, D_MODEL), lambda i, c: (c, 0))
    return pl.pallas_call(
        _peerffn_kernel,
        grid=(t // tm, PEER_EXPERTS // te),
        in_specs=[tile(D_MODEL), tile(D_MODEL), tile(PEER_SLOTS), tile(PEER_SLOTS), tile(PEER_SLOTS),
                  chunk, chunk],
        out_specs=tile(D_MODEL),
        out_shape=jax.ShapeDtypeStruct((t, D_MODEL), F32),
        scratch_shapes=[pltpu.VMEM((PEER_KEY_PAIRS * PEER_PITCH, LANES), U32)],
        compiler_params=_cparams(("parallel", "arbitrary")),
        name="peerffn",
    )(hn, x1, i1, i2, gate, u, v)


def _neg_slope_rows(slopes, width):
    n = slopes.shape[0]
    return jnp.broadcast_to((-slopes).reshape(n // 2, 2, 1, 1), (n // 2, 2, 1, width))


def _tiled_gain(g, reps):
    return jnp.tile(g, reps)[None, :]


def _layer(x, mem, g_mix, w_in, qg_dil, kg_dil, qg_moba, kg_moba, qg_mem, kg_mem, g_memtok, w_mem_kv,
           og_dil, og_moba, og_mem, w_out, g_ffn, w_peer_q, sub1, sub2, peer_u, peer_v):
    b, s_len, d = x.shape
    t = b * s_len
    n_mix = N_HEADS_DIL + N_HEADS_MOBA
    slopes = jnp.exp2(-8.0 * jnp.arange(1, n_mix + 1, dtype=F32) / n_mix)
    ns_dil = _neg_slope_rows(slopes[0::2], 2 * BAND_BLOCK)
    ns_moba = _neg_slope_rows(slopes[1::2], MOBA_BLOCK)

    x2 = x.reshape(t, d)
    k_m, v_m = _memkv(mem, g_memtok[None, :], w_mem_kv.astype(BF16), _tiled_gain(kg_mem, N_HEADS_MEM))
    q_d, k_d, v_d, q_b, k_b, v_b, q_m = _inproj(
        x2, g_mix[None, :], w_in.astype(BF16),
        _tiled_gain(qg_dil, N_HEADS_DIL), _tiled_gain(kg_dil, N_HEADS_DIL),
        _tiled_gain(qg_moba, N_HEADS_MOBA), _tiled_gain(kg_moba, N_HEADS_MOBA),
        _tiled_gain(qg_mem, N_HEADS_MEM))
    seq = lambda a: a.reshape(b, s_len, a.shape[-1])
    o_dil = _dilated(seq(q_d), seq(k_d), seq(v_d), ns_dil)
    o_moba = _moba(seq(q_b), seq(k_b), seq(v_b), ns_moba)
    o_mem = _memattn(seq(q_m), k_m, v_m)
    x1, hn, s1, s2 = _outproj(
        o_dil.reshape(t, W_DIL), o_moba.reshape(t, W_MOBA), o_mem.reshape(t, W_MEM), x2,
        og_dil[None, :], og_moba[None, :], og_mem[None, :], w_out.astype(BF16), g_ffn[None, :],
        w_peer_q.astype(BF16), sub1.astype(BF16), sub2.astype(BF16))
    i1, i2, gate = _peertopk(s1, s2)
    slots = lambda a: a.reshape(PEER_SLOTS, t).T
    out = _peerffn(hn, x1, slots(i1), slots(i2), slots(gate), peer_u.astype(BF16), peer_v.astype(BF16))
    return out.reshape(b, s_len, d)


def kernel(x, mem, g_mix, w_in, qg_dil, kg_dil, qg_moba, kg_moba, qg_mem, kg_mem, g_memtok, w_mem_kv,
           og_dil, og_moba, og_mem, w_out, g_ffn, w_peer_q, peer_subkeys_1, peer_subkeys_2, peer_u,
           peer_v):
    h = x
    for layer in range(g_mix.shape[0]):
        h = _layer(h, mem, g_mix[layer], w_in[layer], qg_dil[layer], kg_dil[layer], qg_moba[layer],
                   kg_moba[layer], qg_mem[layer], kg_mem[layer], g_memtok[layer], w_mem_kv[layer],
                   og_dil[layer], og_moba[layer], og_mem[layer], w_out[layer], g_ffn[layer],
                   w_peer_q[layer], peer_subkeys_1[layer], peer_subkeys_2[layer], peer_u[layer],
                   peer_v[layer])
    return h
```

```python
import functools
import math

import jax
import jax.numpy as jnp
from jax import lax
from jax.experimental import pallas as pl
from jax.experimental.pallas import tpu as pltpu

F32 = jnp.float32
BF16 = jnp.bfloat16
I32 = jnp.int32
U32 = jnp.uint32

LANES = 128
D_MODEL = 1024
N_HEADS_DIL = 6
N_HEADS_MOBA = 6
N_HEADS_MEM = 4
HEAD_DIM = 64
W_DIL = N_HEADS_DIL * HEAD_DIM
W_MOBA = N_HEADS_MOBA * HEAD_DIM
W_MEM = N_HEADS_MEM * HEAD_DIM
IN_WIDTH = 3 * W_DIL + 3 * W_MOBA + W_MEM
DIL_CONFIGS = ((128, 1), (512, 4), (2048, 16))
BAND_BLOCK = 128
MOBA_BLOCK = 256
MOBA_TOPK = 3
MEM_LEN = 256
PEER_HEADS = 8
PEER_NKEYS = 128
PEER_EXPERTS = PEER_NKEYS * PEER_NKEYS
PEER_TOPK = 16
PEER_DKEY = 256
PEER_SLOTS = PEER_HEADS * PEER_TOPK
RMS_EPS = 1e-6
NEG_INF = -1e30
SCALE = 1.0 / math.sqrt(HEAD_DIM)
SQRT_HALF = math.sqrt(0.5)

VMEM_LIMIT = 56 * 1024 * 1024

_NT = (((1,), (1,)), ((), ()))


def _shr(a, bits):
    return lax.shift_right_logical(a, jnp.full(a.shape, bits, a.dtype))


def _shr_scalar(a, bits):
    return lax.shift_right_logical(a, jnp.int32(bits))


def _cparams(sem):
    return pltpu.CompilerParams(dimension_semantics=sem, vmem_limit_bytes=VMEM_LIMIT)


def _row_rms(a, gain):
    return a * lax.rsqrt(jnp.mean(a * a, axis=-1, keepdims=True) + RMS_EPS) * gain


def _group_mean_sq(p, group):
    w = p.shape[-1]
    bits = group.bit_length() - 1
    gi = _shr(lax.broadcasted_iota(I32, (w, w), 0), bits)
    gj = _shr(lax.broadcasted_iota(I32, (w, w), 1), bits)
    ones_bd = jnp.where(gi == gj, 1.0, 0.0).astype(BF16)
    p2 = p * p
    hi = p2.astype(BF16)
    lo = (p2 - hi.astype(F32)).astype(BF16)
    ss = (jnp.dot(hi, ones_bd, preferred_element_type=F32)
          + jnp.dot(lo, ones_bd, preferred_element_type=F32))
    return ss * (1.0 / group)


def _head_rms(p, gain):
    return p * lax.rsqrt(_group_mean_sq(p, HEAD_DIM) + RMS_EPS) * gain


def _memkv_kernel(mem_ref, g_ref, w_ref, kg_ref, k_ref, v_ref):
    hn = _row_rms(mem_ref[...], g_ref[...]).astype(BF16)
    kv = jnp.dot(hn, w_ref[...], preferred_element_type=F32)
    k_ref[...] = _head_rms(kv[:, :W_MEM], kg_ref[...])
    v_ref[...] = kv[:, W_MEM:]


def _memkv(mem, g_memtok, w_kv, kg_mem):
    b = mem.shape[0]
    return pl.pallas_call(
        _memkv_kernel,
        grid=(b,),
        in_specs=[
            pl.BlockSpec((None, MEM_LEN, D_MODEL), lambda i: (i, 0, 0)),
            pl.BlockSpec((1, D_MODEL), lambda i: (0, 0)),
            pl.BlockSpec((D_MODEL, 2 * W_MEM), lambda i: (0, 0)),
            pl.BlockSpec((1, W_MEM), lambda i: (0, 0)),
        ],
        out_specs=[pl.BlockSpec((None, MEM_LEN, W_MEM), lambda i: (i, 0, 0))] * 2,
        out_shape=[jax.ShapeDtypeStruct((b, MEM_LEN, W_MEM), F32)] * 2,
        compiler_params=_cparams(("parallel",)),
        name="memkv",
    )(mem, g_memtok, w_kv, kg_mem)


INPROJ_TM = 512


def _inproj_kernel(x_ref, g_ref, w_ref, gqd, gkd, gqb, gkb, gqm, qd, kd, vd, qb, kb, vb, qm):
    hb = _row_rms(x_ref[...], g_ref[...]).astype(BF16)

    def seg(lo, width):
        return jnp.dot(hb, w_ref[:, lo:lo + width], preferred_element_type=F32)

    qd[...] = _head_rms(seg(0, W_DIL), gqd[...])
    kd[...] = _head_rms(seg(W_DIL, W_DIL), gkd[...])
    vd[...] = seg(2 * W_DIL, W_DIL)
    base = 3 * W_DIL
    qb[...] = _head_rms(seg(base, W_MOBA), gqb[...])
    kb[...] = _head_rms(seg(base + W_MOBA, W_MOBA), gkb[...])
    vb[...] = seg(base + 2 * W_MOBA, W_MOBA)
    qm[...] = _head_rms(seg(base + 3 * W_MOBA, W_MEM), gqm[...])


def _inproj(x2, g_mix, w_in, gqd, gkd, gqb, gkb, gqm):
    t = x2.shape[0]
    tm = INPROJ_TM
    row = lambda w: pl.BlockSpec((1, w), lambda i: (0, 0))
    tile = lambda w: pl.BlockSpec((tm, w), lambda i: (i, 0))
    widths = (W_DIL, W_DIL, W_DIL, W_MOBA, W_MOBA, W_MOBA, W_MEM)
    return pl.pallas_call(
        _inproj_kernel,
        grid=(t // tm,),
        in_specs=[tile(D_MODEL), row(D_MODEL), pl.BlockSpec((D_MODEL, IN_WIDTH), lambda i: (0, 0)),
                  row(W_DIL), row(W_DIL), row(W_MOBA), row(W_MOBA), row(W_MEM)],
        out_specs=[tile(w) for w in widths],
        out_shape=[jax.ShapeDtypeStruct((t, w), F32) for w in widths],
        compiler_params=_cparams(("parallel",)),
        name="inproj",
    )(x2, g_mix, w_in, gqd, gkd, gqb, gkb, gqm)


DIL_PAD = BAND_BLOCK * max(d for _, d in DIL_CONFIGS)
DIL_MIX_ROWS = 256
DIL_UNROLL = 8


def _dil_kernel(q_ref, k_ref, v_ref, ns_ref, o_ref, kpad, vpad, ob0, ob1, ob2, lb0, lb1, lb2):
    s_len = q_ref.shape[0]
    zeros = jnp.zeros((DIL_PAD, LANES), F32)
    kpad[0:DIL_PAD, :] = zeros
    vpad[0:DIL_PAD, :] = zeros
    kpad[DIL_PAD:, :] = k_ref[...]
    vpad[DIL_PAD:, :] = v_ref[...]

    blk = BAND_BLOCK
    head0 = lax.broadcasted_iota(I32, (blk, LANES), 1) < HEAD_DIM
    ql = lax.broadcasted_iota(I32, (blk, 2 * blk), 0)
    kl = lax.broadcasted_iota(I32, (blk, 2 * blk), 1)
    delta = blk + ql - kl
    obs = (ob0, ob1, ob2)
    lbs = (lb0, lb1, lb2)

    for c, (window, dil) in enumerate(DIL_CONFIGS):
        reach = window // dil
        n_blk = s_len // dil // blk
        in_band = (delta >= 0) & (delta <= reach)
        dist = (delta * dil).astype(F32)
        ob, lb = obs[c], lbs[c]

        def rows(start, size, dil=dil):
            return pl.ds(start, size) if dil == 1 else pl.ds(start, size, stride=dil)

        def block_body(n, r, dil=dil, in_band=in_band, dist=dist, ob=ob, lb=lb, rows=rows):
            q_start = r + dil * blk * n
            k_start = DIL_PAD + r + dil * blk * (n - 1)
            qv = q_ref[rows(q_start, blk), :]
            kv = kpad[rows(k_start, 2 * blk), :].astype(BF16)
            vv = vpad[rows(k_start, 2 * blk), :].astype(BF16)
            valid = in_band & (kl >= jnp.where(n == 0, blk, 0))
            outs, lses = [], []
            for h in range(2):
                qh = jnp.where(head0 if h == 0 else ~head0, qv, 0.0).astype(BF16)
                s = lax.dot_general(qh, kv, _NT, preferred_element_type=F32) * SCALE
                s = jnp.where(valid, s + ns_ref[h] * dist, NEG_INF)
                m = jnp.max(s, axis=-1, keepdims=True)
                p = jnp.exp(s - m)
                den = jnp.sum(p, axis=-1, keepdims=True)
                lses.append(jnp.broadcast_to(m + jnp.log(den), (blk, LANES)))
                outs.append(jnp.dot(p.astype(BF16), vv, preferred_element_type=F32) / den)
            ob[rows(q_start, blk), :] = jnp.where(head0, outs[0], outs[1])
            lb[rows(q_start, blk), :] = jnp.where(head0, lses[0], lses[1])

        def flat_body(idx, carry, n_blk=n_blk, block_body=block_body):
            block_body(idx & (n_blk - 1), _shr_scalar(idx, n_blk.bit_length() - 1))
            return carry

        lax.fori_loop(0, dil * n_blk, flat_body, 0, unroll=DIL_UNROLL)

    def mix(i, carry):
        sl = pl.ds(pl.multiple_of(i * DIL_MIX_ROWS, DIL_MIX_ROWS), DIL_MIX_ROWS)
        l0, l1, l2 = lb0[sl, :], lb1[sl, :], lb2[sl, :]
        mx = jnp.maximum(jnp.maximum(l0, l1), l2)
        e0, e1, e2 = jnp.exp(l0 - mx), jnp.exp(l1 - mx), jnp.exp(l2 - mx)
        tot = e0 + e1 + e2
        o_ref[sl, :] = (e0 / tot) * ob0[sl, :] + (e1 / tot) * ob1[sl, :] + (e2 / tot) * ob2[sl, :]
        return carry

    lax.fori_loop(0, s_len // DIL_MIX_ROWS, mix, 0)


def _dilated(q, k, v, neg_slopes):
    b, s_len, w = q.shape
    n_pair = w // LANES
    slab = pl.BlockSpec((None, s_len, LANES), lambda i, p: (i, 0, p))
    buf = lambda rows: pltpu.VMEM((rows, LANES), F32)
    return pl.pallas_call(
        _dil_kernel,
        grid=(b, n_pair),
        in_specs=[slab, slab, slab,
                  pl.BlockSpec((None, 2, 1, 2 * BAND_BLOCK), lambda i, p: (p, 0, 0, 0))],
        out_specs=slab,
        out_shape=jax.ShapeDtypeStruct((b, s_len, w), F32),
        scratch_shapes=[buf(DIL_PAD + s_len), buf(DIL_PAD + s_len)] + [buf(s_len)] * 6,
        compiler_params=_cparams(("parallel", "parallel")),
        name="dilated",
    )(q, k, v, neg_slopes)


def _split_bf16(a):
    hi = a.astype(BF16)
    lo = (a - hi.astype(F32)).astype(BF16)
    return hi, lo


def _split3_bf16(a):
    t0 = a.astype(BF16).astype(F32)
    t1 = (a - t0).astype(BF16).astype(F32)
    t2 = (a - t0 - t1).astype(BF16).astype(F32)
    return t0, t1, t2


MOBA_QB = 16
MOBA_KB = 19
MOBA_GROUPS = (8, 4, 2, 1)


def _moba_kernel(q_ref, k_ref, v_ref, ns_ref, o_ref, qa_ref, kb_ref, kx_ref, vt_ref, m_ref, acc_ref):
    s_len = q_ref.shape[0]
    blk = MOBA_BLOCK
    n_blk = s_len // blk
    lane = lax.broadcasted_iota(I32, (blk, LANES), 1)
    head0 = lane < HEAD_DIM
    pos0 = lax.broadcasted_iota(I32, (blk, LANES), 0).astype(F32)

    kb_ref[...] = k_ref[...].astype(BF16)

    k_mean = jnp.concatenate(
        [jnp.sum(k_ref[j * blk:(j + 1) * blk, :], axis=0, keepdims=True) for j in range(n_blk)],
        axis=0) * (1.0 / blk)
    q_hi, q_lo = _split_bf16(q_ref[...])

    blk_id = lax.broadcasted_iota(I32, (n_blk, s_len), 0)
    n_past = _shr(lax.broadcasted_iota(I32, (n_blk, s_len), 1), blk.bit_length() - 1)
    past = blk_id < n_past
    blk_f = blk_id.astype(F32)
    pad_rows = LANES - n_blk

    for h in range(2):
        hm = (lax.broadcasted_iota(I32, (n_blk, LANES), 1) < HEAD_DIM) == (h == 0)
        km_hi, km_lo = _split_bf16(jnp.where(hm, k_mean, 0.0))
        gate = (lax.dot_general(km_hi, q_hi, _NT, preferred_element_type=F32)
                + lax.dot_general(km_hi, q_lo, _NT, preferred_element_type=F32)
                + lax.dot_general(km_lo, q_hi, _NT, preferred_element_type=F32))
        g = jnp.where(past, gate, NEG_INF)
        sel = jnp.zeros((n_blk, s_len), F32)
        for _ in range(MOBA_TOPK):
            m = jnp.max(g, axis=0, keepdims=True)
            first = jnp.min(jnp.where(g == m, blk_f, float(n_blk)), axis=0, keepdims=True)
            pick = blk_f == first
            sel = jnp.where(pick, 1.0, sel)
            g = jnp.where(pick, -jnp.inf, g)
        not_sel = jnp.where(past, 1.0 - sel, jnp.where(blk_id == n_past, 0.0, 1.0))
        not_sel = jnp.concatenate([not_sel, jnp.zeros((pad_rows, s_len), F32)], axis=0)
        ns = ns_ref[h][:, 0:LANES]
        for i in range(n_blk):
            rows = slice(i * blk, (i + 1) * blk)
            qh = jnp.where(head0 if h == 0 else ~head0, q_ref[rows, :], 0.0) * SCALE
            qa_ref[h, rows, 0:LANES] = qh.astype(BF16)
            t_pos = pos0 + float(i * blk)
            q0, q1, q2 = _split3_bf16(ns * t_pos)
            k0, k1, k2 = _split3_bf16(-ns * t_pos)
            qx = not_sel[:, rows].T
            kx = jnp.where(lane == i, NEG_INF, 0.0)
            for d, (qt, kt) in enumerate(((q0, k0), (q1, k1), (q2, k2))):
                qx = jnp.where(lane == MOBA_QB + d, qt, jnp.where(lane == MOBA_KB + d, 1.0, qx))
                kx = jnp.where(lane == MOBA_QB + d, 1.0, jnp.where(lane == MOBA_KB + d, kt, kx))
            qa_ref[h, rows, LANES:] = qx.astype(BF16)
            kx_ref[h, rows, :] = kx.astype(BF16)

    top_rows = lax.broadcasted_iota(I32, (LANES, blk), 0) < HEAD_DIM
    for j in range(n_blk):
        vt = v_ref[j * blk:(j + 1) * blk, :].T
        vt_ref[0, :, j * blk:(j + 1) * blk] = jnp.where(top_rows, vt, 1.0).astype(BF16)
        vt_ref[1, :, j * blk:(j + 1) * blk] = jnp.where(top_rows, 1.0, vt).astype(BF16)

    key_i = lax.broadcasted_iota(I32, (blk, blk), 0)
    qry_i = lax.broadcasted_iota(I32, (blk, blk), 1)
    causal = key_i <= qry_i

    def q_aug(i, h):
        r0 = pl.multiple_of(i * blk, blk)
        return jnp.concatenate([qa_ref[h, pl.ds(r0, blk), 0:LANES], qa_ref[h, pl.ds(r0, blk), LANES:]],
                               axis=1)

    def k_aug(j, h):
        c0 = pl.multiple_of(j * blk, blk)
        return jnp.concatenate([kb_ref[pl.ds(c0, blk), :], kx_ref[h, pl.ds(c0, blk), :]], axis=1)

    def v_t(j, h):
        return vt_ref[h, :, pl.ds(pl.multiple_of(j * blk, blk), blk)]

    def own_block(i, carry):
        for h in range(2):
            s = lax.dot_general(k_aug(i, h), q_aug(i, h), _NT, preferred_element_type=F32)
            s = jnp.where(causal, s, NEG_INF)
            m0 = jnp.max(s, axis=0, keepdims=True)
            m_ref[i, h] = m0
            acc_ref[i, h] = jnp.dot(v_t(i, h), jnp.exp(s - m0).astype(BF16), preferred_element_type=F32)
        return carry

    lax.fori_loop(0, n_blk, own_block, 0, unroll=2)

    def key_block(j, carry):
        ks = [k_aug(j, h) for h in range(2)]
        vs = [v_t(j, h) for h in range(2)]

        def update(i):
            for h in range(2):
                sj = lax.dot_general(ks[h], q_aug(i, h), _NT, preferred_element_type=F32)
                m = m_ref[i, h]
                m_new = jnp.maximum(m, jnp.max(sj, axis=0, keepdims=True))
                acc_ref[i, h] = jnp.exp(m - m_new) * acc_ref[i, h] + jnp.dot(
                    vs[h], jnp.exp(sj - m_new).astype(BF16), preferred_element_type=F32)
                m_ref[i, h] = m_new

        def tiles(first, count):
            def body(g, c2):
                for d in range(count):
                    update(first + count * g + d)
                return c2
            return body

        first = j + 1
        for size in MOBA_GROUPS:
            count = (n_blk - first) // size
            lax.fori_loop(0, count, tiles(first, size), 0)
            first = first + size * count
        return carry

    lax.fori_loop(0, n_blk - 1, key_block, 0)

    def finish(i, carry):
        a0, a1 = acc_ref[i, 0], acc_ref[i, 1]
        out_t = jnp.where(top_rows, a0 / a0[HEAD_DIM:HEAD_DIM + 1, :], a1 / a1[0:1, :])
        o_ref[pl.ds(pl.multiple_of(i * blk, blk), blk), :] = out_t.T
        return carry

    lax.fori_loop(0, n_blk, finish, 0, unroll=2)


def _moba(q, k, v, neg_slopes):
    b, s_len, w = q.shape
    n_pair = w // LANES
    slab = pl.BlockSpec((None, s_len, LANES), lambda i, p: (i, 0, p))
    return pl.pallas_call(
        _moba_kernel,
        grid=(b, n_pair),
        in_specs=[slab, slab, slab,
                  pl.BlockSpec((None, 2, 1, MOBA_BLOCK), lambda i, p: (p, 0, 0, 0))],
        out_specs=slab,
        out_shape=jax.ShapeDtypeStruct((b, s_len, w), F32),
        scratch_shapes=[pltpu.VMEM((2, s_len, 2 * LANES), BF16),
                        pltpu.VMEM((s_len, LANES), BF16),
                        pltpu.VMEM((2, s_len, LANES), BF16),
                        pltpu.VMEM((2, LANES, s_len), BF16),
                        pltpu.VMEM((s_len // MOBA_BLOCK, 2, 1, MOBA_BLOCK), F32),
                        pltpu.VMEM((s_len // MOBA_BLOCK, 2, LANES, MOBA_BLOCK), F32)],
        compiler_params=_cparams(("parallel", "parallel")),
        name="moba",
    )(q, k, v, neg_slopes)


MEMATTN_TQ = 512


def _memattn_kernel(q_ref, k_ref, v_ref, o_ref):
    q = q_ref[...]
    kb = k_ref[...].astype(BF16)
    vb = v_ref[...].astype(BF16)
    head = _shr(lax.broadcasted_iota(I32, q.shape, 1), HEAD_DIM.bit_length() - 1)
    out = jnp.zeros(q.shape, F32)
    for h in range(N_HEADS_MEM):
        qh = jnp.where(head == h, q, 0.0).astype(BF16)
        s = lax.dot_general(qh, kb, _NT, preferred_element_type=F32) * SCALE
        m = jnp.max(s, axis=-1, keepdims=True)
        p = jnp.exp(s - m)
        den = jnp.sum(p, axis=-1, keepdims=True)
        oh = jnp.dot(p.astype(BF16), vb, preferred_element_type=F32) / den
        out = jnp.where(head == h, oh, out)
    o_ref[...] = out


def _memattn(q, k, v):
    b, s_len, w = q.shape
    tq = MEMATTN_TQ
    return pl.pallas_call(
        _memattn_kernel,
        grid=(b, s_len // tq),
        in_specs=[pl.BlockSpec((None, tq, w), lambda i, j: (i, j, 0)),
                  pl.BlockSpec((None, MEM_LEN, w), lambda i, j: (i, 0, 0)),
                  pl.BlockSpec((None, MEM_LEN, w), lambda i, j: (i, 0, 0))],
        out_specs=pl.BlockSpec((None, tq, w), lambda i, j: (i, j, 0)),
        out_shape=jax.ShapeDtypeStruct((b, s_len, w), F32),
        compiler_params=_cparams(("parallel", "parallel")),
        name="memattn",
    )(q, k, v)


OUTPROJ_TM = 512


def _outproj_kernel(od_ref, ob_ref, om_ref, x_ref, gd, gb, gm, wo_ref, gf, wq_ref, k1_ref, k2_ref,
                    x1_ref, hn_ref, s1_ref, s2_ref):
    y = jnp.dot(_row_rms(od_ref[...], gd[...]).astype(BF16), wo_ref[0:W_DIL, :],
                preferred_element_type=F32)
    y += jnp.dot(_row_rms(ob_ref[...], gb[...]).astype(BF16), wo_ref[W_DIL:W_DIL + W_MOBA, :],
                 preferred_element_type=F32)
    y += jnp.dot(_row_rms(om_ref[...], gm[...]).astype(BF16), wo_ref[W_DIL + W_MOBA:, :],
                 preferred_element_type=F32)
    x1 = x_ref[...] + y
    x1_ref[...] = x1
    hb = _row_rms(x1, gf[...]).astype(BF16)
    hn_ref[...] = hb
    half = PEER_DKEY // 2
    for h in range(PEER_HEADS):
        qh = jnp.dot(hb, wq_ref[:, h * PEER_DKEY:(h + 1) * PEER_DKEY],
                     preferred_element_type=F32).astype(BF16)
        s1_ref[h] = lax.dot_general(k1_ref[h], qh[:, :half], _NT, preferred_element_type=F32)
        s2_ref[h] = lax.dot_general(k2_ref[h], qh[:, half:], _NT, preferred_element_type=F32)


def _outproj(o_dil, o_moba, o_mem, x2, og_dil, og_moba, og_mem, w_out, g_ffn, w_q, sub1, sub2):
    t = x2.shape[0]
    tm = OUTPROJ_TM
    row = lambda w: pl.BlockSpec((1, w), lambda i: (0, 0))
    tile = lambda w: pl.BlockSpec((tm, w), lambda i: (i, 0))
    full = lambda shape: pl.BlockSpec(shape, lambda i: (0,) * len(shape))
    score = pl.BlockSpec((PEER_HEADS, PEER_NKEYS, tm), lambda i: (0, 0, i))
    half = PEER_DKEY // 2
    return pl.pallas_call(
        _outproj_kernel,
        grid=(t // tm,),
        in_specs=[tile(W_DIL), tile(W_MOBA), tile(W_MEM), tile(D_MODEL),
                  row(W_DIL), row(W_MOBA), row(W_MEM), full((D_MODEL, D_MODEL)), row(D_MODEL),
                  full((D_MODEL, PEER_HEADS * PEER_DKEY)),
                  full((PEER_HEADS, PEER_NKEYS, half)), full((PEER_HEADS, PEER_NKEYS, half))],
        out_specs=[tile(D_MODEL), tile(D_MODEL), score, score],
        out_shape=[jax.ShapeDtypeStruct((t, D_MODEL), F32),
                   jax.ShapeDtypeStruct((t, D_MODEL), BF16),
                   jax.ShapeDtypeStruct((PEER_HEADS, PEER_NKEYS, t), F32),
                   jax.ShapeDtypeStruct((PEER_HEADS, PEER_NKEYS, t), F32)],
        compiler_params=_cparams(("parallel",)),
        name="outproj",
    )(o_dil, o_moba, o_mem, x2, og_dil, og_moba, og_mem, w_out, g_ffn, w_q, sub1, sub2)


TOPK_TM = 512
TOPK_COLS_PER_ITER = 8
_ID_LIMIT = 1024.0


def _top_rows(s, k, ids=None, rows_at=None):
    n_rows = s.shape[0]
    if ids is None:
        ids = lax.broadcasted_iota(I32, s.shape, 0).astype(F32)
    vals, idxs = [], []
    for t in range(k):
        r = n_rows if rows_at is None else rows_at[t]
        head, head_ids = s[:r, :], ids[:r, :]
        m = jnp.max(head, axis=0, keepdims=True)
        first = jnp.min(jnp.where(head == m, head_ids, _ID_LIMIT), axis=0, keepdims=True)
        vals.append(m)
        idxs.append(first)
        head = jnp.where(head_ids == first, -jnp.inf, head)
        s = head if r == n_rows else jnp.concatenate([head, s[r:, :]], axis=0)
    return jnp.concatenate(vals, axis=0), jnp.concatenate(idxs, axis=0)


def _pick_rows(table, sel):
    out = jnp.zeros(sel.shape, table.dtype)
    for j in range(table.shape[0]):
        out = jnp.where(sel == j, table[j:j + 1, :], out)
    return out


def _candidates(v1, v2):
    n = v1.shape[1]
    sub = lax.broadcasted_iota(I32, (8, n), 0)
    sub_f = sub.astype(F32)
    sums, ids = [], []
    for j1 in range(8):
        both = v1[j1:j1 + 1, :] + v2[0:8, :]
        sums.append(both if j1 == 0 else jnp.where(sub < PEER_TOPK // (j1 + 1), both, -jnp.inf))
        ids.append(sub_f + float(j1 * PEER_TOPK))
    sums += [v1[0:1, :] + v2[8:16, :], v1[8:16, :] + v2[0:1, :]]
    ids += [sub_f + 8.0, sub_f * float(PEER_TOPK) + float(8 * PEER_TOPK)]
    rows_at = [8 * min(t, 8) if t <= 8 else 80 for t in range(1, PEER_TOPK + 1)]
    return jnp.concatenate(sums, axis=0), jnp.concatenate(ids, axis=0), rows_at


def _retrieve(s1, s2):
    v1, i1 = _top_rows(s1, PEER_TOPK)
    v2, i2 = _top_rows(s2, PEER_TOPK)
    sums, ids, rows_at = _candidates(v1, v2)
    top_s, pair = _top_rows(sums, PEER_TOPK, ids, rows_at)
    e = jnp.exp(top_s - top_s[0:1, :])
    pair = pair.astype(I32)
    bits = PEER_TOPK.bit_length() - 1
    return (_pick_rows(i1, _shr(pair, bits)), _pick_rows(i2, pair & (PEER_TOPK - 1)),
            e / jnp.sum(e, axis=0, keepdims=True))


def _peertopk_kernel(s1_ref, s2_ref, i1_ref, i2_ref, g_ref):
    n_col = s1_ref.shape[2] // LANES

    def body(i, carry):
        for c in range(TOPK_COLS_PER_ITER):
            item = i * TOPK_COLS_PER_ITER + c
            h = _shr_scalar(item, n_col.bit_length() - 1)
            cols = pl.ds(pl.multiple_of((item & (n_col - 1)) * LANES, LANES), LANES)
            i1_ref[h, :, cols], i2_ref[h, :, cols], g_ref[h, :, cols] = _retrieve(
                s1_ref[h, :, cols], s2_ref[h, :, cols])
        return carry

    lax.fori_loop(0, PEER_HEADS * n_col // TOPK_COLS_PER_ITER, body, 0)


def _peertopk(s1, s2):
    t = s1.shape[2]
    tm = TOPK_TM
    score = pl.BlockSpec((PEER_HEADS, PEER_NKEYS, tm), lambda i: (0, 0, i))
    slot = pl.BlockSpec((PEER_HEADS, PEER_TOPK, tm), lambda i: (0, 0, i))
    return pl.pallas_call(
        _peertopk_kernel,
        grid=(t // tm,),
        in_specs=[score, score],
        out_specs=[slot, slot, slot],
        out_shape=[jax.ShapeDtypeStruct((PEER_HEADS, PEER_TOPK, t), F32)] * 3,
        compiler_params=_cparams(("parallel",)),
        name="peertopk",
    )(s1, s2)


PEER_TM = 512
PEER_TE = 2048
PEER_KEY_PAIRS = PEER_NKEYS // 2
PEER_PITCH = PEER_TM + 8
PEER_BUILD_UNROLL = 32
BF16_ROWS = 16


def _rows_bf16(row, n_rows):
    tile = jnp.broadcast_to(row, (BF16_ROWS, row.shape[1])).astype(BF16)
    return jnp.concatenate([tile] * (n_rows // BF16_ROWS), axis=0)


def _peerffn_kernel(h_ref, x1_ref, i1_ref, i2_ref, g_ref, u_ref, v_ref, o_ref, gate_ref):
    c = pl.program_id(1)
    slabs = PEER_TE // PEER_NKEYS

    @pl.when(c == 0)
    def _build_gate_matrix():
        key_id = lax.broadcasted_iota(I32, (PEER_NKEYS, PEER_SLOTS), 0).astype(BF16)
        one = jnp.ones((PEER_NKEYS, PEER_SLOTS), BF16)
        zero = jnp.zeros((PEER_NKEYS, PEER_SLOTS), BF16)

        def token(t, carry):
            i1 = _rows_bf16(i1_ref[pl.ds(t, 1), :], PEER_NKEYS)
            i2 = _rows_bf16(i2_ref[pl.ds(t, 1), :], PEER_NKEYS)
            g = _rows_bf16(g_ref[pl.ds(t, 1), :], PEER_NKEYS)
            lhs = jnp.where(key_id == i1, g, zero)
            rhs = jnp.where(key_id == i2, one, zero)
            gt = lax.dot_general(lhs, rhs, _NT, preferred_element_type=F32)
            gate_ref[pl.ds(t, PEER_KEY_PAIRS, stride=PEER_PITCH), :] = pltpu.bitcast(gt.astype(BF16), U32)
            return carry

        lax.fori_loop(0, PEER_TM, token, 0, unroll=PEER_BUILD_UNROLL)

    a = lax.dot_general(h_ref[...], u_ref[...], _NT, preferred_element_type=F32)
    ws = []
    for r in range(slabs // 2):
        start = pl.multiple_of((c * (slabs // 2) + r) * PEER_PITCH, 8)
        word = gate_ref[pl.ds(start, PEER_TM), :]
        g_even = lax.bitcast_convert_type(lax.shift_left(word, jnp.uint32(16)), F32)
        g_odd = lax.bitcast_convert_type(word & jnp.uint32(0xFFFF0000), F32)
        for j, gj in ((2 * r, g_even), (2 * r + 1, g_odd)):
            aj = a[:, j * PEER_NKEYS:(j + 1) * PEER_NKEYS]
            ws.append((0.5 * aj * (1.0 + lax.erf(aj * SQRT_HALF)) * gj).astype(BF16))
    y = jnp.dot(jnp.concatenate(ws, axis=1), v_ref[...], preferred_element_type=F32)

    @pl.when(c == 0)
    def _first():
        o_ref[...] = x1_ref[...] + y

    @pl.when(c > 0)
    def _rest():
        o_ref[...] += y


def _peerffn(hn, x1, i1, i2, gate, u, v):
    t = hn.shape[0]
    tm, te = PEER_TM, PEER_TE
    tile = lambda w: pl.BlockSpec((tm, w), lambda i, c: (i, 0))
    chunk = pl.BlockSpec((te, D_MODEL), lambda i, c: (c, 0))
    return pl.pallas_call(
        _peerffn_kernel,
        grid=(t // tm, PEER_EXPERTS // te),
        in_specs=[tile(D_MODEL), tile(D_MODEL), tile(PEER_SLOTS), tile(PEER_SLOTS), tile(PEER_SLOTS),
                  chunk, chunk],
        out_specs=tile(D_MODEL),
        out_shape=jax.ShapeDtypeStruct((t, D_MODEL), F32),
        scratch_shapes=[pltpu.VMEM((PEER_KEY_PAIRS * PEER_PITCH, LANES), U32)],
        compiler_params=_cparams(("parallel", "arbitrary")),
        name="peerffn",
    )(hn, x1, i1, i2, gate, u, v)


def _neg_slope_rows(slopes, width):
    n = slopes.shape[0]
    return jnp.broadcast_to((-slopes).reshape(n // 2, 2, 1, 1), (n // 2, 2, 1, width))


def _tiled_gain(g, reps):
    return jnp.tile(g, reps)[None, :]


def _layer(x, mem, g_mix, w_in, qg_dil, kg_dil, qg_moba, kg_moba, qg_mem, kg_mem, g_memtok, w_mem_kv,
           og_dil, og_moba, og_mem, w_out, g_ffn, w_peer_q, sub1, sub2, peer_u, peer_v):
    b, s_len, d = x.shape
    t = b * s_len
    n_mix = N_HEADS_DIL + N_HEADS_MOBA
    slopes = jnp.exp2(-8.0 * jnp.arange(1, n_mix + 1, dtype=F32) / n_mix)
    ns_dil = _neg_slope_rows(slopes[0::2], 2 * BAND_BLOCK)
    ns_moba = _neg_slope_rows(slopes[1::2], MOBA_BLOCK)

    x2 = x.reshape(t, d)
    k_m, v_m = _memkv(mem, g_memtok[None, :], w_mem_kv.astype(BF16), _tiled_gain(kg_mem, N_HEADS_MEM))
    q_d, k_d, v_d, q_b, k_b, v_b, q_m = _inproj(
        x2, g_mix[None, :], w_in.astype(BF16),
        _tiled_gain(qg_dil, N_HEADS_DIL), _tiled_gain(kg_dil, N_HEADS_DIL),
        _tiled_gain(qg_moba, N_HEADS_MOBA), _tiled_gain(kg_moba, N_HEADS_MOBA),
        _tiled_gain(qg_mem, N_HEADS_MEM))
    seq = lambda a: a.reshape(b, s_len, a.shape[-1])
    o_dil = _dilated(seq(q_d), seq(k_d), seq(v_d), ns_dil)
    o_moba = _moba(seq(q_b), seq(k_b), seq(v_b), ns_moba)
    o_mem = _memattn(seq(q_m), k_m, v_m)
    x1, hn, s1, s2 = _outproj(
        o_dil.reshape(t, W_DIL), o_moba.reshape(t, W_MOBA), o_mem.reshape(t, W_MEM), x2,
        og_dil[None, :], og_moba[None, :], og_mem[None, :], w_out.astype(BF16), g_ffn[None, :],
        w_peer_q.astype(BF16), sub1.astype(BF16), sub2.astype(BF16))
    i1, i2, gate = _peertopk(s1, s2)
    slots = lambda a: a.reshape(PEER_SLOTS, t).T
    out = _peerffn(hn, x1, slots(i1), slots(i2), slots(gate), peer_u.astype(BF16), peer_v.astype(BF16))
    return out.reshape(b, s_len, d)


def kernel(x, mem, g_mix, w_in, qg_dil, kg_dil, qg_moba, kg_moba, qg_mem, kg_mem, g_memtok, w_mem_kv,
           og_dil, og_moba, og_mem, w_out, g_ffn, w_peer_q, peer_subkeys_1, peer_subkeys_2, peer_u,
           peer_v):
    h = x
    for layer in range(g_mix.shape[0]):
        h = _layer(h, mem, g_mix[layer], w_in[layer], qg_dil[layer], kg_dil[layer], qg_moba[layer],
                   kg_moba[layer], qg_mem[layer], kg_mem[layer], g_memtok[layer], w_mem_kv[layer],
                   og_dil[layer], og_moba[layer], og_mem[layer], w_out[layer], g_ffn[layer],
                   w_peer_q[layer], peer_subkeys_1[layer], peer_subkeys_2[layer], peer_u[layer],
                   peer_v[layer])
    return h
```

```python
import functools
import math

import jax
import jax.numpy as jnp
from jax import lax
from jax.experimental import pallas as pl
from jax.experimental.pallas import tpu as pltpu

F32 = jnp.float32
BF16 = jnp.bfloat16
I32 = jnp.int32
U32 = jnp.uint32

LANES = 128
D_MODEL = 1024
N_HEADS_DIL = 6
N_HEADS_MOBA = 6
N_HEADS_MEM = 4
HEAD_DIM = 64
W_DIL = N_HEADS_DIL * HEAD_DIM
W_MOBA = N_HEADS_MOBA * HEAD_DIM
W_MEM = N_HEADS_MEM * HEAD_DIM
IN_WIDTH = 3 * W_DIL + 3 * W_MOBA + W_MEM
DIL_CONFIGS = ((128, 1), (512, 4), (2048, 16))
BAND_BLOCK = 128
MOBA_BLOCK = 256
MOBA_TOPK = 3
MEM_LEN = 256
PEER_HEADS = 8
PEER_NKEYS = 128
PEER_EXPERTS = PEER_NKEYS * PEER_NKEYS
PEER_TOPK = 16
PEER_DKEY = 256
PEER_SLOTS = PEER_HEADS * PEER_TOPK
RMS_EPS = 1e-6
NEG_INF = -1e30
SCALE = 1.0 / math.sqrt(HEAD_DIM)
SQRT_HALF = math.sqrt(0.5)

VMEM_LIMIT = 56 * 1024 * 1024

_NT = (((1,), (1,)), ((), ()))


def _shr(a, bits):
    return lax.shift_right_logical(a, jnp.full(a.shape, bits, a.dtype))


def _shr_scalar(a, bits):
    return lax.shift_right_logical(a, jnp.int32(bits))


def _cparams(sem):
    return pltpu.CompilerParams(dimension_semantics=sem, vmem_limit_bytes=VMEM_LIMIT)


def _row_rms(a, gain):
    return a * lax.rsqrt(jnp.mean(a * a, axis=-1, keepdims=True) + RMS_EPS) * gain


def _group_mean_sq(p, group):
    w = p.shape[-1]
    bits = group.bit_length() - 1
    gi = _shr(lax.broadcasted_iota(I32, (w, w), 0), bits)
    gj = _shr(lax.broadcasted_iota(I32, (w, w), 1), bits)
    ones_bd = jnp.where(gi == gj, 1.0, 0.0).astype(BF16)
    p2 = p * p
    hi = p2.astype(BF16)
    lo = (p2 - hi.astype(F32)).astype(BF16)
    ss = (jnp.dot(hi, ones_bd, preferred_element_type=F32)
          + jnp.dot(lo, ones_bd, preferred_element_type=F32))
    return ss * (1.0 / group)


def _head_rms(p, gain):
    return p * lax.rsqrt(_group_mean_sq(p, HEAD_DIM) + RMS_EPS) * gain


def _memkv_kernel(mem_ref, g_ref, w_ref, kg_ref, k_ref, v_ref):
    hn = _row_rms(mem_ref[...], g_ref[...]).astype(BF16)
    kv = jnp.dot(hn, w_ref[...], preferred_element_type=F32)
    k_ref[...] = _head_rms(kv[:, :W_MEM], kg_ref[...])
    v_ref[...] = kv[:, W_MEM:]


def _memkv(mem, g_memtok, w_kv, kg_mem):
    b = mem.shape[0]
    return pl.pallas_call(
        _memkv_kernel,
        grid=(b,),
        in_specs=[
            pl.BlockSpec((None, MEM_LEN, D_MODEL), lambda i: (i, 0, 0)),
            pl.BlockSpec((1, D_MODEL), lambda i: (0, 0)),
            pl.BlockSpec((D_MODEL, 2 * W_MEM), lambda i: (0, 0)),
            pl.BlockSpec((1, W_MEM), lambda i: (0, 0)),
        ],
        out_specs=[pl.BlockSpec((None, MEM_LEN, W_MEM), lambda i: (i, 0, 0))] * 2,
        out_shape=[jax.ShapeDtypeStruct((b, MEM_LEN, W_MEM), F32)] * 2,
        compiler_params=_cparams(("parallel",)),
        name="memkv",
    )(mem, g_memtok, w_kv, kg_mem)


INPROJ_TM = 512


def _inproj_kernel(x_ref, g_ref, w_ref, gqd, gkd, gqb, gkb, gqm, qd, kd, vd, qb, kb, vb, qm):
    hb = _row_rms(x_ref[...], g_ref[...]).astype(BF16)

    def seg(lo, width):
        return jnp.dot(hb, w_ref[:, lo:lo + width], preferred_element_type=F32)

    qd[...] = _head_rms(seg(0, W_DIL), gqd[...])
    kd[...] = _head_rms(seg(W_DIL, W_DIL), gkd[...])
    vd[...] = seg(2 * W_DIL, W_DIL)
    base = 3 * W_DIL
    qb[...] = _head_rms(seg(base, W_MOBA), gqb[...])
    kb[...] = _head_rms(seg(base + W_MOBA, W_MOBA), gkb[...])
    vb[...] = seg(base + 2 * W_MOBA, W_MOBA)
    qm[...] = _head_rms(seg(base + 3 * W_MOBA, W_MEM), gqm[...])


def _inproj(x2, g_mix, w_in, gqd, gkd, gqb, gkb, gqm):
    t = x2.shape[0]
    tm = INPROJ_TM
    row = lambda w: pl.BlockSpec((1, w), lambda i: (0, 0))
    tile = lambda w: pl.BlockSpec((tm, w), lambda i: (i, 0))
    widths = (W_DIL, W_DIL, W_DIL, W_MOBA, W_MOBA, W_MOBA, W_MEM)
    return pl.pallas_call(
        _inproj_kernel,
        grid=(t // tm,),
        in_specs=[tile(D_MODEL), row(D_MODEL), pl.BlockSpec((D_MODEL, IN_WIDTH), lambda i: (0, 0)),
                  row(W_DIL), row(W_DIL), row(W_MOBA), row(W_MOBA), row(W_MEM)],
        out_specs=[tile(w) for w in widths],
        out_shape=[jax.ShapeDtypeStruct((t, w), F32) for w in widths],
        compiler_params=_cparams(("parallel",)),
        name="inproj",
    )(x2, g_mix, w_in, gqd, gkd, gqb, gkb, gqm)


DIL_PAD = BAND_BLOCK * max(d for _, d in DIL_CONFIGS)
DIL_MIX_ROWS = 256
DIL_UNROLL = 8


def _dil_kernel(q_ref, k_ref, v_ref, ns_ref, o_ref, kpad, vpad, ob0, ob1, ob2, lb0, lb1, lb2):
    s_len = q_ref.shape[0]
    zeros = jnp.zeros((DIL_PAD, LANES), F32)
    kpad[0:DIL_PAD, :] = zeros
    vpad[0:DIL_PAD, :] = zeros
    kpad[DIL_PAD:, :] = k_ref[...]
    vpad[DIL_PAD:, :] = v_ref[...]

    blk = BAND_BLOCK
    head0 = lax.broadcasted_iota(I32, (blk, LANES), 1) < HEAD_DIM
    ql = lax.broadcasted_iota(I32, (blk, 2 * blk), 0)
    kl = lax.broadcasted_iota(I32, (blk, 2 * blk), 1)
    delta = blk + ql - kl
    obs = (ob0, ob1, ob2)
    lbs = (lb0, lb1, lb2)

    for c, (window, dil) in enumerate(DIL_CONFIGS):
        reach = window // dil
        n_blk = s_len // dil // blk
        in_band = (delta >= 0) & (delta <= reach)
        dist = (delta * dil).astype(F32)
        ob, lb = obs[c], lbs[c]

        def rows(start, size, dil=dil):
            return pl.ds(start, size) if dil == 1 else pl.ds(start, size, stride=dil)

        def block_body(n, r, dil=dil, in_band=in_band, dist=dist, ob=ob, lb=lb, rows=rows):
            q_start = r + dil * blk * n
            k_start = DIL_PAD + r + dil * blk * (n - 1)
            qv = q_ref[rows(q_start, blk), :]
            kv = kpad[rows(k_start, 2 * blk), :].astype(BF16)
            vv = vpad[rows(k_start, 2 * blk), :].astype(BF16)
            valid = in_band & (kl >= jnp.where(n == 0, blk, 0))
            outs, lses = [], []
            for h in range(2):
                qh = jnp.where(head0 if h == 0 else ~head0, qv, 0.0).astype(BF16)
                s = lax.dot_general(qh, kv, _NT, preferred_element_type=F32) * SCALE
                s = jnp.where(valid, s + ns_ref[h] * dist, NEG_INF)
                m = jnp.max(s, axis=-1, keepdims=True)
                p = jnp.exp(s - m)
                den = jnp.sum(p, axis=-1, keepdims=True)
                lses.append(jnp.broadcast_to(m + jnp.log(den), (blk, LANES)))
                outs.append(jnp.dot(p.astype(BF16), vv, preferred_element_type=F32) / den)
            ob[rows(q_start, blk), :] = jnp.where(head0, outs[0], outs[1])
            lb[rows(q_start, blk), :] = jnp.where(head0, lses[0], lses[1])

        def flat_body(idx, carry, n_blk=n_blk, block_body=block_body):
            block_body(idx & (n_blk - 1), _shr_scalar(idx, n_blk.bit_length() - 1))
            return carry

        lax.fori_loop(0, dil * n_blk, flat_body, 0, unroll=DIL_UNROLL)

    def mix(i, carry):
        sl = pl.ds(pl.multiple_of(i * DIL_MIX_ROWS, DIL_MIX_ROWS), DIL_MIX_ROWS)
        l0, l1, l2 = lb0[sl, :], lb1[sl, :], lb2[sl, :]
        mx = jnp.maximum(jnp.maximum(l0, l1), l2)
        e0, e1, e2 = jnp.exp(l0 - mx), jnp.exp(l1 - mx), jnp.exp(l2 - mx)
        tot = e0 + e1 + e2
        o_ref[sl, :] = (e0 / tot) * ob0[sl, :] + (e1 / tot) * ob1[sl, :] + (e2 / tot) * ob2[sl, :]
        return carry

    lax.fori_loop(0, s_len // DIL_MIX_ROWS, mix, 0)


def _dilated(q, k, v, neg_slopes):
    b, s_len, w = q.shape
    n_pair = w // LANES
    slab = pl.BlockSpec((None, s_len, LANES), lambda i, p: (i, 0, p))
    buf = lambda rows: pltpu.VMEM((rows, LANES), F32)
    return pl.pallas_call(
        _dil_kernel,
        grid=(b, n_pair),
        in_specs=[slab, slab, slab,
                  pl.BlockSpec((None, 2, 1, 2 * BAND_BLOCK), lambda i, p: (p, 0, 0, 0))],
        out_specs=slab,
        out_shape=jax.ShapeDtypeStruct((b, s_len, w), F32),
        scratch_shapes=[buf(DIL_PAD + s_len), buf(DIL_PAD + s_len)] + [buf(s_len)] * 6,
        compiler_params=_cparams(("parallel", "parallel")),
        name="dilated",
    )(q, k, v, neg_slopes)


def _split_bf16(a):
    hi = a.astype(BF16)
    lo = (a - hi.astype(F32)).astype(BF16)
    return hi, lo


MOBA_QB = 16
MOBA_KB = 19
MOBA_GROUPS = (8, 4, 2, 1)


def _split3_masked(a):
    def keep_high_bits(x):
        bits = lax.bitcast_convert_type(x, U32) & jnp.uint32(0xFFFF0000)
        return lax.bitcast_convert_type(bits, F32)

    t0 = keep_high_bits(a)
    t1 = keep_high_bits(a - t0)
    t2 = keep_high_bits(a - t0 - t1)
    return t0, t1, t2


def _moba_bias_lanes(slopes, s_len):
    t_pos = jnp.arange(s_len, dtype=F32)
    lane = jnp.arange(LANES)
    neg = (-slopes).reshape(-1, 2, 1) * t_pos
    q_terms = _split3_masked(neg)
    k_terms = _split3_masked(-neg)
    shape = neg.shape + (LANES,)
    qx = jnp.zeros(shape, F32)
    kx = jnp.broadcast_to(jnp.where(lane == (t_pos[:, None] // MOBA_BLOCK), NEG_INF, 0.0), shape)
    for d in range(3):
        qx = jnp.where(lane == MOBA_QB + d, q_terms[d][..., None], jnp.where(lane == MOBA_KB + d, 1.0, qx))
        kx = jnp.where(lane == MOBA_QB + d, 1.0, jnp.where(lane == MOBA_KB + d, k_terms[d][..., None], kx))
    return qx.astype(BF16), kx.astype(BF16)


def _moba_kernel(q_ref, k_ref, v_ref, qx_ref, kx_ref, o_ref, qa_ref, kb_ref, vt_ref, m_ref, acc_ref):
    s_len = q_ref.shape[0]
    blk = MOBA_BLOCK
    n_blk = s_len // blk
    lane = lax.broadcasted_iota(I32, (blk, LANES), 1)
    head0 = lane < HEAD_DIM

    kb_ref[...] = k_ref[...].astype(BF16)

    k_mean = jnp.concatenate(
        [jnp.sum(k_ref[j * blk:(j + 1) * blk, :], axis=0, keepdims=True) for j in range(n_blk)],
        axis=0) * (1.0 / blk)
    q_hi, q_lo = _split_bf16(q_ref[...])

    blk_id = lax.broadcasted_iota(I32, (n_blk, s_len), 0)
    n_past = _shr(lax.broadcasted_iota(I32, (n_blk, s_len), 1), blk.bit_length() - 1)
    past = blk_id < n_past
    blk_f = blk_id.astype(F32)
    pad_rows = LANES - n_blk

    for h in range(2):
        hm = (lax.broadcasted_iota(I32, (n_blk, LANES), 1) < HEAD_DIM) == (h == 0)
        km_hi, km_lo = _split_bf16(jnp.where(hm, k_mean, 0.0))
        gate = (lax.dot_general(km_hi, q_hi, _NT, preferred_element_type=F32)
                + lax.dot_general(km_hi, q_lo, _NT, preferred_element_type=F32)
                + lax.dot_general(km_lo, q_hi, _NT, preferred_element_type=F32))
        g = jnp.where(past, gate, NEG_INF)
        sel = jnp.zeros((n_blk, s_len), F32)
        for _ in range(MOBA_TOPK):
            m = jnp.max(g, axis=0, keepdims=True)
            first = jnp.min(jnp.where(g == m, blk_f, float(n_blk)), axis=0, keepdims=True)
            pick = blk_f == first
            sel = jnp.where(pick, 1.0, sel)
            g = jnp.where(pick, -jnp.inf, g)
        not_sel = jnp.where(past, 1.0 - sel, jnp.where(blk_id == n_past, 0.0, 1.0))
        not_sel = jnp.concatenate([not_sel, jnp.zeros((pad_rows, s_len), F32)], axis=0)
        for i in range(n_blk):
            rows = slice(i * blk, (i + 1) * blk)
            qh = jnp.where(head0 if h == 0 else ~head0, q_ref[rows, :], 0.0) * SCALE
            qa_ref[h, rows, 0:LANES] = qh.astype(BF16)
            qa_ref[h, rows, LANES:] = not_sel[:, rows].T.astype(BF16) + qx_ref[h, rows, :]

    top_rows = lax.broadcasted_iota(I32, (LANES, blk), 0) < HEAD_DIM
    for j in range(n_blk):
        vt = v_ref[j * blk:(j + 1) * blk, :].T
        vt_ref[0, :, j * blk:(j + 1) * blk] = jnp.where(top_rows, vt, 1.0).astype(BF16)
        vt_ref[1, :, j * blk:(j + 1) * blk] = jnp.where(top_rows, 1.0, vt).astype(BF16)

    key_i = lax.broadcasted_iota(I32, (blk, blk), 0)
    qry_i = lax.broadcasted_iota(I32, (blk, blk), 1)
    causal = key_i <= qry_i

    def q_aug(i, h):
        r0 = pl.multiple_of(i * blk, blk)
        return jnp.concatenate([qa_ref[h, pl.ds(r0, blk), 0:LANES], qa_ref[h, pl.ds(r0, blk), LANES:]],
                               axis=1)

    def k_aug(j, h):
        c0 = pl.multiple_of(j * blk, blk)
        return jnp.concatenate([kb_ref[pl.ds(c0, blk), :], kx_ref[h, pl.ds(c0, blk), :]], axis=1)

    def v_t(j, h):
        return vt_ref[h, :, pl.ds(pl.multiple_of(j * blk, blk), blk)]

    def own_block(i, carry):
        for h in range(2):
            s = lax.dot_general(k_aug(i, h), q_aug(i, h), _NT, preferred_element_type=F32)
            s = jnp.where(causal, s, NEG_INF)
            m0 = jnp.max(s, axis=0, keepdims=True)
            m_ref[i, h] = m0
            acc_ref[i, h] = jnp.dot(v_t(i, h), jnp.exp(s - m0).astype(BF16), preferred_element_type=F32)
        return carry

    lax.fori_loop(0, n_blk, own_block, 0, unroll=2)

    def key_block(j, carry):
        ks = [k_aug(j, h) for h in range(2)]
        vs = [v_t(j, h) for h in range(2)]

        def update(i):
            for h in range(2):
                sj = lax.dot_general(ks[h], q_aug(i, h), _NT, preferred_element_type=F32)
                m = m_ref[i, h]
                m_new = jnp.maximum(m, jnp.max(sj, axis=0, keepdims=True))
                acc_ref[i, h] = jnp.exp(m - m_new) * acc_ref[i, h] + jnp.dot(
                    vs[h], jnp.exp(sj - m_new).astype(BF16), preferred_element_type=F32)
                m_ref[i, h] = m_new

        def tiles(first, count):
            def body(g, c2):
                for d in range(count):
                    update(first + count * g + d)
                return c2
            return body

        first = j + 1
        for size in MOBA_GROUPS:
            count = (n_blk - first) // size
            lax.fori_loop(0, count, tiles(first, size), 0)
            first = first + size * count
        return carry

    lax.fori_loop(0, n_blk - 1, key_block, 0)

    def finish(i, carry):
        a0, a1 = acc_ref[i, 0], acc_ref[i, 1]
        out_t = jnp.where(top_rows, a0 / a0[HEAD_DIM:HEAD_DIM + 1, :], a1 / a1[0:1, :])
        o_ref[pl.ds(pl.multiple_of(i * blk, blk), blk), :] = out_t.T
        return carry

    lax.fori_loop(0, n_blk, finish, 0, unroll=2)


def _moba(q, k, v, slopes):
    b, s_len, w = q.shape
    n_pair = w // LANES
    slab = pl.BlockSpec((None, s_len, LANES), lambda i, p: (i, 0, p))
    table = pl.BlockSpec((None, 2, s_len, LANES), lambda i, p: (p, 0, 0, 0))
    qx, kx = _moba_bias_lanes(slopes, s_len)
    return pl.pallas_call(
        _moba_kernel,
        grid=(b, n_pair),
        in_specs=[slab, slab, slab, table, table],
        out_specs=slab,
        out_shape=jax.ShapeDtypeStruct((b, s_len, w), F32),
        scratch_shapes=[pltpu.VMEM((2, s_len, 2 * LANES), BF16),
                        pltpu.VMEM((s_len, LANES), BF16),
                        pltpu.VMEM((2, LANES, s_len), BF16),
                        pltpu.VMEM((s_len // MOBA_BLOCK, 2, 1, MOBA_BLOCK), F32),
                        pltpu.VMEM((s_len // MOBA_BLOCK, 2, LANES, MOBA_BLOCK), F32)],
        compiler_params=_cparams(("parallel", "parallel")),
        name="moba",
    )(q, k, v, qx, kx)


MEMATTN_TQ = 512


def _memattn_kernel(q_ref, k_ref, v_ref, o_ref):
    q = q_ref[...]
    kb = k_ref[...].astype(BF16)
    vb = v_ref[...].astype(BF16)
    head = _shr(lax.broadcasted_iota(I32, q.shape, 1), HEAD_DIM.bit_length() - 1)
    out = jnp.zeros(q.shape, F32)
    for h in range(N_HEADS_MEM):
        qh = jnp.where(head == h, q, 0.0).astype(BF16)
        s = lax.dot_general(qh, kb, _NT, preferred_element_type=F32) * SCALE
        m = jnp.max(s, axis=-1, keepdims=True)
        p = jnp.exp(s - m)
        den = jnp.sum(p, axis=-1, keepdims=True)
        oh = jnp.dot(p.astype(BF16), vb, preferred_element_type=F32) / den
        out = jnp.where(head == h, oh, out)
    o_ref[...] = out


def _memattn(q, k, v):
    b, s_len, w = q.shape
    tq = MEMATTN_TQ
    return pl.pallas_call(
        _memattn_kernel,
        grid=(b, s_len // tq),
        in_specs=[pl.BlockSpec((None, tq, w), lambda i, j: (i, j, 0)),
                  pl.BlockSpec((None, MEM_LEN, w), lambda i, j: (i, 0, 0)),
                  pl.BlockSpec((None, MEM_LEN, w), lambda i, j: (i, 0, 0))],
        out_specs=pl.BlockSpec((None, tq, w), lambda i, j: (i, j, 0)),
        out_shape=jax.ShapeDtypeStruct((b, s_len, w), F32),
        compiler_params=_cparams(("parallel", "parallel")),
        name="memattn",
    )(q, k, v)


OUTPROJ_TM = 512


def _outproj_kernel(od_ref, ob_ref, om_ref, x_ref, gd, gb, gm, wo_ref, gf, wq_ref, k1_ref, k2_ref,
                    x1_ref, hn_ref, s1_ref, s2_ref):
    y = jnp.dot(_row_rms(od_ref[...], gd[...]).astype(BF16), wo_ref[0:W_DIL, :],
                preferred_element_type=F32)
    y += jnp.dot(_row_rms(ob_ref[...], gb[...]).astype(BF16), wo_ref[W_DIL:W_DIL + W_MOBA, :],
                 preferred_element_type=F32)
    y += jnp.dot(_row_rms(om_ref[...], gm[...]).astype(BF16), wo_ref[W_DIL + W_MOBA:, :],
                 preferred_element_type=F32)
    x1 = x_ref[...] + y
    x1_ref[...] = x1
    hb = _row_rms(x1, gf[...]).astype(BF16)
    hn_ref[...] = hb
    half = PEER_DKEY // 2
    for h in range(PEER_HEADS):
        qh = jnp.dot(hb, wq_ref[:, h * PEER_DKEY:(h + 1) * PEER_DKEY],
                     preferred_element_type=F32).astype(BF16)
        s1_ref[h] = lax.dot_general(k1_ref[h], qh[:, :half], _NT, preferred_element_type=F32)
        s2_ref[h] = lax.dot_general(k2_ref[h], qh[:, half:], _NT, preferred_element_type=F32)


def _outproj(o_dil, o_moba, o_mem, x2, og_dil, og_moba, og_mem, w_out, g_ffn, w_q, sub1, sub2):
    t = x2.shape[0]
    tm = OUTPROJ_TM
    row = lambda w: pl.BlockSpec((1, w), lambda i: (0, 0))
    tile = lambda w: pl.BlockSpec((tm, w), lambda i: (i, 0))
    full = lambda shape: pl.BlockSpec(shape, lambda i: (0,) * len(shape))
    score = pl.BlockSpec((PEER_HEADS, PEER_NKEYS, tm), lambda i: (0, 0, i))
    half = PEER_DKEY // 2
    return pl.pallas_call(
        _outproj_kernel,
        grid=(t // tm,),
        in_specs=[tile(W_DIL), tile(W_MOBA), tile(W_MEM), tile(D_MODEL),
                  row(W_DIL), row(W_MOBA), row(W_MEM), full((D_MODEL, D_MODEL)), row(D_MODEL),
                  full((D_MODEL, PEER_HEADS * PEER_DKEY)),
                  full((PEER_HEADS, PEER_NKEYS, half)), full((PEER_HEADS, PEER_NKEYS, half))],
        out_specs=[tile(D_MODEL), tile(D_MODEL), score, score],
        out_shape=[jax.ShapeDtypeStruct((t, D_MODEL), F32),
                   jax.ShapeDtypeStruct((t, D_MODEL), BF16),
                   jax.ShapeDtypeStruct((PEER_HEADS, PEER_NKEYS, t), F32),
                   jax.ShapeDtypeStruct((PEER_HEADS, PEER_NKEYS, t), F32)],
        compiler_params=_cparams(("parallel",)),
        name="outproj",
    )(o_dil, o_moba, o_mem, x2, og_dil, og_moba, og_mem, w_out, g_ffn, w_q, sub1, sub2)


TOPK_TM = 512
TOPK_COLS_PER_ITER = 8
_ID_LIMIT = 1024.0


def _top_rows(s, k, ids=None, rows_at=None):
    n_rows = s.shape[0]
    if ids is None:
        ids = lax.broadcasted_iota(I32, s.shape, 0).astype(F32)
    vals, idxs = [], []
    for t in range(k):
        r = n_rows if rows_at is None else rows_at[t]
        head, head_ids = s[:r, :], ids[:r, :]
        m = jnp.max(head, axis=0, keepdims=True)
        first = jnp.min(jnp.where(head == m, head_ids, _ID_LIMIT), axis=0, keepdims=True)
        vals.append(m)
        idxs.append(first)
        head = jnp.where(head_ids == first, -jnp.inf, head)
        s = head if r == n_rows else jnp.concatenate([head, s[r:, :]], axis=0)
    return jnp.concatenate(vals, axis=0), jnp.concatenate(idxs, axis=0)


def _pick_rows(table, sel):
    out = jnp.zeros(sel.shape, table.dtype)
    for j in range(table.shape[0]):
        out = jnp.where(sel == j, table[j:j + 1, :], out)
    return out


def _candidates(v1, v2):
    n = v1.shape[1]
    sub = lax.broadcasted_iota(I32, (8, n), 0)
    sub_f = sub.astype(F32)
    sums, ids = [], []
    for j1 in range(8):
        both = v1[j1:j1 + 1, :] + v2[0:8, :]
        sums.append(both if j1 == 0 else jnp.where(sub < PEER_TOPK // (j1 + 1), both, -jnp.inf))
        ids.append(sub_f + float(j1 * PEER_TOPK))
    sums += [v1[0:1, :] + v2[8:16, :], v1[8:16, :] + v2[0:1, :]]
    ids += [sub_f + 8.0, sub_f * float(PEER_TOPK) + float(8 * PEER_TOPK)]
    rows_at = [8 * min(t, 8) if t <= 8 else 80 for t in range(1, PEER_TOPK + 1)]
    return jnp.concatenate(sums, axis=0), jnp.concatenate(ids, axis=0), rows_at


def _retrieve(s1, s2):
    v1, i1 = _top_rows(s1, PEER_TOPK)
    v2, i2 = _top_rows(s2, PEER_TOPK)
    sums, ids, rows_at = _candidates(v1, v2)
    top_s, pair = _top_rows(sums, PEER_TOPK, ids, rows_at)
    e = jnp.exp(top_s - top_s[0:1, :])
    pair = pair.astype(I32)
    bits = PEER_TOPK.bit_length() - 1
    return (_pick_rows(i1, _shr(pair, bits)), _pick_rows(i2, pair & (PEER_TOPK - 1)),
            e / jnp.sum(e, axis=0, keepdims=True))


def _peertopk_kernel(s1_ref, s2_ref, i1_ref, i2_ref, g_ref):
    n_col = s1_ref.shape[2] // LANES

    def body(i, carry):
        for c in range(TOPK_COLS_PER_ITER):
            item = i * TOPK_COLS_PER_ITER + c
            h = _shr_scalar(item, n_col.bit_length() - 1)
            cols = pl.ds(pl.multiple_of((item & (n_col - 1)) * LANES, LANES), LANES)
            i1_ref[h, :, cols], i2_ref[h, :, cols], g_ref[h, :, cols] = _retrieve(
                s1_ref[h, :, cols], s2_ref[h, :, cols])
        return carry

    lax.fori_loop(0, PEER_HEADS * n_col // TOPK_COLS_PER_ITER, body, 0)


def _peertopk(s1, s2):
    t = s1.shape[2]
    tm = TOPK_TM
    score = pl.BlockSpec((PEER_HEADS, PEER_NKEYS, tm), lambda i: (0, 0, i))
    slot = pl.BlockSpec((PEER_HEADS, PEER_TOPK, tm), lambda i: (0, 0, i))
    return pl.pallas_call(
        _peertopk_kernel,
        grid=(t // tm,),
        in_specs=[score, score],
        out_specs=[slot, slot, slot],
        out_shape=[jax.ShapeDtypeStruct((PEER_HEADS, PEER_TOPK, t), F32)] * 3,
        compiler_params=_cparams(("parallel",)),
        name="peertopk",
    )(s1, s2)


PEER_TM = 512
PEER_TE = 2048
PEER_KEY_PAIRS = PEER_NKEYS // 2
PEER_PITCH = PEER_TM + 8
PEER_BUILD_UNROLL = 64
BF16_ROWS = 16


def _rows_bf16(row, n_rows):
    tile = jnp.broadcast_to(row, (BF16_ROWS, row.shape[1])).astype(BF16)
    return jnp.concatenate([tile] * (n_rows // BF16_ROWS), axis=0)


def _peerffn_kernel(h_ref, x1_ref, i1_ref, i2_ref, g_ref, u_ref, v_ref, o_ref, gate_ref):
    c = pl.program_id(1)
    slabs = PEER_TE // PEER_NKEYS

    @pl.when(c == 0)
    def _build_gate_matrix():
        key_id = lax.broadcasted_iota(I32, (PEER_NKEYS, PEER_SLOTS), 0).astype(BF16)
        one = jnp.ones((PEER_NKEYS, PEER_SLOTS), BF16)
        zero = jnp.zeros((PEER_NKEYS, PEER_SLOTS), BF16)

        def token(t, carry):
            i1 = _rows_bf16(i1_ref[pl.ds(t, 1), :], PEER_NKEYS)
            i2 = _rows_bf16(i2_ref[pl.ds(t, 1), :], PEER_NKEYS)
            g = _rows_bf16(g_ref[pl.ds(t, 1), :], PEER_NKEYS)
            lhs = jnp.where(key_id == i1, g, zero)
            rhs = jnp.where(key_id == i2, one, zero)
            gt = lax.dot_general(lhs, rhs, _NT, preferred_element_type=F32)
            gate_ref[pl.ds(t, PEER_KEY_PAIRS, stride=PEER_PITCH), :] = pltpu.bitcast(gt.astype(BF16), U32)
            return carry

        lax.fori_loop(0, PEER_TM, token, 0, unroll=PEER_BUILD_UNROLL)

    a = lax.dot_general(h_ref[...], u_ref[...], _NT, preferred_element_type=F32)
    ws = []
    for r in range(slabs // 2):
        start = pl.multiple_of((c * (slabs // 2) + r) * PEER_PITCH, 8)
        word = gate_ref[pl.ds(start, PEER_TM), :]
        g_even = lax.bitcast_convert_type(lax.shift_left(word, jnp.uint32(16)), F32)
        g_odd = lax.bitcast_convert_type(word & jnp.uint32(0xFFFF0000), F32)
        for j, gj in ((2 * r, g_even), (2 * r + 1, g_odd)):
            aj = a[:, j * PEER_NKEYS:(j + 1) * PEER_NKEYS]
            ws.append((0.5 * aj * (1.0 + lax.erf(aj * SQRT_HALF)) * gj).astype(BF16))
    y = jnp.dot(jnp.concatenate(ws, axis=1), v_ref[...], preferred_element_type=F32)

    @pl.when(c == 0)
    def _first():
        o_ref[...] = x1_ref[...] + y

    @pl.when(c > 0)
    def _rest():
        o_ref[...] += y


def _peerffn(hn, x1, i1, i2, gate, u, v):
    t = hn.shape[0]
    tm, te = PEER_TM, PEER_TE
    tile = lambda w: pl.BlockSpec((tm, w), lambda i, c: (i, 0))
    chunk = pl.BlockSpec((te, D_MODEL), lambda i, c: (c, 0))
    return pl.pallas_call(
        _peerffn_kernel,
        grid=(t // tm, PEER_EXPERTS // te),
        in_specs=[tile(D_MODEL), tile(D_MODEL), tile(PEER_SLOTS), tile(PEER_SLOTS), tile(PEER_SLOTS),
                  chunk, chunk],
        out_specs=tile(D_MODEL),
        out_shape=jax.ShapeDtypeStruct((t, D_MODEL), F32),
        scratch_shapes=[pltpu.VMEM((PEER_KEY_PAIRS * PEER_PITCH, LANES), U32)],
        compiler_params=_cparams(("parallel", "arbitrary")),
        name="peerffn",
    )(hn, x1, i1, i2, gate, u, v)


def _neg_slope_rows(slopes, width):
    n = slopes.shape[0]
    return jnp.broadcast_to((-slopes).reshape(n // 2, 2, 1, 1), (n // 2, 2, 1, width))


def _tiled_gain(g, reps):
    return jnp.tile(g, reps)[None, :]


def _layer(x, mem, g_mix, w_in, qg_dil, kg_dil, qg_moba, kg_moba, qg_mem, kg_mem, g_memtok, w_mem_kv,
           og_dil, og_moba, og_mem, w_out, g_ffn, w_peer_q, sub1, sub2, peer_u, peer_v):
    b, s_len, d = x.shape
    t = b * s_len
    n_mix = N_HEADS_DIL + N_HEADS_MOBA
    slopes = jnp.exp2(-8.0 * jnp.arange(1, n_mix + 1, dtype=F32) / n_mix)
    ns_dil = _neg_slope_rows(slopes[0::2], 2 * BAND_BLOCK)

    x2 = x.reshape(t, d)
    k_m, v_m = _memkv(mem, g_memtok[None, :], w_mem_kv.astype(BF16), _tiled_gain(kg_mem, N_HEADS_MEM))
    q_d, k_d, v_d, q_b, k_b, v_b, q_m = _inproj(
        x2, g_mix[None, :], w_in.astype(BF16),
        _tiled_gain(qg_dil, N_HEADS_DIL), _tiled_gain(kg_dil, N_HEADS_DIL),
        _tiled_gain(qg_moba, N_HEADS_MOBA), _tiled_gain(kg_moba, N_HEADS_MOBA),
        _tiled_gain(qg_mem, N_HEADS_MEM))
    seq = lambda a: a.reshape(b, s_len, a.shape[-1])
    o_dil = _dilated(seq(q_d), seq(k_d), seq(v_d), ns_dil)
    o_moba = _moba(seq(q_b), seq(k_b), seq(v_b), slopes[1::2])
    o_mem = _memattn(seq(q_m), k_m, v_m)
    x1, hn, s1, s2 = _outproj(
        o_dil.reshape(t, W_DIL), o_moba.reshape(t, W_MOBA), o_mem.reshape(t, W_MEM), x2,
        og_dil[None, :], og_moba[None, :], og_mem[None, :], w_out.astype(BF16), g_ffn[None, :],
        w_peer_q.astype(BF16), sub1.astype(BF16), sub2.astype(BF16))
    i1, i2, gate = _peertopk(s1, s2)
    slots = lambda a: a.reshape(PEER_SLOTS, t).T
    out = _peerffn(hn, x1, slots(i1), slots(i2), slots(gate), peer_u.astype(BF16), peer_v.astype(BF16))
    return out.reshape(b, s_len, d)


def kernel(x, mem, g_mix, w_in, qg_dil, kg_dil, qg_moba, kg_moba, qg_mem, kg_mem, g_memtok, w_mem_kv,
           og_dil, og_moba, og_mem, w_out, g_ffn, w_peer_q, peer_subkeys_1, peer_subkeys_2, peer_u,
           peer_v):
    h = x
    for layer in range(g_mix.shape[0]):
        h = _layer(h, mem, g_mix[layer], w_in[layer], qg_dil[layer], kg_dil[layer], qg_moba[layer],
                   kg_moba[layer], qg_mem[layer], kg_mem[layer], g_memtok[layer], w_mem_kv[layer],
                   og_dil[layer], og_moba[layer], og_mem[layer], w_out[layer], g_ffn[layer],
                   w_peer_q[layer], peer_subkeys_1[layer], peer_subkeys_2[layer], peer_u[layer],
                   peer_v[layer])
    return h
```

```python
import functools
import math

import jax
import jax.numpy as jnp
from jax import lax
from jax.experimental import pallas as pl
from jax.experimental.pallas import tpu as pltpu

F32 = jnp.float32
BF16 = jnp.bfloat16
I32 = jnp.int32
U32 = jnp.uint32

LANES = 128
D_MODEL = 1024
N_HEADS_DIL = 6
N_HEADS_MOBA = 6
N_HEADS_MEM = 4
HEAD_DIM = 64
W_DIL = N_HEADS_DIL * HEAD_DIM
W_MOBA = N_HEADS_MOBA * HEAD_DIM
W_MEM = N_HEADS_MEM * HEAD_DIM
IN_WIDTH = 3 * W_DIL + 3 * W_MOBA + W_MEM
DIL_CONFIGS = ((128, 1), (512, 4), (2048, 16))
BAND_BLOCK = 128
MOBA_BLOCK = 256
MOBA_TOPK = 3
MEM_LEN = 256
PEER_HEADS = 8
PEER_NKEYS = 128
PEER_EXPERTS = PEER_NKEYS * PEER_NKEYS
PEER_TOPK = 16
PEER_DKEY = 256
PEER_SLOTS = PEER_HEADS * PEER_TOPK
RMS_EPS = 1e-6
NEG_INF = -1e30
SCALE = 1.0 / math.sqrt(HEAD_DIM)
SQRT_HALF = math.sqrt(0.5)

VMEM_LIMIT = 56 * 1024 * 1024

_NT = (((1,), (1,)), ((), ()))


def _shr(a, bits):
    return lax.shift_right_logical(a, jnp.full(a.shape, bits, a.dtype))


def _shr_scalar(a, bits):
    return lax.shift_right_logical(a, jnp.int32(bits))


def _cparams(sem):
    return pltpu.CompilerParams(dimension_semantics=sem, vmem_limit_bytes=VMEM_LIMIT)


def _row_rms(a, gain):
    return a * lax.rsqrt(jnp.mean(a * a, axis=-1, keepdims=True) + RMS_EPS) * gain


def _group_mean_sq(p, group):
    w = p.shape[-1]
    bits = group.bit_length() - 1
    gi = _shr(lax.broadcasted_iota(I32, (w, w), 0), bits)
    gj = _shr(lax.broadcasted_iota(I32, (w, w), 1), bits)
    ones_bd = jnp.where(gi == gj, 1.0, 0.0).astype(BF16)
    p2 = p * p
    hi = p2.astype(BF16)
    lo = (p2 - hi.astype(F32)).astype(BF16)
    ss = (jnp.dot(hi, ones_bd, preferred_element_type=F32)
          + jnp.dot(lo, ones_bd, preferred_element_type=F32))
    return ss * (1.0 / group)


def _head_rms(p, gain):
    return p * lax.rsqrt(_group_mean_sq(p, HEAD_DIM) + RMS_EPS) * gain


def _memkv_kernel(mem_ref, g_ref, w_ref, kg_ref, k_ref, v_ref):
    hn = _row_rms(mem_ref[...], g_ref[...]).astype(BF16)
    kv = jnp.dot(hn, w_ref[...], preferred_element_type=F32)
    k_ref[...] = _head_rms(kv[:, :W_MEM], kg_ref[...])
    v_ref[...] = kv[:, W_MEM:]


def _memkv(mem, g_memtok, w_kv, kg_mem):
    b = mem.shape[0]
    return pl.pallas_call(
        _memkv_kernel,
        grid=(b,),
        in_specs=[
            pl.BlockSpec((None, MEM_LEN, D_MODEL), lambda i: (i, 0, 0)),
            pl.BlockSpec((1, D_MODEL), lambda i: (0, 0)),
            pl.BlockSpec((D_MODEL, 2 * W_MEM), lambda i: (0, 0)),
            pl.BlockSpec((1, W_MEM), lambda i: (0, 0)),
        ],
        out_specs=[pl.BlockSpec((None, MEM_LEN, W_MEM), lambda i: (i, 0, 0))] * 2,
        out_shape=[jax.ShapeDtypeStruct((b, MEM_LEN, W_MEM), F32)] * 2,
        compiler_params=_cparams(("parallel",)),
        name="memkv",
    )(mem, g_memtok, w_kv, kg_mem)


INPROJ_TM = 512


def _inproj_kernel(x_ref, g_ref, w_ref, gqd, gkd, gqb, gkb, gqm, qd, kd, vd, qb, kb, vb, qm):
    hb = _row_rms(x_ref[...], g_ref[...]).astype(BF16)

    def seg(lo, width):
        return jnp.dot(hb, w_ref[:, lo:lo + width], preferred_element_type=F32)

    qd[...] = _head_rms(seg(0, W_DIL), gqd[...])
    kd[...] = _head_rms(seg(W_DIL, W_DIL), gkd[...])
    vd[...] = seg(2 * W_DIL, W_DIL)
    base = 3 * W_DIL
    qb[...] = _head_rms(seg(base, W_MOBA), gqb[...])
    kb[...] = _head_rms(seg(base + W_MOBA, W_MOBA), gkb[...])
    vb[...] = seg(base + 2 * W_MOBA, W_MOBA)
    qm[...] = _head_rms(seg(base + 3 * W_MOBA, W_MEM), gqm[...])


def _inproj(x2, g_mix, w_in, gqd, gkd, gqb, gkb, gqm):
    t = x2.shape[0]
    tm = INPROJ_TM
    row = lambda w: pl.BlockSpec((1, w), lambda i: (0, 0))
    tile = lambda w: pl.BlockSpec((tm, w), lambda i: (i, 0))
    widths = (W_DIL, W_DIL, W_DIL, W_MOBA, W_MOBA, W_MOBA, W_MEM)
    return pl.pallas_call(
        _inproj_kernel,
        grid=(t // tm,),
        in_specs=[tile(D_MODEL), row(D_MODEL), pl.BlockSpec((D_MODEL, IN_WIDTH), lambda i: (0, 0)),
                  row(W_DIL), row(W_DIL), row(W_MOBA), row(W_MOBA), row(W_MEM)],
        out_specs=[tile(w) for w in widths],
        out_shape=[jax.ShapeDtypeStruct((t, w), F32) for w in widths],
        compiler_params=_cparams(("parallel",)),
        name="inproj",
    )(x2, g_mix, w_in, gqd, gkd, gqb, gkb, gqm)


DIL_PAD = BAND_BLOCK * max(d for _, d in DIL_CONFIGS)
DIL_MIX_ROWS = 256
DIL_UNROLL = 8


def _dil_kernel(q_ref, k_ref, v_ref, ns_ref, o_ref, kpad, vpad, ob0, ob1, ob2, lb0, lb1, lb2, bias_ref):
    s_len = q_ref.shape[0]
    zeros = jnp.zeros((DIL_PAD, LANES), F32)
    kpad[0:DIL_PAD, :] = zeros
    vpad[0:DIL_PAD, :] = zeros
    kpad[DIL_PAD:, :] = k_ref[...]
    vpad[DIL_PAD:, :] = v_ref[...]

    blk = BAND_BLOCK
    head0 = lax.broadcasted_iota(I32, (blk, LANES), 1) < HEAD_DIM
    ql = lax.broadcasted_iota(I32, (blk, 2 * blk), 0)
    kl = lax.broadcasted_iota(I32, (blk, 2 * blk), 1)
    delta = blk + ql - kl
    obs = (ob0, ob1, ob2)
    lbs = (lb0, lb1, lb2)

    for c, (window, dil) in enumerate(DIL_CONFIGS):
        reach = window // dil
        n_blk = s_len // dil // blk
        in_band = (delta >= 0) & (delta <= reach)
        dist = (delta * dil).astype(F32)
        ob, lb = obs[c], lbs[c]
        for h in range(2):
            bias = ns_ref[h] * dist
            bias_ref[h, 0] = jnp.where(in_band & (kl >= blk), bias, NEG_INF)
            bias_ref[h, 1] = jnp.where(in_band, bias, NEG_INF)

        def rows(start, size, dil=dil):
            return pl.ds(start, size) if dil == 1 else pl.ds(start, size, stride=dil)

        def block_body(n, r, dil=dil, ob=ob, lb=lb, rows=rows):
            q_start = r + dil * blk * n
            k_start = DIL_PAD + r + dil * blk * (n - 1)
            qv = q_ref[rows(q_start, blk), :] * SCALE
            kv = kpad[rows(k_start, 2 * blk), :].astype(BF16)
            vv = vpad[rows(k_start, 2 * blk), :].astype(BF16)
            variant = jnp.minimum(n, 1)
            outs, lses = [], []
            for h in range(2):
                qh = jnp.where(head0 if h == 0 else ~head0, qv, 0.0).astype(BF16)
                s = lax.dot_general(qh, kv, _NT, preferred_element_type=F32) + bias_ref[h, variant]
                m = jnp.max(s, axis=-1, keepdims=True)
                p = jnp.exp(s - m)
                den = jnp.sum(p, axis=-1, keepdims=True)
                lses.append(jnp.broadcast_to(m + jnp.log(den), (blk, LANES)))
                outs.append(jnp.dot(p.astype(BF16), vv, preferred_element_type=F32) / den)
            ob[rows(q_start, blk), :] = jnp.where(head0, outs[0], outs[1])
            lb[rows(q_start, blk), :] = jnp.where(head0, lses[0], lses[1])

        def flat_body(idx, carry, n_blk=n_blk, block_body=block_body):
            block_body(idx & (n_blk - 1), _shr_scalar(idx, n_blk.bit_length() - 1))
            return carry

        lax.fori_loop(0, dil * n_blk, flat_body, 0, unroll=DIL_UNROLL)

    def mix(i, carry):
        sl = pl.ds(pl.multiple_of(i * DIL_MIX_ROWS, DIL_MIX_ROWS), DIL_MIX_ROWS)
        l0, l1, l2 = lb0[sl, :], lb1[sl, :], lb2[sl, :]
        mx = jnp.maximum(jnp.maximum(l0, l1), l2)
        e0, e1, e2 = jnp.exp(l0 - mx), jnp.exp(l1 - mx), jnp.exp(l2 - mx)
        tot = e0 + e1 + e2
        o_ref[sl, :] = (e0 / tot) * ob0[sl, :] + (e1 / tot) * ob1[sl, :] + (e2 / tot) * ob2[sl, :]
        return carry

    lax.fori_loop(0, s_len // DIL_MIX_ROWS, mix, 0)


def _dilated(q, k, v, neg_slopes):
    b, s_len, w = q.shape
    n_pair = w // LANES
    slab = pl.BlockSpec((None, s_len, LANES), lambda i, p: (i, 0, p))
    buf = lambda rows: pltpu.VMEM((rows, LANES), F32)
    return pl.pallas_call(
        _dil_kernel,
        grid=(b, n_pair),
        in_specs=[slab, slab, slab,
                  pl.BlockSpec((None, 2, 1, 2 * BAND_BLOCK), lambda i, p: (p, 0, 0, 0))],
        out_specs=slab,
        out_shape=jax.ShapeDtypeStruct((b, s_len, w), F32),
        scratch_shapes=([buf(DIL_PAD + s_len), buf(DIL_PAD + s_len)] + [buf(s_len)] * 6
                        + [pltpu.VMEM((2, 2, BAND_BLOCK, 2 * BAND_BLOCK), F32)]),
        compiler_params=_cparams(("parallel", "parallel")),
        name="dilated",
    )(q, k, v, neg_slopes)


def _split_bf16(a):
    hi = a.astype(BF16)
    lo = (a - hi.astype(F32)).astype(BF16)
    return hi, lo


MOBA_QB = 16
MOBA_KB = 19
MOBA_GROUPS = (8, 4, 2, 1)


def _split3_masked(a):
    def keep_high_bits(x):
        bits = lax.bitcast_convert_type(x, U32) & jnp.uint32(0xFFFF0000)
        return lax.bitcast_convert_type(bits, F32)

    t0 = keep_high_bits(a)
    t1 = keep_high_bits(a - t0)
    t2 = keep_high_bits(a - t0 - t1)
    return t0, t1, t2


def _moba_bias_lanes(slopes, s_len):
    t_pos = jnp.arange(s_len, dtype=F32)
    lane = jnp.arange(LANES)
    neg = (-slopes).reshape(-1, 2, 1) * t_pos
    q_terms = _split3_masked(neg)
    k_terms = _split3_masked(-neg)
    shape = neg.shape + (LANES,)
    qx = jnp.zeros(shape, F32)
    kx = jnp.broadcast_to(jnp.where(lane == (t_pos[:, None] // MOBA_BLOCK), NEG_INF, 0.0), shape)
    for d in range(3):
        qx = jnp.where(lane == MOBA_QB + d, q_terms[d][..., None], jnp.where(lane == MOBA_KB + d, 1.0, qx))
        kx = jnp.where(lane == MOBA_QB + d, 1.0, jnp.where(lane == MOBA_KB + d, k_terms[d][..., None], kx))
    return qx.astype(BF16), kx.astype(BF16)


def _moba_kernel(q_ref, k_ref, v_ref, qx_ref, kx_ref, o_ref, qa_ref, kb_ref, vt_ref, m_ref, acc_ref):
    s_len = q_ref.shape[0]
    blk = MOBA_BLOCK
    n_blk = s_len // blk
    lane = lax.broadcasted_iota(I32, (blk, LANES), 1)
    head0 = lane < HEAD_DIM

    kb_ref[...] = k_ref[...].astype(BF16)

    k_mean = jnp.concatenate(
        [jnp.sum(k_ref[j * blk:(j + 1) * blk, :], axis=0, keepdims=True) for j in range(n_blk)],
        axis=0) * (1.0 / blk)
    q_hi, q_lo = _split_bf16(q_ref[...])

    blk_id = lax.broadcasted_iota(I32, (n_blk, s_len), 0)
    n_past = _shr(lax.broadcasted_iota(I32, (n_blk, s_len), 1), blk.bit_length() - 1)
    past = blk_id < n_past
    blk_f = blk_id.astype(F32)
    pad_rows = LANES - n_blk

    for h in range(2):
        hm = (lax.broadcasted_iota(I32, (n_blk, LANES), 1) < HEAD_DIM) == (h == 0)
        km_hi, km_lo = _split_bf16(jnp.where(hm, k_mean, 0.0))
        gate = (lax.dot_general(km_hi, q_hi, _NT, preferred_element_type=F32)
                + lax.dot_general(km_hi, q_lo, _NT, preferred_element_type=F32)
                + lax.dot_general(km_lo, q_hi, _NT, preferred_element_type=F32))
        g = jnp.where(past, gate, NEG_INF)
        sel = jnp.zeros((n_blk, s_len), F32)
        for _ in range(MOBA_TOPK):
            m = jnp.max(g, axis=0, keepdims=True)
            first = jnp.min(jnp.where(g == m, blk_f, float(n_blk)), axis=0, keepdims=True)
            pick = blk_f == first
            sel = jnp.where(pick, 1.0, sel)
            g = jnp.where(pick, -jnp.inf, g)
        not_sel = jnp.where(past, 1.0 - sel, jnp.where(blk_id == n_past, 0.0, 1.0))
        not_sel = jnp.concatenate([not_sel, jnp.zeros((pad_rows, s_len), F32)], axis=0)
        for i in range(n_blk):
            rows = slice(i * blk, (i + 1) * blk)
            qh = jnp.where(head0 if h == 0 else ~head0, q_ref[rows, :], 0.0) * SCALE
            qa_ref[h, rows, 0:LANES] = qh.astype(BF16)
            qa_ref[h, rows, LANES:] = not_sel[:, rows].T.astype(BF16) + qx_ref[h, rows, :]

    top_rows = lax.broadcasted_iota(I32, (LANES, blk), 0) < HEAD_DIM
    for j in range(n_blk):
        vt = v_ref[j * blk:(j + 1) * blk, :].T
        vt_ref[0, :, j * blk:(j + 1) * blk] = jnp.where(top_rows, vt, 1.0).astype(BF16)
        vt_ref[1, :, j * blk:(j + 1) * blk] = jnp.where(top_rows, 1.0, vt).astype(BF16)

    key_i = lax.broadcasted_iota(I32, (blk, blk), 0)
    qry_i = lax.broadcasted_iota(I32, (blk, blk), 1)
    causal = key_i <= qry_i

    def q_aug(i, h):
        r0 = pl.multiple_of(i * blk, blk)
        return jnp.concatenate([qa_ref[h, pl.ds(r0, blk), 0:LANES], qa_ref[h, pl.ds(r0, blk), LANES:]],
                               axis=1)

    def k_aug(j, h):
        c0 = pl.multiple_of(j * blk, blk)
        return jnp.concatenate([kb_ref[pl.ds(c0, blk), :], kx_ref[h, pl.ds(c0, blk), :]], axis=1)

    def v_t(j, h):
        return vt_ref[h, :, pl.ds(pl.multiple_of(j * blk, blk), blk)]

    def own_block(i, carry):
        for h in range(2):
            s = lax.dot_general(k_aug(i, h), q_aug(i, h), _NT, preferred_element_type=F32)
            s = jnp.where(causal, s, NEG_INF)
            m0 = jnp.max(s, axis=0, keepdims=True)
            m_ref[i, h] = m0
            acc_ref[i, h] = jnp.dot(v_t(i, h), jnp.exp(s - m0).astype(BF16), preferred_element_type=F32)
        return carry

    lax.fori_loop(0, n_blk, own_block, 0, unroll=2)

    def key_block(j, carry):
        ks = [k_aug(j, h) for h in range(2)]
        vs = [v_t(j, h) for h in range(2)]

        def update(i):
            for h in range(2):
                sj = lax.dot_general(ks[h], q_aug(i, h), _NT, preferred_element_type=F32)
                m = m_ref[i, h]
                m_new = jnp.maximum(m, jnp.max(sj, axis=0, keepdims=True))
                acc_ref[i, h] = jnp.exp(m - m_new) * acc_ref[i, h] + jnp.dot(
                    vs[h], jnp.exp(sj - m_new).astype(BF16), preferred_element_type=F32)
                m_ref[i, h] = m_new

        def tiles(first, count):
            def body(g, c2):
                for d in range(count):
                    update(first + count * g + d)
                return c2
            return body

        first = j + 1
        for size in MOBA_GROUPS:
            count = (n_blk - first) // size
            lax.fori_loop(0, count, tiles(first, size), 0)
            first = first + size * count
        return carry

    lax.fori_loop(0, n_blk - 1, key_block, 0)

    def finish(i, carry):
        a0, a1 = acc_ref[i, 0], acc_ref[i, 1]
        out_t = jnp.where(top_rows, a0 / a0[HEAD_DIM:HEAD_DIM + 1, :], a1 / a1[0:1, :])
        o_ref[pl.ds(pl.multiple_of(i * blk, blk), blk), :] = out_t.T
        return carry

    lax.fori_loop(0, n_blk, finish, 0, unroll=2)


def _moba(q, k, v, slopes):
    b, s_len, w = q.shape
    n_pair = w // LANES
    slab = pl.BlockSpec((None, s_len, LANES), lambda i, p: (i, 0, p))
    table = pl.BlockSpec((None, 2, s_len, LANES), lambda i, p: (p, 0, 0, 0))
    qx, kx = _moba_bias_lanes(slopes, s_len)
    return pl.pallas_call(
        _moba_kernel,
        grid=(b, n_pair),
        in_specs=[slab, slab, slab, table, table],
        out_specs=slab,
        out_shape=jax.ShapeDtypeStruct((b, s_len, w), F32),
        scratch_shapes=[pltpu.VMEM((2, s_len, 2 * LANES), BF16),
                        pltpu.VMEM((s_len, LANES), BF16),
                        pltpu.VMEM((2, LANES, s_len), BF16),
                        pltpu.VMEM((s_len // MOBA_BLOCK, 2, 1, MOBA_BLOCK), F32),
                        pltpu.VMEM((s_len // MOBA_BLOCK, 2, LANES, MOBA_BLOCK), F32)],
        compiler_params=_cparams(("parallel", "parallel")),
        name="moba",
    )(q, k, v, qx, kx)


MEMATTN_TQ = 512


def _memattn_kernel(q_ref, k_ref, v_ref, o_ref):
    q = q_ref[...]
    kb = k_ref[...].astype(BF16)
    vb = v_ref[...].astype(BF16)
    head = _shr(lax.broadcasted_iota(I32, q.shape, 1), HEAD_DIM.bit_length() - 1)
    out = jnp.zeros(q.shape, F32)
    for h in range(N_HEADS_MEM):
        qh = jnp.where(head == h, q, 0.0).astype(BF16)
        s = lax.dot_general(qh, kb, _NT, preferred_element_type=F32) * SCALE
        m = jnp.max(s, axis=-1, keepdims=True)
        p = jnp.exp(s - m)
        den = jnp.sum(p, axis=-1, keepdims=True)
        oh = jnp.dot(p.astype(BF16), vb, preferred_element_type=F32) / den
        out = jnp.where(head == h, oh, out)
    o_ref[...] = out


def _memattn(q, k, v):
    b, s_len, w = q.shape
    tq = MEMATTN_TQ
    return pl.pallas_call(
        _memattn_kernel,
        grid=(b, s_len // tq),
        in_specs=[pl.BlockSpec((None, tq, w), lambda i, j: (i, j, 0)),
                  pl.BlockSpec((None, MEM_LEN, w), lambda i, j: (i, 0, 0)),
                  pl.BlockSpec((None, MEM_LEN, w), lambda i, j: (i, 0, 0))],
        out_specs=pl.BlockSpec((None, tq, w), lambda i, j: (i, j, 0)),
        out_shape=jax.ShapeDtypeStruct((b, s_len, w), F32),
        compiler_params=_cparams(("parallel", "parallel")),
        name="memattn",
    )(q, k, v)


OUTPROJ_TM = 512


def _outproj_kernel(od_ref, ob_ref, om_ref, x_ref, gd, gb, gm, wo_ref, gf, wq_ref, k1_ref, k2_ref,
                    x1_ref, hn_ref, s1_ref, s2_ref):
    y = jnp.dot(_row_rms(od_ref[...], gd[...]).astype(BF16), wo_ref[0:W_DIL, :],
                preferred_element_type=F32)
    y += jnp.dot(_row_rms(ob_ref[...], gb[...]).astype(BF16), wo_ref[W_DIL:W_DIL + W_MOBA, :],
                 preferred_element_type=F32)
    y += jnp.dot(_row_rms(om_ref[...], gm[...]).astype(BF16), wo_ref[W_DIL + W_MOBA:, :],
                 preferred_element_type=F32)
    x1 = x_ref[...] + y
    x1_ref[...] = x1
    hb = _row_rms(x1, gf[...]).astype(BF16)
    hn_ref[...] = hb
    half = PEER_DKEY // 2
    for h in range(PEER_HEADS):
        qh = jnp.dot(hb, wq_ref[:, h * PEER_DKEY:(h + 1) * PEER_DKEY],
                     preferred_element_type=F32).astype(BF16)
        s1_ref[h] = lax.dot_general(k1_ref[h], qh[:, :half], _NT, preferred_element_type=F32)
        s2_ref[h] = lax.dot_general(k2_ref[h], qh[:, half:], _NT, preferred_element_type=F32)


def _outproj(o_dil, o_moba, o_mem, x2, og_dil, og_moba, og_mem, w_out, g_ffn, w_q, sub1, sub2):
    t = x2.shape[0]
    tm = OUTPROJ_TM
    row = lambda w: pl.BlockSpec((1, w), lambda i: (0, 0))
    tile = lambda w: pl.BlockSpec((tm, w), lambda i: (i, 0))
    full = lambda shape: pl.BlockSpec(shape, lambda i: (0,) * len(shape))
    score = pl.BlockSpec((PEER_HEADS, PEER_NKEYS, tm), lambda i: (0, 0, i))
    half = PEER_DKEY // 2
    return pl.pallas_call(
        _outproj_kernel,
        grid=(t // tm,),
        in_specs=[tile(W_DIL), tile(W_MOBA), tile(W_MEM), tile(D_MODEL),
                  row(W_DIL), row(W_MOBA), row(W_MEM), full((D_MODEL, D_MODEL)), row(D_MODEL),
                  full((D_MODEL, PEER_HEADS * PEER_DKEY)),
                  full((PEER_HEADS, PEER_NKEYS, half)), full((PEER_HEADS, PEER_NKEYS, half))],
        out_specs=[tile(D_MODEL), tile(D_MODEL), score, score],
        out_shape=[jax.ShapeDtypeStruct((t, D_MODEL), F32),
                   jax.ShapeDtypeStruct((t, D_MODEL), BF16),
                   jax.ShapeDtypeStruct((PEER_HEADS, PEER_NKEYS, t), F32),
                   jax.ShapeDtypeStruct((PEER_HEADS, PEER_NKEYS, t), F32)],
        compiler_params=_cparams(("parallel",)),
        name="outproj",
    )(o_dil, o_moba, o_mem, x2, og_dil, og_moba, og_mem, w_out, g_ffn, w_q, sub1, sub2)


TOPK_TM = 512
TOPK_COLS_PER_ITER = 8
_ID_LIMIT = 1024.0


def _top_rows(s, k, ids=None, rows_at=None):
    n_rows = s.shape[0]
    if ids is None:
        ids = lax.broadcasted_iota(I32, s.shape, 0).astype(F32)
    vals, idxs = [], []
    for t in range(k):
        r = n_rows if rows_at is None else rows_at[t]
        head, head_ids = s[:r, :], ids[:r, :]
        m = jnp.max(head, axis=0, keepdims=True)
        first = jnp.min(jnp.where(head == m, head_ids, _ID_LIMIT), axis=0, keepdims=True)
        vals.append(m)
        idxs.append(first)
        head = jnp.where(head_ids == first, -jnp.inf, head)
        s = head if r == n_rows else jnp.concatenate([head, s[r:, :]], axis=0)
    return jnp.concatenate(vals, axis=0), jnp.concatenate(idxs, axis=0)


def _pick_rows(table, sel):
    out = jnp.zeros(sel.shape, table.dtype)
    for j in range(table.shape[0]):
        out = jnp.where(sel == j, table[j:j + 1, :], out)
    return out


def _candidates(v1, v2):
    n = v1.shape[1]
    sub = lax.broadcasted_iota(I32, (8, n), 0)
    sub_f = sub.astype(F32)
    sums, ids = [], []
    for j1 in range(8):
        both = v1[j1:j1 + 1, :] + v2[0:8, :]
        sums.append(both if j1 == 0 else jnp.where(sub < PEER_TOPK // (j1 + 1), both, -jnp.inf))
        ids.append(sub_f + float(j1 * PEER_TOPK))
    sums += [v1[0:1, :] + v2[8:16, :], v1[8:16, :] + v2[0:1, :]]
    ids += [sub_f + 8.0, sub_f * float(PEER_TOPK) + float(8 * PEER_TOPK)]
    rows_at = [8 * min(t, 8) if t <= 8 else 80 for t in range(1, PEER_TOPK + 1)]
    return jnp.concatenate(sums, axis=0), jnp.concatenate(ids, axis=0), rows_at


def _retrieve(s1, s2):
    v1, i1 = _top_rows(s1, PEER_TOPK)
    v2, i2 = _top_rows(s2, PEER_TOPK)
    sums, ids, rows_at = _candidates(v1, v2)
    top_s, pair = _top_rows(sums, PEER_TOPK, ids, rows_at)
    e = jnp.exp(top_s - top_s[0:1, :])
    pair = pair.astype(I32)
    bits = PEER_TOPK.bit_length() - 1
    return (_pick_rows(i1, _shr(pair, bits)), _pick_rows(i2, pair & (PEER_TOPK - 1)),
            e / jnp.sum(e, axis=0, keepdims=True))


def _peertopk_kernel(s1_ref, s2_ref, i1_ref, i2_ref, g_ref):
    n_col = s1_ref.shape[2] // LANES

    def body(i, carry):
        for c in range(TOPK_COLS_PER_ITER):
            item = i * TOPK_COLS_PER_ITER + c
            h = _shr_scalar(item, n_col.bit_length() - 1)
            cols = pl.ds(pl.multiple_of((item & (n_col - 1)) * LANES, LANES), LANES)
            i1_ref[h, :, cols], i2_ref[h, :, cols], g_ref[h, :, cols] = _retrieve(
                s1_ref[h, :, cols], s2_ref[h, :, cols])
        return carry

    lax.fori_loop(0, PEER_HEADS * n_col // TOPK_COLS_PER_ITER, body, 0)


def _peertopk(s1, s2):
    t = s1.shape[2]
    tm = TOPK_TM
    score = pl.BlockSpec((PEER_HEADS, PEER_NKEYS, tm), lambda i: (0, 0, i))
    slot = pl.BlockSpec((PEER_HEADS, PEER_TOPK, tm), lambda i: (0, 0, i))
    return pl.pallas_call(
        _peertopk_kernel,
        grid=(t // tm,),
        in_specs=[score, score],
        out_specs=[slot, slot, slot],
        out_shape=[jax.ShapeDtypeStruct((PEER_HEADS, PEER_TOPK, t), F32)] * 3,
        compiler_params=_cparams(("parallel",)),
        name="peertopk",
    )(s1, s2)


PEER_TM = 512
PEER_TE = 2048
PEER_KEY_PAIRS = PEER_NKEYS // 2
PEER_PITCH = PEER_TM + 8
PEER_BUILD_UNROLL = 64
BF16_ROWS = 16


def _rows_bf16(row, n_rows):
    tile = jnp.broadcast_to(row, (BF16_ROWS, row.shape[1])).astype(BF16)
    return jnp.concatenate([tile] * (n_rows // BF16_ROWS), axis=0)


def _peerffn_kernel(h_ref, x1_ref, i1_ref, i2_ref, g_ref, u_ref, v_ref, o_ref, gate_ref):
    c = pl.program_id(1)
    slabs = PEER_TE // PEER_NKEYS

    @pl.when(c == 0)
    def _build_gate_matrix():
        key_id = lax.broadcasted_iota(I32, (PEER_NKEYS, PEER_SLOTS), 0).astype(BF16)
        one = jnp.ones((PEER_NKEYS, PEER_SLOTS), BF16)
        zero = jnp.zeros((PEER_NKEYS, PEER_SLOTS), BF16)

        def token(t, carry):
            i1 = _rows_bf16(i1_ref[pl.ds(t, 1), :], PEER_NKEYS)
            i2 = _rows_bf16(i2_ref[pl.ds(t, 1), :], PEER_NKEYS)
            g = _rows_bf16(g_ref[pl.ds(t, 1), :], PEER_NKEYS)
            lhs = jnp.where(key_id == i1, g, zero)
            rhs = jnp.where(key_id == i2, one, zero)
            gt = lax.dot_general(lhs, rhs, _NT, preferred_element_type=F32)
            gate_ref[pl.ds(t, PEER_KEY_PAIRS, stride=PEER_PITCH), :] = pltpu.bitcast(gt.astype(BF16), U32)
            return carry

        lax.fori_loop(0, PEER_TM, token, 0, unroll=PEER_BUILD_UNROLL)

    a = lax.dot_general(h_ref[...], u_ref[...], _NT, preferred_element_type=F32)
    ws = []
    for r in range(slabs // 2):
        start = pl.multiple_of((c * (slabs // 2) + r) * PEER_PITCH, 8)
        word = gate_ref[pl.ds(start, PEER_TM), :]
        g_even = lax.bitcast_convert_type(lax.shift_left(word, jnp.uint32(16)), F32)
        g_odd = lax.bitcast_convert_type(word & jnp.uint32(0xFFFF0000), F32)
        for j, gj in ((2 * r, g_even), (2 * r + 1, g_odd)):
            aj = a[:, j * PEER_NKEYS:(j + 1) * PEER_NKEYS]
            ws.append((0.5 * aj * (1.0 + lax.erf(aj * SQRT_HALF)) * gj).astype(BF16))
    y = jnp.dot(jnp.concatenate(ws, axis=1), v_ref[...], preferred_element_type=F32)

    @pl.when(c == 0)
    def _first():
        o_ref[...] = x1_ref[...] + y

    @pl.when(c > 0)
    def _rest():
        o_ref[...] += y


def _peerffn(hn, x1, i1, i2, gate, u, v):
    t = hn.shape[0]
    tm, te = PEER_TM, PEER_TE
    tile = lambda w: pl.BlockSpec((tm, w), lambda i, c: (i, 0))
    chunk = pl.BlockSpec((te, D_MODEL), lambda i, c: (c, 0))
    return pl.pallas_call(
        _peerffn_kernel,
        grid=(t // tm, PEER_EXPERTS // te),
        in_specs=[tile(D_MODEL), tile(D_MODEL), tile(PEER_SLOTS), tile(PEER_SLOTS), tile(PEER_SLOTS),
                  chunk, chunk],
        out_specs=tile(D_MODEL),
        out_shape=jax.ShapeDtypeStruct((t, D_MODEL), F32),
        scratch_shapes=[pltpu.VMEM((PEER_KEY_PAIRS * PEER_PITCH, LANES), U32)],
        compiler_params=_cparams(("parallel", "arbitrary")),
        name="peerffn",
    )(hn, x1, i1, i2, gate, u, v)


def _neg_slope_rows(slopes, width):
    n = slopes.shape[0]
    return jnp.broadcast_to((-slopes).reshape(n // 2, 2, 1, 1), (n // 2, 2, 1, width))


def _tiled_gain(g, reps):
    return jnp.tile(g, reps)[None, :]


def _layer(x, mem, g_mix, w_in, qg_dil, kg_dil, qg_moba, kg_moba, qg_mem, kg_mem, g_memtok, w_mem_kv,
           og_dil, og_moba, og_mem, w_out, g_ffn, w_peer_q, sub1, sub2, peer_u, peer_v):
    b, s_len, d = x.shape
    t = b * s_len
    n_mix = N_HEADS_DIL + N_HEADS_MOBA
    slopes = jnp.exp2(-8.0 * jnp.arange(1, n_mix + 1, dtype=F32) / n_mix)
    ns_dil = _neg_slope_rows(slopes[0::2], 2 * BAND_BLOCK)

    x2 = x.reshape(t, d)
    k_m, v_m = _memkv(mem, g_memtok[None, :], w_mem_kv.astype(BF16), _tiled_gain(kg_mem, N_HEADS_MEM))
    q_d, k_d, v_d, q_b, k_b, v_b, q_m = _inproj(
        x2, g_mix[None, :], w_in.astype(BF16),
        _tiled_gain(qg_dil, N_HEADS_DIL), _tiled_gain(kg_dil, N_HEADS_DIL),
        _tiled_gain(qg_moba, N_HEADS_MOBA), _tiled_gain(kg_moba, N_HEADS_MOBA),
        _tiled_gain(qg_mem, N_HEADS_MEM))
    seq = lambda a: a.reshape(b, s_len, a.shape[-1])
    o_dil = _dilated(seq(q_d), seq(k_d), seq(v_d), ns_dil)
    o_moba = _moba(seq(q_b), seq(k_b), seq(v_b), slopes[1::2])
    o_mem = _memattn(seq(q_m), k_m, v_m)
    x1, hn, s1, s2 = _outproj(
        o_dil.reshape(t, W_DIL), o_moba.reshape(t, W_MOBA), o_mem.reshape(t, W_MEM), x2,
        og_dil[None, :], og_moba[None, :], og_mem[None, :], w_out.astype(BF16), g_ffn[None, :],
        w_peer_q.astype(BF16), sub1.astype(BF16), sub2.astype(BF16))
    i1, i2, gate = _peertopk(s1, s2)
    slots = lambda a: a.reshape(PEER_SLOTS, t).T
    out = _peerffn(hn, x1, slots(i1), slots(i2), slots(gate), peer_u.astype(BF16), peer_v.astype(BF16))
    return out.reshape(b, s_len, d)


def kernel(x, mem, g_mix, w_in, qg_dil, kg_dil, qg_moba, kg_moba, qg_mem, kg_mem, g_memtok, w_mem_kv,
           og_dil, og_moba, og_mem, w_out, g_ffn, w_peer_q, peer_subkeys_1, peer_subkeys_2, peer_u,
           peer_v):
    h = x
    for layer in range(g_mix.shape[0]):
        h = _layer(h, mem, g_mix[layer], w_in[layer], qg_dil[layer], kg_dil[layer], qg_moba[layer],
                   kg_moba[layer], qg_mem[layer], kg_mem[layer], g_memtok[layer], w_mem_kv[layer],
                   og_dil[layer], og_moba[layer], og_mem[layer], w_out[layer], g_ffn[layer],
                   w_peer_q[layer], peer_subkeys_1[layer], peer_subkeys_2[layer], peer_u[layer],
                   peer_v[layer])
    return h
```

```python
import functools
import math

import jax
import jax.numpy as jnp
from jax import lax
from jax.experimental import pallas as pl
from jax.experimental.pallas import tpu as pltpu

F32 = jnp.float32
BF16 = jnp.bfloat16
I32 = jnp.int32
U32 = jnp.uint32

LANES = 128
D_MODEL = 1024
N_HEADS_DIL = 6
N_HEADS_MOBA = 6
N_HEADS_MEM = 4
HEAD_DIM = 64
W_DIL = N_HEADS_DIL * HEAD_DIM
W_MOBA = N_HEADS_MOBA * HEAD_DIM
W_MEM = N_HEADS_MEM * HEAD_DIM
IN_WIDTH = 3 * W_DIL + 3 * W_MOBA + W_MEM
DIL_CONFIGS = ((128, 1), (512, 4), (2048, 16))
BAND_BLOCK = 128
MOBA_BLOCK = 256
MOBA_TOPK = 3
MEM_LEN = 256
PEER_HEADS = 8
PEER_NKEYS = 128
PEER_EXPERTS = PEER_NKEYS * PEER_NKEYS
PEER_TOPK = 16
PEER_DKEY = 256
PEER_SLOTS = PEER_HEADS * PEER_TOPK
RMS_EPS = 1e-6
NEG_INF = -1e30
SCALE = 1.0 / math.sqrt(HEAD_DIM)
SQRT_HALF = math.sqrt(0.5)

VMEM_LIMIT = 56 * 1024 * 1024

_NT = (((1,), (1,)), ((), ()))


def _shr(a, bits):
    return lax.shift_right_logical(a, jnp.full(a.shape, bits, a.dtype))


def _shr_scalar(a, bits):
    return lax.shift_right_logical(a, jnp.int32(bits))


def _cparams(sem):
    return pltpu.CompilerParams(dimension_semantics=sem, vmem_limit_bytes=VMEM_LIMIT)


def _row_rms(a, gain):
    return a * lax.rsqrt(jnp.mean(a * a, axis=-1, keepdims=True) + RMS_EPS) * gain


def _group_mean_sq(p, group):
    w = p.shape[-1]
    bits = group.bit_length() - 1
    gi = _shr(lax.broadcasted_iota(I32, (w, w), 0), bits)
    gj = _shr(lax.broadcasted_iota(I32, (w, w), 1), bits)
    ones_bd = jnp.where(gi == gj, 1.0, 0.0).astype(BF16)
    p2 = p * p
    hi = p2.astype(BF16)
    lo = (p2 - hi.astype(F32)).astype(BF16)
    ss = (jnp.dot(hi, ones_bd, preferred_element_type=F32)
          + jnp.dot(lo, ones_bd, preferred_element_type=F32))
    return ss * (1.0 / group)


def _head_rms(p, gain):
    return p * lax.rsqrt(_group_mean_sq(p, HEAD_DIM) + RMS_EPS) * gain


def _memkv_kernel(mem_ref, g_ref, w_ref, kg_ref, k_ref, v_ref):
    hn = _row_rms(mem_ref[...], g_ref[...]).astype(BF16)
    kv = jnp.dot(hn, w_ref[...], preferred_element_type=F32)
    k_ref[...] = _head_rms(kv[:, :W_MEM], kg_ref[...])
    v_ref[...] = kv[:, W_MEM:]


def _memkv(mem, g_memtok, w_kv, kg_mem):
    b = mem.shape[0]
    return pl.pallas_call(
        _memkv_kernel,
        grid=(b,),
        in_specs=[
            pl.BlockSpec((None, MEM_LEN, D_MODEL), lambda i: (i, 0, 0)),
            pl.BlockSpec((1, D_MODEL), lambda i: (0, 0)),
            pl.BlockSpec((D_MODEL, 2 * W_MEM), lambda i: (0, 0)),
            pl.BlockSpec((1, W_MEM), lambda i: (0, 0)),
        ],
        out_specs=[pl.BlockSpec((None, MEM_LEN, W_MEM), lambda i: (i, 0, 0))] * 2,
        out_shape=[jax.ShapeDtypeStruct((b, MEM_LEN, W_MEM), F32)] * 2,
        compiler_params=_cparams(("parallel",)),
        name="memkv",
    )(mem, g_memtok, w_kv, kg_mem)


INPROJ_TM = 512


def _inproj_kernel(x_ref, g_ref, w_ref, gqd, gkd, gqb, gkb, gqm, qd, kd, vd, qb, kb, vb, qm):
    hb = _row_rms(x_ref[...], g_ref[...]).astype(BF16)
    proj = jnp.dot(hb, w_ref[...], preferred_element_type=F32)

    def seg(lo, width):
        return proj[:, lo:lo + width]

    qd[...] = _head_rms(seg(0, W_DIL), gqd[...])
    kd[...] = _head_rms(seg(W_DIL, W_DIL), gkd[...])
    vd[...] = seg(2 * W_DIL, W_DIL)
    base = 3 * W_DIL
    qb[...] = _head_rms(seg(base, W_MOBA), gqb[...])
    kb[...] = _head_rms(seg(base + W_MOBA, W_MOBA), gkb[...])
    vb[...] = seg(base + 2 * W_MOBA, W_MOBA)
    qm[...] = _head_rms(seg(base + 3 * W_MOBA, W_MEM), gqm[...])


def _inproj(x2, g_mix, w_in, gqd, gkd, gqb, gkb, gqm):
    t = x2.shape[0]
    tm = INPROJ_TM
    row = lambda w: pl.BlockSpec((1, w), lambda i: (0, 0))
    tile = lambda w: pl.BlockSpec((tm, w), lambda i: (i, 0))
    widths = (W_DIL, W_DIL, W_DIL, W_MOBA, W_MOBA, W_MOBA, W_MEM)
    return pl.pallas_call(
        _inproj_kernel,
        grid=(t // tm,),
        in_specs=[tile(D_MODEL), row(D_MODEL), pl.BlockSpec((D_MODEL, IN_WIDTH), lambda i: (0, 0)),
                  row(W_DIL), row(W_DIL), row(W_MOBA), row(W_MOBA), row(W_MEM)],
        out_specs=[tile(w) for w in widths],
        out_shape=[jax.ShapeDtypeStruct((t, w), F32) for w in widths],
        compiler_params=_cparams(("parallel",)),
        name="inproj",
    )(x2, g_mix, w_in, gqd, gkd, gqb, gkb, gqm)


DIL_PAD = BAND_BLOCK * max(d for _, d in DIL_CONFIGS)
DIL_MIX_ROWS = 256
DIL_UNROLL = 8


def _dil_kernel(q_ref, k_ref, v_ref, ns_ref, o_ref, kpad, vpad, ob0, ob1, ob2, lb0, lb1, lb2, bias_ref):
    s_len = q_ref.shape[0]
    zeros = jnp.zeros((DIL_PAD, LANES), F32)
    kpad[0:DIL_PAD, :] = zeros
    vpad[0:DIL_PAD, :] = zeros
    kpad[DIL_PAD:, :] = k_ref[...]
    vpad[DIL_PAD:, :] = v_ref[...]

    blk = BAND_BLOCK
    head0 = lax.broadcasted_iota(I32, (blk, LANES), 1) < HEAD_DIM
    ql = lax.broadcasted_iota(I32, (blk, 2 * blk), 0)
    kl = lax.broadcasted_iota(I32, (blk, 2 * blk), 1)
    delta = blk + ql - kl
    obs = (ob0, ob1, ob2)
    lbs = (lb0, lb1, lb2)

    for c, (window, dil) in enumerate(DIL_CONFIGS):
        reach = window // dil
        n_blk = s_len // dil // blk
        in_band = (delta >= 0) & (delta <= reach)
        dist = (delta * dil).astype(F32)
        ob, lb = obs[c], lbs[c]
        for h in range(2):
            bias = ns_ref[h] * dist
            bias_ref[h, 0] = jnp.where(in_band & (kl >= blk), bias, NEG_INF)
            bias_ref[h, 1] = jnp.where(in_band, bias, NEG_INF)

        def rows(start, size, dil=dil):
            return pl.ds(start, size) if dil == 1 else pl.ds(start, size, stride=dil)

        def block_body(n, r, dil=dil, ob=ob, lb=lb, rows=rows):
            q_start = r + dil * blk * n
            k_start = DIL_PAD + r + dil * blk * (n - 1)
            qv = q_ref[rows(q_start, blk), :] * SCALE
            kv = kpad[rows(k_start, 2 * blk), :].astype(BF16)
            vv = vpad[rows(k_start, 2 * blk), :].astype(BF16)
            variant = jnp.minimum(n, 1)
            outs, lses = [], []
            for h in range(2):
                qh = jnp.where(head0 if h == 0 else ~head0, qv, 0.0).astype(BF16)
                s = lax.dot_general(qh, kv, _NT, preferred_element_type=F32) + bias_ref[h, variant]
                m = jnp.max(s, axis=-1, keepdims=True)
                p = jnp.exp(s - m)
                den = jnp.sum(p, axis=-1, keepdims=True)
                lses.append(jnp.broadcast_to(m + jnp.log(den), (blk, LANES)))
                outs.append(jnp.dot(p.astype(BF16), vv, preferred_element_type=F32) / den)
            ob[rows(q_start, blk), :] = jnp.where(head0, outs[0], outs[1])
            lb[rows(q_start, blk), :] = jnp.where(head0, lses[0], lses[1])

        def flat_body(idx, carry, n_blk=n_blk, block_body=block_body):
            block_body(idx & (n_blk - 1), _shr_scalar(idx, n_blk.bit_length() - 1))
            return carry

        lax.fori_loop(0, dil * n_blk, flat_body, 0, unroll=DIL_UNROLL)

    def mix(i, carry):
        sl = pl.ds(pl.multiple_of(i * DIL_MIX_ROWS, DIL_MIX_ROWS), DIL_MIX_ROWS)
        l0, l1, l2 = lb0[sl, :], lb1[sl, :], lb2[sl, :]
        mx = jnp.maximum(jnp.maximum(l0, l1), l2)
        e0, e1, e2 = jnp.exp(l0 - mx), jnp.exp(l1 - mx), jnp.exp(l2 - mx)
        tot = e0 + e1 + e2
        o_ref[sl, :] = (e0 / tot) * ob0[sl, :] + (e1 / tot) * ob1[sl, :] + (e2 / tot) * ob2[sl, :]
        return carry

    lax.fori_loop(0, s_len // DIL_MIX_ROWS, mix, 0)


def _dilated(q, k, v, neg_slopes):
    b, s_len, w = q.shape
    n_pair = w // LANES
    slab = pl.BlockSpec((None, s_len, LANES), lambda i, p: (i, 0, p))
    buf = lambda rows: pltpu.VMEM((rows, LANES), F32)
    return pl.pallas_call(
        _dil_kernel,
        grid=(b, n_pair),
        in_specs=[slab, slab, slab,
                  pl.BlockSpec((None, 2, 1, 2 * BAND_BLOCK), lambda i, p: (p, 0, 0, 0))],
        out_specs=slab,
        out_shape=jax.ShapeDtypeStruct((b, s_len, w), F32),
        scratch_shapes=([buf(DIL_PAD + s_len), buf(DIL_PAD + s_len)] + [buf(s_len)] * 6
                        + [pltpu.VMEM((2, 2, BAND_BLOCK, 2 * BAND_BLOCK), F32)]),
        compiler_params=_cparams(("parallel", "parallel")),
        name="dilated",
    )(q, k, v, neg_slopes)


def _split_bf16(a):
    hi = a.astype(BF16)
    lo = (a - hi.astype(F32)).astype(BF16)
    return hi, lo


MOBA_QB = 16
MOBA_KB = 19
MOBA_GROUPS = (8, 4, 2, 1)


def _split3_masked(a):
    def keep_high_bits(x):
        bits = lax.bitcast_convert_type(x, U32) & jnp.uint32(0xFFFF0000)
        return lax.bitcast_convert_type(bits, F32)

    t0 = keep_high_bits(a)
    t1 = keep_high_bits(a - t0)
    t2 = keep_high_bits(a - t0 - t1)
    return t0, t1, t2


def _moba_bias_lanes(slopes, s_len):
    t_pos = jnp.arange(s_len, dtype=F32)
    lane = jnp.arange(LANES)
    neg = (-slopes).reshape(-1, 2, 1) * t_pos
    q_terms = _split3_masked(neg)
    k_terms = _split3_masked(-neg)
    shape = neg.shape + (LANES,)
    qx = jnp.zeros(shape, F32)
    kx = jnp.broadcast_to(jnp.where(lane == (t_pos[:, None] // MOBA_BLOCK), NEG_INF, 0.0), shape)
    for d in range(3):
        qx = jnp.where(lane == MOBA_QB + d, q_terms[d][..., None], jnp.where(lane == MOBA_KB + d, 1.0, qx))
        kx = jnp.where(lane == MOBA_QB + d, 1.0, jnp.where(lane == MOBA_KB + d, k_terms[d][..., None], kx))
    return qx.astype(BF16), kx.astype(BF16)


def _moba_kernel(q_ref, k_ref, v_ref, qx_ref, kx_ref, o_ref, qa_ref, kb_ref, vt_ref, m_ref, acc_ref):
    s_len = q_ref.shape[0]
    blk = MOBA_BLOCK
    n_blk = s_len // blk
    lane = lax.broadcasted_iota(I32, (blk, LANES), 1)
    head0 = lane < HEAD_DIM

    kb_ref[...] = k_ref[...].astype(BF16)

    k_mean = jnp.concatenate(
        [jnp.sum(k_ref[j * blk:(j + 1) * blk, :], axis=0, keepdims=True) for j in range(n_blk)],
        axis=0) * (1.0 / blk)
    q_hi, q_lo = _split_bf16(q_ref[...])

    blk_id = lax.broadcasted_iota(I32, (n_blk, s_len), 0)
    n_past = _shr(lax.broadcasted_iota(I32, (n_blk, s_len), 1), blk.bit_length() - 1)
    past = blk_id < n_past
    blk_f = blk_id.astype(F32)
    pad_rows = LANES - n_blk

    for h in range(2):
        hm = (lax.broadcasted_iota(I32, (n_blk, LANES), 1) < HEAD_DIM) == (h == 0)
        km_hi, km_lo = _split_bf16(jnp.where(hm, k_mean, 0.0))
        gate = (lax.dot_general(km_hi, q_hi, _NT, preferred_element_type=F32)
                + lax.dot_general(km_hi, q_lo, _NT, preferred_element_type=F32)
                + lax.dot_general(km_lo, q_hi, _NT, preferred_element_type=F32))
        g = jnp.where(past, gate, NEG_INF)
        sel = jnp.zeros((n_blk, s_len), F32)
        for _ in range(MOBA_TOPK):
            m = jnp.max(g, axis=0, keepdims=True)
            first = jnp.min(jnp.where(g == m, blk_f, float(n_blk)), axis=0, keepdims=True)
            pick = blk_f == first
            sel = jnp.where(pick, 1.0, sel)
            g = jnp.where(pick, -jnp.inf, g)
        not_sel = jnp.where(past, 1.0 - sel, jnp.where(blk_id == n_past, 0.0, 1.0))
        not_sel = jnp.concatenate([not_sel, jnp.zeros((pad_rows, s_len), F32)], axis=0)
        for i in range(n_blk):
            rows = slice(i * blk, (i + 1) * blk)
            qh = jnp.where(head0 if h == 0 else ~head0, q_ref[rows, :], 0.0) * SCALE
            qa_ref[h, rows, 0:LANES] = qh.astype(BF16)
            qa_ref[h, rows, LANES:] = not_sel[:, rows].T.astype(BF16) + qx_ref[h, rows, :]

    top_rows = lax.broadcasted_iota(I32, (LANES, blk), 0) < HEAD_DIM
    for j in range(n_blk):
        vt = v_ref[j * blk:(j + 1) * blk, :].T
        vt_ref[0, :, j * blk:(j + 1) * blk] = jnp.where(top_rows, vt, 1.0).astype(BF16)
        vt_ref[1, :, j * blk:(j + 1) * blk] = jnp.where(top_rows, 1.0, vt).astype(BF16)

    key_i = lax.broadcasted_iota(I32, (blk, blk), 0)
    qry_i = lax.broadcasted_iota(I32, (blk, blk), 1)
    causal = key_i <= qry_i

    def q_aug(i, h):
        r0 = pl.multiple_of(i * blk, blk)
        return jnp.concatenate([qa_ref[h, pl.ds(r0, blk), 0:LANES], qa_ref[h, pl.ds(r0, blk), LANES:]],
                               axis=1)

    def k_aug(j, h):
        c0 = pl.multiple_of(j * blk, blk)
        return jnp.concatenate([kb_ref[pl.ds(c0, blk), :], kx_ref[h, pl.ds(c0, blk), :]], axis=1)

    def v_t(j, h):
        return vt_ref[h, :, pl.ds(pl.multiple_of(j * blk, blk), blk)]

    def own_block(i, carry):
        for h in range(2):
            s = lax.dot_general(k_aug(i, h), q_aug(i, h), _NT, preferred_element_type=F32)
            s = jnp.where(causal, s, NEG_INF)
            m0 = jnp.max(s, axis=0, keepdims=True)
            m_ref[i, h] = m0
            acc_ref[i, h] = jnp.dot(v_t(i, h), jnp.exp(s - m0).astype(BF16), preferred_element_type=F32)
        return carry

    lax.fori_loop(0, n_blk, own_block, 0, unroll=2)

    def key_block(j, carry):
        ks = [k_aug(j, h) for h in range(2)]
        vs = [v_t(j, h) for h in range(2)]

        def update(i):
            for h in range(2):
                sj = lax.dot_general(ks[h], q_aug(i, h), _NT, preferred_element_type=F32)
                m = m_ref[i, h]
                m_new = jnp.maximum(m, jnp.max(sj, axis=0, keepdims=True))
                acc_ref[i, h] = jnp.exp(m - m_new) * acc_ref[i, h] + jnp.dot(
                    vs[h], jnp.exp(sj - m_new).astype(BF16), preferred_element_type=F32)
                m_ref[i, h] = m_new

        def tiles(first, count):
            def body(g, c2):
                for d in range(count):
                    update(first + count * g + d)
                return c2
            return body

        first = j + 1
        for size in MOBA_GROUPS:
            count = (n_blk - first) // size
            lax.fori_loop(0, count, tiles(first, size), 0)
            first = first + size * count
        return carry

    lax.fori_loop(0, n_blk - 1, key_block, 0)

    def finish(i, carry):
        a0, a1 = acc_ref[i, 0], acc_ref[i, 1]
        out_t = jnp.where(top_rows, a0 / a0[HEAD_DIM:HEAD_DIM + 1, :], a1 / a1[0:1, :])
        o_ref[pl.ds(pl.multiple_of(i * blk, blk), blk), :] = out_t.T
        return carry

    lax.fori_loop(0, n_blk, finish, 0, unroll=2)


def _moba(q, k, v, slopes):
    b, s_len, w = q.shape
    n_pair = w // LANES
    slab = pl.BlockSpec((None, s_len, LANES), lambda i, p: (i, 0, p))
    table = pl.BlockSpec((None, 2, s_len, LANES), lambda i, p: (p, 0, 0, 0))
    qx, kx = _moba_bias_lanes(slopes, s_len)
    return pl.pallas_call(
        _moba_kernel,
        grid=(b, n_pair),
        in_specs=[slab, slab, slab, table, table],
        out_specs=slab,
        out_shape=jax.ShapeDtypeStruct((b, s_len, w), F32),
        scratch_shapes=[pltpu.VMEM((2, s_len, 2 * LANES), BF16),
                        pltpu.VMEM((s_len, LANES), BF16),
                        pltpu.VMEM((2, LANES, s_len), BF16),
                        pltpu.VMEM((s_len // MOBA_BLOCK, 2, 1, MOBA_BLOCK), F32),
                        pltpu.VMEM((s_len // MOBA_BLOCK, 2, LANES, MOBA_BLOCK), F32)],
        compiler_params=_cparams(("parallel", "parallel")),
        name="moba",
    )(q, k, v, qx, kx)


MEMATTN_TQ = 512


def _memattn_kernel(q_ref, k_ref, v_ref, o_ref):
    q = q_ref[...]
    kb = k_ref[...].astype(BF16)
    vb = v_ref[...].astype(BF16)
    head = _shr(lax.broadcasted_iota(I32, q.shape, 1), HEAD_DIM.bit_length() - 1)
    out = jnp.zeros(q.shape, F32)
    for h in range(N_HEADS_MEM):
        qh = jnp.where(head == h, q, 0.0).astype(BF16)
        s = lax.dot_general(qh, kb, _NT, preferred_element_type=F32) * SCALE
        m = jnp.max(s, axis=-1, keepdims=True)
        p = jnp.exp(s - m)
        den = jnp.sum(p, axis=-1, keepdims=True)
        oh = jnp.dot(p.astype(BF16), vb, preferred_element_type=F32) / den
        out = jnp.where(head == h, oh, out)
    o_ref[...] = out


def _memattn(q, k, v):
    b, s_len, w = q.shape
    tq = MEMATTN_TQ
    return pl.pallas_call(
        _memattn_kernel,
        grid=(b, s_len // tq),
        in_specs=[pl.BlockSpec((None, tq, w), lambda i, j: (i, j, 0)),
                  pl.BlockSpec((None, MEM_LEN, w), lambda i, j: (i, 0, 0)),
                  pl.BlockSpec((None, MEM_LEN, w), lambda i, j: (i, 0, 0))],
        out_specs=pl.BlockSpec((None, tq, w), lambda i, j: (i, j, 0)),
        out_shape=jax.ShapeDtypeStruct((b, s_len, w), F32),
        compiler_params=_cparams(("parallel", "parallel")),
        name="memattn",
    )(q, k, v)


OUTPROJ_TM = 512


def _outproj_kernel(od_ref, ob_ref, om_ref, x_ref, gd, gb, gm, wo_ref, gf, wq_ref, k1_ref, k2_ref,
                    x1_ref, hn_ref, s1_ref, s2_ref):
    y = jnp.dot(_row_rms(od_ref[...], gd[...]).astype(BF16), wo_ref[0:W_DIL, :],
                preferred_element_type=F32)
    y += jnp.dot(_row_rms(ob_ref[...], gb[...]).astype(BF16), wo_ref[W_DIL:W_DIL + W_MOBA, :],
                 preferred_element_type=F32)
    y += jnp.dot(_row_rms(om_ref[...], gm[...]).astype(BF16), wo_ref[W_DIL + W_MOBA:, :],
                 preferred_element_type=F32)
    x1 = x_ref[...] + y
    x1_ref[...] = x1
    hb = _row_rms(x1, gf[...]).astype(BF16)
    hn_ref[...] = hb
    half = PEER_DKEY // 2
    qry = jnp.dot(hb, wq_ref[...], preferred_element_type=F32).astype(BF16)
    for h in range(PEER_HEADS):
        qh = qry[:, h * PEER_DKEY:(h + 1) * PEER_DKEY]
        s1_ref[h] = lax.dot_general(k1_ref[h], qh[:, :half], _NT, preferred_element_type=F32)
        s2_ref[h] = lax.dot_general(k2_ref[h], qh[:, half:], _NT, preferred_element_type=F32)


def _outproj(o_dil, o_moba, o_mem, x2, og_dil, og_moba, og_mem, w_out, g_ffn, w_q, sub1, sub2):
    t = x2.shape[0]
    tm = OUTPROJ_TM
    row = lambda w: pl.BlockSpec((1, w), lambda i: (0, 0))
    tile = lambda w: pl.BlockSpec((tm, w), lambda i: (i, 0))
    full = lambda shape: pl.BlockSpec(shape, lambda i: (0,) * len(shape))
    score = pl.BlockSpec((PEER_HEADS, PEER_NKEYS, tm), lambda i: (0, 0, i))
    half = PEER_DKEY // 2
    return pl.pallas_call(
        _outproj_kernel,
        grid=(t // tm,),
        in_specs=[tile(W_DIL), tile(W_MOBA), tile(W_MEM), tile(D_MODEL),
                  row(W_DIL), row(W_MOBA), row(W_MEM), full((D_MODEL, D_MODEL)), row(D_MODEL),
                  full((D_MODEL, PEER_HEADS * PEER_DKEY)),
                  full((PEER_HEADS, PEER_NKEYS, half)), full((PEER_HEADS, PEER_NKEYS, half))],
        out_specs=[tile(D_MODEL), tile(D_MODEL), score, score],
        out_shape=[jax.ShapeDtypeStruct((t, D_MODEL), F32),
                   jax.ShapeDtypeStruct((t, D_MODEL), BF16),
                   jax.ShapeDtypeStruct((PEER_HEADS, PEER_NKEYS, t), F32),
                   jax.ShapeDtypeStruct((PEER_HEADS, PEER_NKEYS, t), F32)],
        compiler_params=_cparams(("parallel",)),
        name="outproj",
    )(o_dil, o_moba, o_mem, x2, og_dil, og_moba, og_mem, w_out, g_ffn, w_q, sub1, sub2)


TOPK_TM = 512
TOPK_COLS_PER_ITER = 8
_ID_LIMIT = 1024.0


def _top_rows(s, k, ids=None, rows_at=None):
    n_rows = s.shape[0]
    if ids is None:
        ids = lax.broadcasted_iota(I32, s.shape, 0).astype(F32)
    vals, idxs = [], []
    for t in range(k):
        r = n_rows if rows_at is None else rows_at[t]
        head, head_ids = s[:r, :], ids[:r, :]
        m = jnp.max(head, axis=0, keepdims=True)
        first = jnp.min(jnp.where(head == m, head_ids, _ID_LIMIT), axis=0, keepdims=True)
        vals.append(m)
        idxs.append(first)
        head = jnp.where(head_ids == first, -jnp.inf, head)
        s = head if r == n_rows else jnp.concatenate([head, s[r:, :]], axis=0)
    return jnp.concatenate(vals, axis=0), jnp.concatenate(idxs, axis=0)


def _pick_rows(table, sel):
    out = jnp.zeros(sel.shape, table.dtype)
    for j in range(table.shape[0]):
        out = jnp.where(sel == j, table[j:j + 1, :], out)
    return out


def _candidates(v1, v2):
    n = v1.shape[1]
    sub = lax.broadcasted_iota(I32, (8, n), 0)
    sub_f = sub.astype(F32)
    sums, ids = [], []
    for j1 in range(8):
        both = v1[j1:j1 + 1, :] + v2[0:8, :]
        sums.append(both if j1 == 0 else jnp.where(sub < PEER_TOPK // (j1 + 1), both, -jnp.inf))
        ids.append(sub_f + float(j1 * PEER_TOPK))
    sums += [v1[0:1, :] + v2[8:16, :], v1[8:16, :] + v2[0:1, :]]
    ids += [sub_f + 8.0, sub_f * float(PEER_TOPK) + float(8 * PEER_TOPK)]
    rows_at = [8 * min(t, 8) if t <= 8 else 80 for t in range(1, PEER_TOPK + 1)]
    return jnp.concatenate(sums, axis=0), jnp.concatenate(ids, axis=0), rows_at


def _retrieve(s1, s2):
    v1, i1 = _top_rows(s1, PEER_TOPK)
    v2, i2 = _top_rows(s2, PEER_TOPK)
    sums, ids, rows_at = _candidates(v1, v2)
    top_s, pair = _top_rows(sums, PEER_TOPK, ids, rows_at)
    e = jnp.exp(top_s - top_s[0:1, :])
    pair = pair.astype(I32)
    bits = PEER_TOPK.bit_length() - 1
    return (_pick_rows(i1, _shr(pair, bits)), _pick_rows(i2, pair & (PEER_TOPK - 1)),
            e / jnp.sum(e, axis=0, keepdims=True))


def _peertopk_kernel(s1_ref, s2_ref, i1_ref, i2_ref, g_ref):
    n_col = s1_ref.shape[2] // LANES

    def body(i, carry):
        for c in range(TOPK_COLS_PER_ITER):
            item = i * TOPK_COLS_PER_ITER + c
            h = _shr_scalar(item, n_col.bit_length() - 1)
            cols = pl.ds(pl.multiple_of((item & (n_col - 1)) * LANES, LANES), LANES)
            i1_ref[h, :, cols], i2_ref[h, :, cols], g_ref[h, :, cols] = _retrieve(
                s1_ref[h, :, cols], s2_ref[h, :, cols])
        return carry

    lax.fori_loop(0, PEER_HEADS * n_col // TOPK_COLS_PER_ITER, body, 0)


def _peertopk(s1, s2):
    t = s1.shape[2]
    tm = TOPK_TM
    score = pl.BlockSpec((PEER_HEADS, PEER_NKEYS, tm), lambda i: (0, 0, i))
    slot = pl.BlockSpec((PEER_HEADS, PEER_TOPK, tm), lambda i: (0, 0, i))
    return pl.pallas_call(
        _peertopk_kernel,
        grid=(t // tm,),
        in_specs=[score, score],
        out_specs=[slot, slot, slot],
        out_shape=[jax.ShapeDtypeStruct((PEER_HEADS, PEER_TOPK, t), F32)] * 3,
        compiler_params=_cparams(("parallel",)),
        name="peertopk",
    )(s1, s2)


PEER_TM = 512
PEER_TE = 2048
PEER_KEY_PAIRS = PEER_NKEYS // 2
PEER_PITCH = PEER_TM + 8
PEER_BUILD_UNROLL = 64
BF16_ROWS = 16


def _rows_bf16(row, n_rows):
    tile = jnp.broadcast_to(row, (BF16_ROWS, row.shape[1])).astype(BF16)
    return jnp.concatenate([tile] * (n_rows // BF16_ROWS), axis=0)


def _peerffn_kernel(h_ref, x1_ref, i1_ref, i2_ref, g_ref, u_ref, v_ref, o_ref, gate_ref):
    c = pl.program_id(1)
    slabs = PEER_TE // PEER_NKEYS

    @pl.when(c == 0)
    def _build_gate_matrix():
        key_id = lax.broadcasted_iota(I32, (PEER_NKEYS, PEER_SLOTS), 0).astype(BF16)
        one = jnp.ones((PEER_NKEYS, PEER_SLOTS), BF16)
        zero = jnp.zeros((PEER_NKEYS, PEER_SLOTS), BF16)

        def token(t, carry):
            i1 = _rows_bf16(i1_ref[pl.ds(t, 1), :], PEER_NKEYS)
            i2 = _rows_bf16(i2_ref[pl.ds(t, 1), :], PEER_NKEYS)
            g = _rows_bf16(g_ref[pl.ds(t, 1), :], PEER_NKEYS)
            lhs = jnp.where(key_id == i1, g, zero)
            rhs = jnp.where(key_id == i2, one, zero)
            gt = lax.dot_general(lhs, rhs, _NT, preferred_element_type=F32)
            gate_ref[pl.ds(t, PEER_KEY_PAIRS, stride=PEER_PITCH), :] = pltpu.bitcast(gt.astype(BF16), U32)
            return carry

        lax.fori_loop(0, PEER_TM, token, 0, unroll=PEER_BUILD_UNROLL)

    a = lax.dot_general(h_ref[...], u_ref[...], _NT, preferred_element_type=F32)
    ws = []
    for r in range(slabs // 2):
        start = pl.multiple_of((c * (slabs // 2) + r) * PEER_PITCH, 8)
        word = gate_ref[pl.ds(start, PEER_TM), :]
        g_even = lax.bitcast_convert_type(lax.shift_left(word, jnp.uint32(16)), F32)
        g_odd = lax.bitcast_convert_type(word & jnp.uint32(0xFFFF0000), F32)
        for j, gj in ((2 * r, g_even), (2 * r + 1, g_odd)):
            aj = a[:, j * PEER_NKEYS:(j + 1) * PEER_NKEYS]
            ws.append((0.5 * aj * (1.0 + lax.erf(aj * SQRT_HALF)) * gj).astype(BF16))
    y = jnp.dot(jnp.concatenate(ws, axis=1), v_ref[...], preferred_element_type=F32)

    @pl.when(c == 0)
    def _first():
        o_ref[...] = x1_ref[...] + y

    @pl.when(c > 0)
    def _rest():
        o_ref[...] += y


def _peerffn(hn, x1, i1, i2, gate, u, v):
    t = hn.shape[0]
    tm, te = PEER_TM, PEER_TE
    tile = lambda w: pl.BlockSpec((tm, w), lambda i, c: (i, 0))
    chunk = pl.BlockSpec((te, D_MODEL), lambda i, c: (c, 0))
    return pl.pallas_call(
        _peerffn_kernel,
        grid=(t // tm, PEER_EXPERTS // te),
        in_specs=[tile(D_MODEL), tile(D_MODEL), tile(PEER_SLOTS), tile(PEER_SLOTS), tile(PEER_SLOTS),
                  chunk, chunk],
        out_specs=tile(D_MODEL),
        out_shape=jax.ShapeDtypeStruct((t, D_MODEL), F32),
        scratch_shapes=[pltpu.VMEM((PEER_KEY_PAIRS * PEER_PITCH, LANES), U32)],
        compiler_params=_cparams(("parallel", "arbitrary")),
        name="peerffn",
    )(hn, x1, i1, i2, gate, u, v)


def _neg_slope_rows(slopes, width):
    n = slopes.shape[0]
    return jnp.broadcast_to((-slopes).reshape(n // 2, 2, 1, 1), (n // 2, 2, 1, width))


def _tiled_gain(g, reps):
    return jnp.tile(g, reps)[None, :]


def _layer(x, mem, g_mix, w_in, qg_dil, kg_dil, qg_moba, kg_moba, qg_mem, kg_mem, g_memtok, w_mem_kv,
           og_dil, og_moba, og_mem, w_out, g_ffn, w_peer_q, sub1, sub2, peer_u, peer_v):
    b, s_len, d = x.shape
    t = b * s_len
    n_mix = N_HEADS_DIL + N_HEADS_MOBA
    slopes = jnp.exp2(-8.0 * jnp.arange(1, n_mix + 1, dtype=F32) / n_mix)
    ns_dil = _neg_slope_rows(slopes[0::2], 2 * BAND_BLOCK)

    x2 = x.reshape(t, d)
    k_m, v_m = _memkv(mem, g_memtok[None, :], w_mem_kv.astype(BF16), _tiled_gain(kg_mem, N_HEADS_MEM))
    q_d, k_d, v_d, q_b, k_b, v_b, q_m = _inproj(
        x2, g_mix[None, :], w_in.astype(BF16),
        _tiled_gain(qg_dil, N_HEADS_DIL), _tiled_gain(kg_dil, N_HEADS_DIL),
        _tiled_gain(qg_moba, N_HEADS_MOBA), _tiled_gain(kg_moba, N_HEADS_MOBA),
        _tiled_gain(qg_mem, N_HEADS_MEM))
    seq = lambda a: a.reshape(b, s_len, a.shape[-1])
    o_dil = _dilated(seq(q_d), seq(k_d), seq(v_d), ns_dil)
    o_moba = _moba(seq(q_b), seq(k_b), seq(v_b), slopes[1::2])
    o_mem = _memattn(seq(q_m), k_m, v_m)
    x1, hn, s1, s2 = _outproj(
        o_dil.reshape(t, W_DIL), o_moba.reshape(t, W_MOBA), o_mem.reshape(t, W_MEM), x2,
        og_dil[None, :], og_moba[None, :], og_mem[None, :], w_out.astype(BF16), g_ffn[None, :],
        w_peer_q.astype(BF16), sub1.astype(BF16), sub2.astype(BF16))
    i1, i2, gate = _peertopk(s1, s2)
    slots = lambda a: a.reshape(PEER_SLOTS, t).T
    out = _peerffn(hn, x1, slots(i1), slots(i2), slots(gate), peer_u.astype(BF16), peer_v.astype(BF16))
    return out.reshape(b, s_len, d)


def kernel(x, mem, g_mix, w_in, qg_dil, kg_dil, qg_moba, kg_moba, qg_mem, kg_mem, g_memtok, w_mem_kv,
           og_dil, og_moba, og_mem, w_out, g_ffn, w_peer_q, peer_subkeys_1, peer_subkeys_2, peer_u,
           peer_v):
    h = x
    for layer in range(g_mix.shape[0]):
        h = _layer(h, mem, g_mix[layer], w_in[layer], qg_dil[layer], kg_dil[layer], qg_moba[layer],
                   kg_moba[layer], qg_mem[layer], kg_mem[layer], g_memtok[layer], w_mem_kv[layer],
                   og_dil[layer], og_moba[layer], og_mem[layer], w_out[layer], g_ffn[layer],
                   w_peer_q[layer], peer_subkeys_1[layer], peer_subkeys_2[layer], peer_u[layer],
                   peer_v[layer])
    return h
```

```python
import functools
import math

import jax
import jax.numpy as jnp
from jax import lax
from jax.experimental import pallas as pl
from jax.experimental.pallas import tpu as pltpu

F32 = jnp.float32
BF16 = jnp.bfloat16
I32 = jnp.int32
U32 = jnp.uint32

LANES = 128
D_MODEL = 1024
N_HEADS_DIL = 6
N_HEADS_MOBA = 6
N_HEADS_MEM = 4
HEAD_DIM = 64
W_DIL = N_HEADS_DIL * HEAD_DIM
W_MOBA = N_HEADS_MOBA * HEAD_DIM
W_MEM = N_HEADS_MEM * HEAD_DIM
IN_WIDTH = 3 * W_DIL + 3 * W_MOBA + W_MEM
DIL_CONFIGS = ((128, 1), (512, 4), (2048, 16))
BAND_BLOCK = 128
MOBA_BLOCK = 256
MOBA_TOPK = 3
MEM_LEN = 256
PEER_HEADS = 8
PEER_NKEYS = 128
PEER_EXPERTS = PEER_NKEYS * PEER_NKEYS
PEER_TOPK = 16
PEER_DKEY = 256
PEER_SLOTS = PEER_HEADS * PEER_TOPK
RMS_EPS = 1e-6
NEG_INF = -1e30
SCALE = 1.0 / math.sqrt(HEAD_DIM)
SQRT_HALF = math.sqrt(0.5)

VMEM_LIMIT = 56 * 1024 * 1024

_NT = (((1,), (1,)), ((), ()))


def _shr(a, bits):
    return lax.shift_right_logical(a, jnp.full(a.shape, bits, a.dtype))


def _shr_scalar(a, bits):
    return lax.shift_right_logical(a, jnp.int32(bits))


def _cparams(sem):
    return pltpu.CompilerParams(dimension_semantics=sem, vmem_limit_bytes=VMEM_LIMIT)


def _row_rms(a, gain):
    return a * lax.rsqrt(jnp.mean(a * a, axis=-1, keepdims=True) + RMS_EPS) * gain


def _group_mean_sq(p, group):
    w = p.shape[-1]
    bits = group.bit_length() - 1
    gi = _shr(lax.broadcasted_iota(I32, (w, w), 0), bits)
    gj = _shr(lax.broadcasted_iota(I32, (w, w), 1), bits)
    ones_bd = jnp.where(gi == gj, 1.0, 0.0).astype(BF16)
    p2 = p * p
    hi = p2.astype(BF16)
    lo = (p2 - hi.astype(F32)).astype(BF16)
    ss = (jnp.dot(hi, ones_bd, preferred_element_type=F32)
          + jnp.dot(lo, ones_bd, preferred_element_type=F32))
    return ss * (1.0 / group)


def _head_rms(p, gain):
    return p * lax.rsqrt(_group_mean_sq(p, HEAD_DIM) + RMS_EPS) * gain


def _memkv_kernel(mem_ref, g_ref, w_ref, kg_ref, k_ref, v_ref):
    hn = _row_rms(mem_ref[...], g_ref[...]).astype(BF16)
    kv = jnp.dot(hn, w_ref[...], preferred_element_type=F32)
    k_ref[...] = _head_rms(kv[:, :W_MEM], kg_ref[...])
    v_ref[...] = kv[:, W_MEM:]


def _memkv(mem, g_memtok, w_kv, kg_mem):
    b = mem.shape[0]
    return pl.pallas_call(
        _memkv_kernel,
        grid=(b,),
        in_specs=[
            pl.BlockSpec((None, MEM_LEN, D_MODEL), lambda i: (i, 0, 0)),
            pl.BlockSpec((1, D_MODEL), lambda i: (0, 0)),
            pl.BlockSpec((D_MODEL, 2 * W_MEM), lambda i: (0, 0)),
            pl.BlockSpec((1, W_MEM), lambda i: (0, 0)),
        ],
        out_specs=[pl.BlockSpec((None, MEM_LEN, W_MEM), lambda i: (i, 0, 0))] * 2,
        out_shape=[jax.ShapeDtypeStruct((b, MEM_LEN, W_MEM), F32)] * 2,
        compiler_params=_cparams(("parallel",)),
        name="memkv",
    )(mem, g_memtok, w_kv, kg_mem)


INPROJ_TM = 512


def _inproj_kernel(x_ref, g_ref, w_ref, gqd, gkd, gqb, gkb, gqm, qd, kd, vd, qb, kb, vb, qm):
    hb = _row_rms(x_ref[...], g_ref[...]).astype(BF16)
    proj = jnp.dot(hb, w_ref[...], preferred_element_type=F32)

    def seg(lo, width):
        return proj[:, lo:lo + width]

    qd[...] = _head_rms(seg(0, W_DIL), gqd[...])
    kd[...] = _head_rms(seg(W_DIL, W_DIL), gkd[...])
    vd[...] = seg(2 * W_DIL, W_DIL)
    base = 3 * W_DIL
    qb[...] = _head_rms(seg(base, W_MOBA), gqb[...])
    kb[...] = _head_rms(seg(base + W_MOBA, W_MOBA), gkb[...])
    vb[...] = seg(base + 2 * W_MOBA, W_MOBA)
    qm[...] = _head_rms(seg(base + 3 * W_MOBA, W_MEM), gqm[...])


def _inproj(x2, g_mix, w_in, gqd, gkd, gqb, gkb, gqm):
    t = x2.shape[0]
    tm = INPROJ_TM
    row = lambda w: pl.BlockSpec((1, w), lambda i: (0, 0))
    tile = lambda w: pl.BlockSpec((tm, w), lambda i: (i, 0))
    widths = (W_DIL, W_DIL, W_DIL, W_MOBA, W_MOBA, W_MOBA, W_MEM)
    return pl.pallas_call(
        _inproj_kernel,
        grid=(t // tm,),
        in_specs=[tile(D_MODEL), row(D_MODEL), pl.BlockSpec((D_MODEL, IN_WIDTH), lambda i: (0, 0)),
                  row(W_DIL), row(W_DIL), row(W_MOBA), row(W_MOBA), row(W_MEM)],
        out_specs=[tile(w) for w in widths],
        out_shape=[jax.ShapeDtypeStruct((t, w), F32) for w in widths],
        compiler_params=_cparams(("parallel",)),
        name="inproj",
    )(x2, g_mix, w_in, gqd, gkd, gqb, gkb, gqm)


DIL_PAD = BAND_BLOCK * max(d for _, d in DIL_CONFIGS)
DIL_MIX_ROWS = 256
DIL_UNROLL = 8


def _dil_kernel(q_ref, k_ref, v_ref, ns_ref, o_ref, kpad, vpad, ob0, ob1, ob2, lb0, lb1, lb2, bias_ref):
    s_len = q_ref.shape[0]
    zeros = jnp.zeros((DIL_PAD, LANES), F32)
    kpad[0:DIL_PAD, :] = zeros
    vpad[0:DIL_PAD, :] = zeros
    kpad[DIL_PAD:, :] = k_ref[...]
    vpad[DIL_PAD:, :] = v_ref[...]

    blk = BAND_BLOCK
    head0 = lax.broadcasted_iota(I32, (blk, LANES), 1) < HEAD_DIM
    ql = lax.broadcasted_iota(I32, (blk, 2 * blk), 0)
    kl = lax.broadcasted_iota(I32, (blk, 2 * blk), 1)
    delta = blk + ql - kl
    obs = (ob0, ob1, ob2)
    lbs = (lb0, lb1, lb2)

    for c, (window, dil) in enumerate(DIL_CONFIGS):
        reach = window // dil
        n_blk = s_len // dil // blk
        in_band = (delta >= 0) & (delta <= reach)
        dist = (delta * dil).astype(F32)
        ob, lb = obs[c], lbs[c]
        for h in range(2):
            bias = ns_ref[h] * dist
            bias_ref[h, 0] = jnp.where(in_band & (kl >= blk), bias, NEG_INF)
            bias_ref[h, 1] = jnp.where(in_band, bias, NEG_INF)

        def rows(start, size, dil=dil):
            return pl.ds(start, size) if dil == 1 else pl.ds(start, size, stride=dil)

        def block_body(n, r, dil=dil, ob=ob, lb=lb, rows=rows):
            q_start = r + dil * blk * n
            k_start = DIL_PAD + r + dil * blk * (n - 1)
            qv = q_ref[rows(q_start, blk), :] * SCALE
            kv = kpad[rows(k_start, 2 * blk), :].astype(BF16)
            vv = vpad[rows(k_start, 2 * blk), :].astype(BF16)
            variant = jnp.minimum(n, 1)
            outs, lses = [], []
            for h in range(2):
                qh = jnp.where(head0 if h == 0 else ~head0, qv, 0.0).astype(BF16)
                s = lax.dot_general(qh, kv, _NT, preferred_element_type=F32) + bias_ref[h, variant]
                m = jnp.max(s, axis=-1, keepdims=True)
                p = jnp.exp(s - m)
                den = jnp.sum(p, axis=-1, keepdims=True)
                lses.append(jnp.broadcast_to(m + jnp.log(den), (blk, LANES)))
                outs.append(jnp.dot(p.astype(BF16), vv, preferred_element_type=F32) / den)
            ob[rows(q_start, blk), :] = jnp.where(head0, outs[0], outs[1])
            lb[rows(q_start, blk), :] = jnp.where(head0, lses[0], lses[1])

        def flat_body(idx, carry, n_blk=n_blk, block_body=block_body):
            block_body(idx & (n_blk - 1), _shr_scalar(idx, n_blk.bit_length() - 1))
            return carry

        lax.fori_loop(0, dil * n_blk, flat_body, 0, unroll=DIL_UNROLL)

    def mix(i, carry):
        sl = pl.ds(pl.multiple_of(i * DIL_MIX_ROWS, DIL_MIX_ROWS), DIL_MIX_ROWS)
        l0, l1, l2 = lb0[sl, :], lb1[sl, :], lb2[sl, :]
        mx = jnp.maximum(jnp.maximum(l0, l1), l2)
        e0, e1, e2 = jnp.exp(l0 - mx), jnp.exp(l1 - mx), jnp.exp(l2 - mx)
        tot = e0 + e1 + e2
        o_ref[sl, :] = (e0 / tot) * ob0[sl, :] + (e1 / tot) * ob1[sl, :] + (e2 / tot) * ob2[sl, :]
        return carry

    lax.fori_loop(0, s_len // DIL_MIX_ROWS, mix, 0)


def _dilated(q, k, v, neg_slopes):
    b, s_len, w = q.shape
    n_pair = w // LANES
    slab = pl.BlockSpec((None, s_len, LANES), lambda i, p: (i, 0, p))
    buf = lambda rows: pltpu.VMEM((rows, LANES), F32)
    return pl.pallas_call(
        _dil_kernel,
        grid=(b, n_pair),
        in_specs=[slab, slab, slab,
                  pl.BlockSpec((None, 2, 1, 2 * BAND_BLOCK), lambda i, p: (p, 0, 0, 0))],
        out_specs=slab,
        out_shape=jax.ShapeDtypeStruct((b, s_len, w), F32),
        scratch_shapes=([buf(DIL_PAD + s_len), buf(DIL_PAD + s_len)] + [buf(s_len)] * 6
                        + [pltpu.VMEM((2, 2, BAND_BLOCK, 2 * BAND_BLOCK), F32)]),
        compiler_params=_cparams(("parallel", "parallel")),
        name="dilated",
    )(q, k, v, neg_slopes)


def _split_bf16(a):
    hi = a.astype(BF16)
    lo = (a - hi.astype(F32)).astype(BF16)
    return hi, lo


MOBA_QB = 16
MOBA_KB = 19
MOBA_GROUPS = (8, 4, 2, 1)


def _split3_masked(a):
    def keep_high_bits(x):
        bits = lax.bitcast_convert_type(x, U32) & jnp.uint32(0xFFFF0000)
        return lax.bitcast_convert_type(bits, F32)

    t0 = keep_high_bits(a)
    t1 = keep_high_bits(a - t0)
    t2 = keep_high_bits(a - t0 - t1)
    return t0, t1, t2


def _moba_bias_lanes(slopes, s_len):
    t_pos = jnp.arange(s_len, dtype=F32)
    lane = jnp.arange(LANES)
    neg = (-slopes).reshape(-1, 2, 1) * t_pos
    q_terms = _split3_masked(neg)
    k_terms = _split3_masked(-neg)
    shape = neg.shape + (LANES,)
    qx = jnp.zeros(shape, F32)
    kx = jnp.broadcast_to(jnp.where(lane == (t_pos[:, None] // MOBA_BLOCK), NEG_INF, 0.0), shape)
    for d in range(3):
        qx = jnp.where(lane == MOBA_QB + d, q_terms[d][..., None], jnp.where(lane == MOBA_KB + d, 1.0, qx))
        kx = jnp.where(lane == MOBA_QB + d, 1.0, jnp.where(lane == MOBA_KB + d, k_terms[d][..., None], kx))
    return qx.astype(BF16), kx.astype(BF16)


def _moba_kernel(q_ref, k_ref, v_ref, qx_ref, kx_ref, o_ref, qa_ref, kb_ref, vt_ref, m_ref, acc_ref):
    s_len = q_ref.shape[0]
    blk = MOBA_BLOCK
    n_blk = s_len // blk
    lane = lax.broadcasted_iota(I32, (blk, LANES), 1)
    head0 = lane < HEAD_DIM

    kb_ref[...] = k_ref[...].astype(BF16)

    k_mean = jnp.concatenate(
        [jnp.sum(k_ref[j * blk:(j + 1) * blk, :], axis=0, keepdims=True) for j in range(n_blk)],
        axis=0) * (1.0 / blk)
    q_hi, q_lo = _split_bf16(q_ref[...])

    blk_id = lax.broadcasted_iota(I32, (n_blk, s_len), 0)
    n_past = _shr(lax.broadcasted_iota(I32, (n_blk, s_len), 1), blk.bit_length() - 1)
    past = blk_id < n_past
    blk_f = blk_id.astype(F32)
    pad_rows = LANES - n_blk

    for h in range(2):
        hm = (lax.broadcasted_iota(I32, (n_blk, LANES), 1) < HEAD_DIM) == (h == 0)
        km_hi, km_lo = _split_bf16(jnp.where(hm, k_mean, 0.0))
        gate = (lax.dot_general(km_hi, q_hi, _NT, preferred_element_type=F32)
                + lax.dot_general(km_hi, q_lo, _NT, preferred_element_type=F32)
                + lax.dot_general(km_lo, q_hi, _NT, preferred_element_type=F32))
        g = jnp.where(past, gate, NEG_INF)
        sel = jnp.zeros((n_blk, s_len), F32)
        for _ in range(MOBA_TOPK):
            m = jnp.max(g, axis=0, keepdims=True)
            first = jnp.min(jnp.where(g == m, blk_f, float(n_blk)), axis=0, keepdims=True)
            pick = blk_f == first
            sel = jnp.where(pick, 1.0, sel)
            g = jnp.where(pick, -jnp.inf, g)
        not_sel = jnp.where(past, 1.0 - sel, jnp.where(blk_id == n_past, 0.0, 1.0))
        not_sel = jnp.concatenate([not_sel, jnp.zeros((pad_rows, s_len), F32)], axis=0)
        for i in range(n_blk):
            rows = slice(i * blk, (i + 1) * blk)
            qh = jnp.where(head0 if h == 0 else ~head0, q_ref[rows, :], 0.0) * SCALE
            qa_ref[h, rows, 0:LANES] = qh.astype(BF16)
            qa_ref[h, rows, LANES:] = not_sel[:, rows].T.astype(BF16) + qx_ref[h, rows, :]

    top_rows = lax.broadcasted_iota(I32, (LANES, blk), 0) < HEAD_DIM
    for j in range(n_blk):
        vt = v_ref[j * blk:(j + 1) * blk, :].T
        vt_ref[0, :, j * blk:(j + 1) * blk] = jnp.where(top_rows, vt, 1.0).astype(BF16)
        vt_ref[1, :, j * blk:(j + 1) * blk] = jnp.where(top_rows, 1.0, vt).astype(BF16)

    key_i = lax.broadcasted_iota(I32, (blk, blk), 0)
    qry_i = lax.broadcasted_iota(I32, (blk, blk), 1)
    causal = key_i <= qry_i

    def q_aug(i, h, tiles=1):
        rows = pl.ds(pl.multiple_of(i * blk, blk), tiles * blk)
        return jnp.concatenate([qa_ref[h, rows, 0:LANES], qa_ref[h, rows, LANES:]], axis=1)

    def q_cols(i, tiles=1):
        return pl.ds(pl.multiple_of(i * blk, blk), tiles * blk)

    def k_aug(j, h):
        c0 = pl.multiple_of(j * blk, blk)
        return jnp.concatenate([kb_ref[pl.ds(c0, blk), :], kx_ref[h, pl.ds(c0, blk), :]], axis=1)

    def v_t(j, h):
        return vt_ref[h, :, pl.ds(pl.multiple_of(j * blk, blk), blk)]

    def own_block(i, carry):
        for h in range(2):
            s = lax.dot_general(k_aug(i, h), q_aug(i, h), _NT, preferred_element_type=F32)
            s = jnp.where(causal, s, NEG_INF)
            m0 = jnp.max(s, axis=0, keepdims=True)
            m_ref[h, :, q_cols(i)] = m0
            acc_ref[h, :, q_cols(i)] = jnp.dot(v_t(i, h), jnp.exp(s - m0).astype(BF16),
                                               preferred_element_type=F32)
        return carry

    lax.fori_loop(0, n_blk, own_block, 0, unroll=2)

    def key_block(j, carry):
        ks = [k_aug(j, h) for h in range(2)]
        vs = [v_t(j, h) for h in range(2)]

        def update(i, count):
            cols = q_cols(i, count)
            for h in range(2):
                sj = lax.dot_general(ks[h], q_aug(i, h, count), _NT, preferred_element_type=F32)
                m = m_ref[h, :, cols]
                m_new = jnp.maximum(m, jnp.max(sj, axis=0, keepdims=True))
                acc_ref[h, :, cols] = jnp.exp(m - m_new) * acc_ref[h, :, cols] + jnp.dot(
                    vs[h], jnp.exp(sj - m_new).astype(BF16), preferred_element_type=F32)
                m_ref[h, :, cols] = m_new

        def tiles(first, count):
            def body(g, c2):
                update(first + count * g, count)
                return c2
            return body

        first = j + 1
        for size in MOBA_GROUPS:
            count = (n_blk - first) // size
            lax.fori_loop(0, count, tiles(first, size), 0)
            first = first + size * count
        return carry

    lax.fori_loop(0, n_blk - 1, key_block, 0)

    def finish(i, carry):
        a0, a1 = acc_ref[0, :, q_cols(i)], acc_ref[1, :, q_cols(i)]
        out_t = jnp.where(top_rows, a0 / a0[HEAD_DIM:HEAD_DIM + 1, :], a1 / a1[0:1, :])
        o_ref[pl.ds(pl.multiple_of(i * blk, blk), blk), :] = out_t.T
        return carry

    lax.fori_loop(0, n_blk, finish, 0, unroll=2)


def _moba(q, k, v, slopes):
    b, s_len, w = q.shape
    n_pair = w // LANES
    slab = pl.BlockSpec((None, s_len, LANES), lambda i, p: (i, 0, p))
    table = pl.BlockSpec((None, 2, s_len, LANES), lambda i, p: (p, 0, 0, 0))
    qx, kx = _moba_bias_lanes(slopes, s_len)
    return pl.pallas_call(
        _moba_kernel,
        grid=(b, n_pair),
        in_specs=[slab, slab, slab, table, table],
        out_specs=slab,
        out_shape=jax.ShapeDtypeStruct((b, s_len, w), F32),
        scratch_shapes=[pltpu.VMEM((2, s_len, 2 * LANES), BF16),
                        pltpu.VMEM((s_len, LANES), BF16),
                        pltpu.VMEM((2, LANES, s_len), BF16),
                        pltpu.VMEM((2, 1, s_len), F32),
                        pltpu.VMEM((2, LANES, s_len), F32)],
        compiler_params=_cparams(("parallel", "parallel")),
        name="moba",
    )(q, k, v, qx, kx)


MEMATTN_TQ = 512


def _memattn_kernel(q_ref, k_ref, v_ref, o_ref):
    q = q_ref[...]
    kb = k_ref[...].astype(BF16)
    vb = v_ref[...].astype(BF16)
    head = _shr(lax.broadcasted_iota(I32, q.shape, 1), HEAD_DIM.bit_length() - 1)
    out = jnp.zeros(q.shape, F32)
    for h in range(N_HEADS_MEM):
        qh = jnp.where(head == h, q, 0.0).astype(BF16)
        s = lax.dot_general(qh, kb, _NT, preferred_element_type=F32) * SCALE
        m = jnp.max(s, axis=-1, keepdims=True)
        p = jnp.exp(s - m)
        den = jnp.sum(p, axis=-1, keepdims=True)
        oh = jnp.dot(p.astype(BF16), vb, preferred_element_type=F32) / den
        out = jnp.where(head == h, oh, out)
    o_ref[...] = out


def _memattn(q, k, v):
    b, s_len, w = q.shape
    tq = MEMATTN_TQ
    return pl.pallas_call(
        _memattn_kernel,
        grid=(b, s_len // tq),
        in_specs=[pl.BlockSpec((None, tq, w), lambda i, j: (i, j, 0)),
                  pl.BlockSpec((None, MEM_LEN, w), lambda i, j: (i, 0, 0)),
                  pl.BlockSpec((None, MEM_LEN, w), lambda i, j: (i, 0, 0))],
        out_specs=pl.BlockSpec((None, tq, w), lambda i, j: (i, j, 0)),
        out_shape=jax.ShapeDtypeStruct((b, s_len, w), F32),
        compiler_params=_cparams(("parallel", "parallel")),
        name="memattn",
    )(q, k, v)


OUTPROJ_TM = 512


def _outproj_kernel(od_ref, ob_ref, om_ref, x_ref, gd, gb, gm, wo_ref, gf, wq_ref, k1_ref, k2_ref,
                    x1_ref, hn_ref, s1_ref, s2_ref):
    y = jnp.dot(_row_rms(od_ref[...], gd[...]).astype(BF16), wo_ref[0:W_DIL, :],
                preferred_element_type=F32)
    y += jnp.dot(_row_rms(ob_ref[...], gb[...]).astype(BF16), wo_ref[W_DIL:W_DIL + W_MOBA, :],
                 preferred_element_type=F32)
    y += jnp.dot(_row_rms(om_ref[...], gm[...]).astype(BF16), wo_ref[W_DIL + W_MOBA:, :],
                 preferred_element_type=F32)
    x1 = x_ref[...] + y
    x1_ref[...] = x1
    hb = _row_rms(x1, gf[...]).astype(BF16)
    hn_ref[...] = hb
    half = PEER_DKEY // 2
    qry = jnp.dot(hb, wq_ref[...], preferred_element_type=F32).astype(BF16)
    for h in range(PEER_HEADS):
        qh = qry[:, h * PEER_DKEY:(h + 1) * PEER_DKEY]
        s1_ref[h] = lax.dot_general(k1_ref[h], qh[:, :half], _NT, preferred_element_type=F32)
        s2_ref[h] = lax.dot_general(k2_ref[h], qh[:, half:], _NT, preferred_element_type=F32)


def _outproj(o_dil, o_moba, o_mem, x2, og_dil, og_moba, og_mem, w_out, g_ffn, w_q, sub1, sub2):
    t = x2.shape[0]
    tm = OUTPROJ_TM
    row = lambda w: pl.BlockSpec((1, w), lambda i: (0, 0))
    tile = lambda w: pl.BlockSpec((tm, w), lambda i: (i, 0))
    full = lambda shape: pl.BlockSpec(shape, lambda i: (0,) * len(shape))
    score = pl.BlockSpec((PEER_HEADS, PEER_NKEYS, tm), lambda i: (0, 0, i))
    half = PEER_DKEY // 2
    return pl.pallas_call(
        _outproj_kernel,
        grid=(t // tm,),
        in_specs=[tile(W_DIL), tile(W_MOBA), tile(W_MEM), tile(D_MODEL),
                  row(W_DIL), row(W_MOBA), row(W_MEM), full((D_MODEL, D_MODEL)), row(D_MODEL),
                  full((D_MODEL, PEER_HEADS * PEER_DKEY)),
                  full((PEER_HEADS, PEER_NKEYS, half)), full((PEER_HEADS, PEER_NKEYS, half))],
        out_specs=[tile(D_MODEL), tile(D_MODEL), score, score],
        out_shape=[jax.ShapeDtypeStruct((t, D_MODEL), F32),
                   jax.ShapeDtypeStruct((t, D_MODEL), BF16),
                   jax.ShapeDtypeStruct((PEER_HEADS, PEER_NKEYS, t), F32),
                   jax.ShapeDtypeStruct((PEER_HEADS, PEER_NKEYS, t), F32)],
        compiler_params=_cparams(("parallel",)),
        name="outproj",
    )(o_dil, o_moba, o_mem, x2, og_dil, og_moba, og_mem, w_out, g_ffn, w_q, sub1, sub2)


TOPK_TM = 512
TOPK_COLS_PER_ITER = 8
_ID_LIMIT = 1024.0


def _top_rows(s, k, ids=None, rows_at=None):
    n_rows = s.shape[0]
    if ids is None:
        ids = lax.broadcasted_iota(I32, s.shape, 0).astype(F32)
    vals, idxs = [], []
    for t in range(k):
        r = n_rows if rows_at is None else rows_at[t]
        head, head_ids = s[:r, :], ids[:r, :]
        m = jnp.max(head, axis=0, keepdims=True)
        first = jnp.min(jnp.where(head == m, head_ids, _ID_LIMIT), axis=0, keepdims=True)
        vals.append(m)
        idxs.append(first)
        head = jnp.where(head_ids == first, -jnp.inf, head)
        s = head if r == n_rows else jnp.concatenate([head, s[r:, :]], axis=0)
    return jnp.concatenate(vals, axis=0), jnp.concatenate(idxs, axis=0)


def _pick_rows(table, sel):
    out = jnp.zeros(sel.shape, table.dtype)
    for j in range(table.shape[0]):
        out = jnp.where(sel == j, table[j:j + 1, :], out)
    return out


def _candidates(v1, v2):
    n = v1.shape[1]
    sub = lax.broadcasted_iota(I32, (8, n), 0)
    sub_f = sub.astype(F32)
    sums, ids = [], []
    for j1 in range(8):
        both = v1[j1:j1 + 1, :] + v2[0:8, :]
        sums.append(both if j1 == 0 else jnp.where(sub < PEER_TOPK // (j1 + 1), both, -jnp.inf))
        ids.append(sub_f + float(j1 * PEER_TOPK))
    sums += [v1[0:1, :] + v2[8:16, :], v1[8:16, :] + v2[0:1, :]]
    ids += [sub_f + 8.0, sub_f * float(PEER_TOPK) + float(8 * PEER_TOPK)]
    rows_at = [8 * min(t, 8) if t <= 8 else 80 for t in range(1, PEER_TOPK + 1)]
    return jnp.concatenate(sums, axis=0), jnp.concatenate(ids, axis=0), rows_at


def _retrieve(s1, s2):
    v1, i1 = _top_rows(s1, PEER_TOPK)
    v2, i2 = _top_rows(s2, PEER_TOPK)
    sums, ids, rows_at = _candidates(v1, v2)
    top_s, pair = _top_rows(sums, PEER_TOPK, ids, rows_at)
    e = jnp.exp(top_s - top_s[0:1, :])
    pair = pair.astype(I32)
    bits = PEER_TOPK.bit_length() - 1
    return (_pick_rows(i1, _shr(pair, bits)), _pick_rows(i2, pair & (PEER_TOPK - 1)),
            e / jnp.sum(e, axis=0, keepdims=True))


def _peertopk_kernel(s1_ref, s2_ref, i1_ref, i2_ref, g_ref):
    n_col = s1_ref.shape[2] // LANES

    def body(i, carry):
        for c in range(TOPK_COLS_PER_ITER):
            item = i * TOPK_COLS_PER_ITER + c
            h = _shr_scalar(item, n_col.bit_length() - 1)
            cols = pl.ds(pl.multiple_of((item & (n_col - 1)) * LANES, LANES), LANES)
            i1_ref[h, :, cols], i2_ref[h, :, cols], g_ref[h, :, cols] = _retrieve(
                s1_ref[h, :, cols], s2_ref[h, :, cols])
        return carry

    lax.fori_loop(0, PEER_HEADS * n_col // TOPK_COLS_PER_ITER, body, 0)


def _peertopk(s1, s2):
    t = s1.shape[2]
    tm = TOPK_TM
    score = pl.BlockSpec((PEER_HEADS, PEER_NKEYS, tm), lambda i: (0, 0, i))
    slot = pl.BlockSpec((PEER_HEADS, PEER_TOPK, tm), lambda i: (0, 0, i))
    return pl.pallas_call(
        _peertopk_kernel,
        grid=(t // tm,),
        in_specs=[score, score],
        out_specs=[slot, slot, slot],
        out_shape=[jax.ShapeDtypeStruct((PEER_HEADS, PEER_TOPK, t), F32)] * 3,
        compiler_params=_cparams(("parallel",)),
        name="peertopk",
    )(s1, s2)


PEER_TM = 512
PEER_TE = 2048
PEER_KEY_PAIRS = PEER_NKEYS // 2
PEER_PITCH = PEER_TM + 8
PEER_BUILD_UNROLL = 64
BF16_ROWS = 16


def _rows_bf16(row, n_rows):
    tile = jnp.broadcast_to(row, (BF16_ROWS, row.shape[1])).astype(BF16)
    return jnp.concatenate([tile] * (n_rows // BF16_ROWS), axis=0)


def _peerffn_kernel(h_ref, x1_ref, i1_ref, i2_ref, g_ref, u_ref, v_ref, o_ref, gate_ref):
    c = pl.program_id(1)
    slabs = PEER_TE // PEER_NKEYS

    @pl.when(c == 0)
    def _build_gate_matrix():
        key_id = lax.broadcasted_iota(I32, (PEER_NKEYS, PEER_SLOTS), 0).astype(BF16)
        one = jnp.ones((PEER_NKEYS, PEER_SLOTS), BF16)
        zero = jnp.zeros((PEER_NKEYS, PEER_SLOTS), BF16)

        def token(t, carry):
            i1 = _rows_bf16(i1_ref[pl.ds(t, 1), :], PEER_NKEYS)
            i2 = _rows_bf16(i2_ref[pl.ds(t, 1), :], PEER_NKEYS)
            g = _rows_bf16(g_ref[pl.ds(t, 1), :], PEER_NKEYS)
            lhs = jnp.where(key_id == i1, g, zero)
            rhs = jnp.where(key_id == i2, one, zero)
            gt = lax.dot_general(lhs, rhs, _NT, preferred_element_type=F32)
            gate_ref[pl.ds(t, PEER_KEY_PAIRS, stride=PEER_PITCH), :] = pltpu.bitcast(gt.astype(BF16), U32)
            return carry

        lax.fori_loop(0, PEER_TM, token, 0, unroll=PEER_BUILD_UNROLL)

    a = lax.dot_general(h_ref[...], u_ref[...], _NT, preferred_element_type=F32)
    ws = []
    for r in range(slabs // 2):
        start = pl.multiple_of((c * (slabs // 2) + r) * PEER_PITCH, 8)
        word = gate_ref[pl.ds(start, PEER_TM), :]
        g_even = lax.bitcast_convert_type(lax.shift_left(word, jnp.uint32(16)), F32)
        g_odd = lax.bitcast_convert_type(word & jnp.uint32(0xFFFF0000), F32)
        for j, gj in ((2 * r, g_even), (2 * r + 1, g_odd)):
            aj = a[:, j * PEER_NKEYS:(j + 1) * PEER_NKEYS]
            ws.append((0.5 * aj * (1.0 + lax.erf(aj * SQRT_HALF)) * gj).astype(BF16))
    y = jnp.dot(jnp.concatenate(ws, axis=1), v_ref[...], preferred_element_type=F32)

    @pl.when(c == 0)
    def _first():
        o_ref[...] = x1_ref[...] + y

    @pl.when(c > 0)
    def _rest():
        o_ref[...] += y


def _peerffn(hn, x1, i1, i2, gate, u, v):
    t = hn.shape[0]
    tm, te = PEER_TM, PEER_TE
    tile = lambda w: pl.BlockSpec((tm, w), lambda i, c: (i, 0))
    chunk = pl.BlockSpec((te, D_MODEL), lambda i, c: (c, 0))
    return pl.pallas_call(
        _peerffn_kernel,
        grid=(t // tm, PEER_EXPERTS // te),
        in_specs=[tile(D_MODEL), tile(D_MODEL), tile(PEER_SLOTS), tile(PEER_SLOTS), tile(PEER_SLOTS),
                  chunk, chunk],
        out_specs=tile(D_MODEL),
        out_shape=jax.ShapeDtypeStruct((t, D_MODEL), F32),
        scratch_shapes=[pltpu.VMEM((PEER_KEY_PAIRS * PEER_PITCH, LANES), U32)],
        compiler_params=_cparams(("parallel", "arbitrary")),
        name="peerffn",
    )(hn, x1, i1, i2, gate, u, v)


def _neg_slope_rows(slopes, width):
    n = slopes.shape[0]
    return jnp.broadcast_to((-slopes).reshape(n // 2, 2, 1, 1), (n // 2, 2, 1, width))


def _tiled_gain(g, reps):
    return jnp.tile(g, reps)[None, :]


def _layer(x, mem, g_mix, w_in, qg_dil, kg_dil, qg_moba, kg_moba, qg_mem, kg_mem, g_memtok, w_mem_kv,
           og_dil, og_moba, og_mem, w_out, g_ffn, w_peer_q, sub1, sub2, peer_u, peer_v):
    b, s_len, d = x.shape
    t = b * s_len
    n_mix = N_HEADS_DIL + N_HEADS_MOBA
    slopes = jnp.exp2(-8.0 * jnp.arange(1, n_mix + 1, dtype=F32) / n_mix)
    ns_dil = _neg_slope_rows(slopes[0::2], 2 * BAND_BLOCK)

    x2 = x.reshape(t, d)
    k_m, v_m = _memkv(mem, g_memtok[None, :], w_mem_kv.astype(BF16), _tiled_gain(kg_mem, N_HEADS_MEM))
    q_d, k_d, v_d, q_b, k_b, v_b, q_m = _inproj(
        x2, g_mix[None, :], w_in.astype(BF16),
        _tiled_gain(qg_dil, N_HEADS_DIL), _tiled_gain(kg_dil, N_HEADS_DIL),
        _tiled_gain(qg_moba, N_HEADS_MOBA), _tiled_gain(kg_moba, N_HEADS_MOBA),
        _tiled_gain(qg_mem, N_HEADS_MEM))
    seq = lambda a: a.reshape(b, s_len, a.shape[-1])
    o_dil = _dilated(seq(q_d), seq(k_d), seq(v_d), ns_dil)
    o_moba = _moba(seq(q_b), seq(k_b), seq(v_b), slopes[1::2])
    o_mem = _memattn(seq(q_m), k_m, v_m)
    x1, hn, s1, s2 = _outproj(
        o_dil.reshape(t, W_DIL), o_moba.reshape(t, W_MOBA), o_mem.reshape(t, W_MEM), x2,
        og_dil[None, :], og_moba[None, :], og_mem[None, :], w_out.astype(BF16), g_ffn[None, :],
        w_peer_q.astype(BF16), sub1.astype(BF16), sub2.astype(BF16))
    i1, i2, gate = _peertopk(s1, s2)
    slots = lambda a: a.reshape(PEER_SLOTS, t).T
    out = _peerffn(hn, x1, slots(i1), slots(i2), slots(gate), peer_u.astype(BF16), peer_v.astype(BF16))
    return out.reshape(b, s_len, d)


def kernel(x, mem, g_mix, w_in, qg_dil, kg_dil, qg_moba, kg_moba, qg_mem, kg_mem, g_memtok, w_mem_kv,
           og_dil, og_moba, og_mem, w_out, g_ffn, w_peer_q, peer_subkeys_1, peer_subkeys_2, peer_u,
           peer_v):
    h = x
    for layer in range(g_mix.shape[0]):
        h = _layer(h, mem, g_mix[layer], w_in[layer], qg_dil[layer], kg_dil[layer], qg_moba[layer],
                   kg_moba[layer], qg_mem[layer], kg_mem[layer], g_memtok[layer], w_mem_kv[layer],
                   og_dil[layer], og_moba[layer], og_mem[layer], w_out[layer], g_ffn[layer],
                   w_peer_q[layer], peer_subkeys_1[layer], peer_subkeys_2[layer], peer_u[layer],
                   peer_v[layer])
    return h
```

```python
import functools
import math

import jax
import jax.numpy as jnp
from jax import lax
from jax.experimental import pallas as pl
from jax.experimental.pallas import tpu as pltpu

F32 = jnp.float32
BF16 = jnp.bfloat16
I32 = jnp.int32
U32 = jnp.uint32

LANES = 128
D_MODEL = 1024
N_HEADS_DIL = 6
N_HEADS_MOBA = 6
N_HEADS_MEM = 4
HEAD_DIM = 64
W_DIL = N_HEADS_DIL * HEAD_DIM
W_MOBA = N_HEADS_MOBA * HEAD_DIM
W_MEM = N_HEADS_MEM * HEAD_DIM
IN_WIDTH = 3 * W_DIL + 3 * W_MOBA + W_MEM
DIL_CONFIGS = ((128, 1), (512, 4), (2048, 16))
BAND_BLOCK = 128
MOBA_BLOCK = 256
MOBA_TOPK = 3
MEM_LEN = 256
PEER_HEADS = 8
PEER_NKEYS = 128
PEER_EXPERTS = PEER_NKEYS * PEER_NKEYS
PEER_TOPK = 16
PEER_DKEY = 256
PEER_SLOTS = PEER_HEADS * PEER_TOPK
RMS_EPS = 1e-6
NEG_INF = -1e30
SCALE = 1.0 / math.sqrt(HEAD_DIM)
SQRT_HALF = math.sqrt(0.5)

VMEM_LIMIT = 56 * 1024 * 1024

_NT = (((1,), (1,)), ((), ()))


def _shr(a, bits):
    return lax.shift_right_logical(a, jnp.full(a.shape, bits, a.dtype))


def _shr_scalar(a, bits):
    return lax.shift_right_logical(a, jnp.int32(bits))


def _cparams(sem):
    return pltpu.CompilerParams(dimension_semantics=sem, vmem_limit_bytes=VMEM_LIMIT)


def _row_rms(a, gain):
    return a * lax.rsqrt(jnp.mean(a * a, axis=-1, keepdims=True) + RMS_EPS) * gain


def _group_mean_sq(p, group):
    w = p.shape[-1]
    bits = group.bit_length() - 1
    gi = _shr(lax.broadcasted_iota(I32, (w, w), 0), bits)
    gj = _shr(lax.broadcasted_iota(I32, (w, w), 1), bits)
    ones_bd = jnp.where(gi == gj, 1.0, 0.0).astype(BF16)
    p2 = p * p
    hi = p2.astype(BF16)
    lo = (p2 - hi.astype(F32)).astype(BF16)
    ss = (jnp.dot(hi, ones_bd, preferred_element_type=F32)
          + jnp.dot(lo, ones_bd, preferred_element_type=F32))
    return ss * (1.0 / group)


def _head_rms(p, gain):
    return p * lax.rsqrt(_group_mean_sq(p, HEAD_DIM) + RMS_EPS) * gain


def _memkv_kernel(mem_ref, g_ref, w_ref, kg_ref, k_ref, v_ref):
    hn = _row_rms(mem_ref[...], g_ref[...]).astype(BF16)
    kv = jnp.dot(hn, w_ref[...], preferred_element_type=F32)
    k_ref[...] = _head_rms(kv[:, :W_MEM], kg_ref[...])
    v_ref[...] = kv[:, W_MEM:]


def _memkv(mem, g_memtok, w_kv, kg_mem):
    b = mem.shape[0]
    return pl.pallas_call(
        _memkv_kernel,
        grid=(b,),
        in_specs=[
            pl.BlockSpec((None, MEM_LEN, D_MODEL), lambda i: (i, 0, 0)),
            pl.BlockSpec((1, D_MODEL), lambda i: (0, 0)),
            pl.BlockSpec((D_MODEL, 2 * W_MEM), lambda i: (0, 0)),
            pl.BlockSpec((1, W_MEM), lambda i: (0, 0)),
        ],
        out_specs=[pl.BlockSpec((None, MEM_LEN, W_MEM), lambda i: (i, 0, 0))] * 2,
        out_shape=[jax.ShapeDtypeStruct((b, MEM_LEN, W_MEM), F32)] * 2,
        compiler_params=_cparams(("parallel",)),
        name="memkv",
    )(mem, g_memtok, w_kv, kg_mem)


INPROJ_TM = 512


def _inproj_kernel(x_ref, g_ref, w_ref, gqd, gkd, gqb, gkb, gqm, qd, kd, vd, qb, kb, vb, qm):
    hb = _row_rms(x_ref[...], g_ref[...]).astype(BF16)
    proj = jnp.dot(hb, w_ref[...], preferred_element_type=F32)

    def seg(lo, width):
        return proj[:, lo:lo + width]

    qd[...] = _head_rms(seg(0, W_DIL), gqd[...])
    kd[...] = _head_rms(seg(W_DIL, W_DIL), gkd[...])
    vd[...] = seg(2 * W_DIL, W_DIL)
    base = 3 * W_DIL
    qb[...] = _head_rms(seg(base, W_MOBA), gqb[...])
    kb[...] = _head_rms(seg(base + W_MOBA, W_MOBA), gkb[...])
    vb[...] = seg(base + 2 * W_MOBA, W_MOBA)
    qm[...] = _head_rms(seg(base + 3 * W_MOBA, W_MEM), gqm[...])


def _inproj(x2, g_mix, w_in, gqd, gkd, gqb, gkb, gqm):
    t = x2.shape[0]
    tm = INPROJ_TM
    row = lambda w: pl.BlockSpec((1, w), lambda i: (0, 0))
    tile = lambda w: pl.BlockSpec((tm, w), lambda i: (i, 0))
    widths = (W_DIL, W_DIL, W_DIL, W_MOBA, W_MOBA, W_MOBA, W_MEM)
    return pl.pallas_call(
        _inproj_kernel,
        grid=(t // tm,),
        in_specs=[tile(D_MODEL), row(D_MODEL), pl.BlockSpec((D_MODEL, IN_WIDTH), lambda i: (0, 0)),
                  row(W_DIL), row(W_DIL), row(W_MOBA), row(W_MOBA), row(W_MEM)],
        out_specs=[tile(w) for w in widths],
        out_shape=[jax.ShapeDtypeStruct((t, w), F32) for w in widths],
        compiler_params=_cparams(("parallel",)),
        name="inproj",
    )(x2, g_mix, w_in, gqd, gkd, gqb, gkb, gqm)


DIL_PAD = BAND_BLOCK * max(d for _, d in DIL_CONFIGS)
DIL_MIX_ROWS = 256
DIL_UNROLL = 8


def _dil_kernel(q_ref, k_ref, v_ref, ns_ref, o_ref, kpad, vpad, ob0, ob1, ob2, lb0, lb1, lb2, bias_ref):
    s_len = q_ref.shape[0]
    zeros = jnp.zeros((DIL_PAD, LANES), F32)
    kpad[0:DIL_PAD, :] = zeros
    vpad[0:DIL_PAD, :] = zeros
    kpad[DIL_PAD:, :] = k_ref[...]
    vpad[DIL_PAD:, :] = v_ref[...]

    blk = BAND_BLOCK
    head0 = lax.broadcasted_iota(I32, (blk, LANES), 1) < HEAD_DIM
    ql = lax.broadcasted_iota(I32, (blk, 2 * blk), 0)
    kl = lax.broadcasted_iota(I32, (blk, 2 * blk), 1)
    delta = blk + ql - kl
    obs = (ob0, ob1, ob2)
    lbs = (lb0, lb1, lb2)

    for c, (window, dil) in enumerate(DIL_CONFIGS):
        reach = window // dil
        n_blk = s_len // dil // blk
        in_band = (delta >= 0) & (delta <= reach)
        dist = (delta * dil).astype(F32)
        ob, lb = obs[c], lbs[c]
        for h in range(2):
            bias = ns_ref[h] * dist
            bias_ref[h, 0] = jnp.where(in_band & (kl >= blk), bias, NEG_INF)
            bias_ref[h, 1] = jnp.where(in_band, bias, NEG_INF)

        def rows(start, size, dil=dil):
            return pl.ds(start, size) if dil == 1 else pl.ds(start, size, stride=dil)

        def block_body(n, r, dil=dil, ob=ob, lb=lb, rows=rows):
            q_start = r + dil * blk * n
            k_start = DIL_PAD + r + dil * blk * (n - 1)
            qv = q_ref[rows(q_start, blk), :] * SCALE
            kv = kpad[rows(k_start, 2 * blk), :].astype(BF16)
            vv = vpad[rows(k_start, 2 * blk), :].astype(BF16)
            variant = jnp.minimum(n, 1)
            outs, lses = [], []
            for h in range(2):
                qh = jnp.where(head0 if h == 0 else ~head0, qv, 0.0).astype(BF16)
                s = lax.dot_general(qh, kv, _NT, preferred_element_type=F32) + bias_ref[h, variant]
                m = jnp.max(s, axis=-1, keepdims=True)
                p = jnp.exp(s - m)
                den = jnp.sum(p, axis=-1, keepdims=True)
                lses.append(jnp.broadcast_to(m + jnp.log(den), (blk, LANES)))
                outs.append(jnp.dot(p.astype(BF16), vv, preferred_element_type=F32) / den)
            ob[rows(q_start, blk), :] = jnp.where(head0, outs[0], outs[1])
            lb[rows(q_start, blk), :] = jnp.where(head0, lses[0], lses[1])

        def flat_body(idx, carry, n_blk=n_blk, block_body=block_body):
            block_body(idx & (n_blk - 1), _shr_scalar(idx, n_blk.bit_length() - 1))
            return carry

        lax.fori_loop(0, dil * n_blk, flat_body, 0, unroll=DIL_UNROLL)

    def mix(i, carry):
        sl = pl.ds(pl.multiple_of(i * DIL_MIX_ROWS, DIL_MIX_ROWS), DIL_MIX_ROWS)
        l0, l1, l2 = lb0[sl, :], lb1[sl, :], lb2[sl, :]
        mx = jnp.maximum(jnp.maximum(l0, l1), l2)
        e0, e1, e2 = jnp.exp(l0 - mx), jnp.exp(l1 - mx), jnp.exp(l2 - mx)
        tot = e0 + e1 + e2
        o_ref[sl, :] = (e0 / tot) * ob0[sl, :] + (e1 / tot) * ob1[sl, :] + (e2 / tot) * ob2[sl, :]
        return carry

    lax.fori_loop(0, s_len // DIL_MIX_ROWS, mix, 0)


def _dilated(q, k, v, neg_slopes):
    b, s_len, w = q.shape
    n_pair = w // LANES
    slab = pl.BlockSpec((None, s_len, LANES), lambda i, p: (i, 0, p))
    buf = lambda rows: pltpu.VMEM((rows, LANES), F32)
    return pl.pallas_call(
        _dil_kernel,
        grid=(b, n_pair),
        in_specs=[slab, slab, slab,
                  pl.BlockSpec((None, 2, 1, 2 * BAND_BLOCK), lambda i, p: (p, 0, 0, 0))],
        out_specs=slab,
        out_shape=jax.ShapeDtypeStruct((b, s_len, w), F32),
        scratch_shapes=([buf(DIL_PAD + s_len), buf(DIL_PAD + s_len)] + [buf(s_len)] * 6
                        + [pltpu.VMEM((2, 2, BAND_BLOCK, 2 * BAND_BLOCK), F32)]),
        compiler_params=_cparams(("parallel", "parallel")),
        name="dilated",
    )(q, k, v, neg_slopes)


def _split_bf16(a):
    hi = a.astype(BF16)
    lo = (a - hi.astype(F32)).astype(BF16)
    return hi, lo


MOBA_QB = 16
MOBA_KB = 19
MOBA_GROUPS = (8, 4, 2, 1)


def _split3_masked(a):
    def keep_high_bits(x):
        bits = lax.bitcast_convert_type(x, U32) & jnp.uint32(0xFFFF0000)
        return lax.bitcast_convert_type(bits, F32)

    t0 = keep_high_bits(a)
    t1 = keep_high_bits(a - t0)
    t2 = keep_high_bits(a - t0 - t1)
    return t0, t1, t2


def _moba_bias_lanes(slopes, s_len):
    t_pos = jnp.arange(s_len, dtype=F32)
    lane = jnp.arange(LANES)
    neg = (-slopes).reshape(-1, 2, 1) * t_pos
    q_terms = _split3_masked(neg)
    k_terms = _split3_masked(-neg)
    shape = neg.shape + (LANES,)
    qx = jnp.zeros(shape, F32)
    kx = jnp.broadcast_to(jnp.where(lane == (t_pos[:, None] // MOBA_BLOCK), NEG_INF, 0.0), shape)
    for d in range(3):
        qx = jnp.where(lane == MOBA_QB + d, q_terms[d][..., None], jnp.where(lane == MOBA_KB + d, 1.0, qx))
        kx = jnp.where(lane == MOBA_QB + d, 1.0, jnp.where(lane == MOBA_KB + d, k_terms[d][..., None], kx))
    return qx.astype(BF16), kx.astype(BF16)


def _moba_kernel(q_ref, k_ref, v_ref, qx_ref, kx_ref, o_ref, qa_ref, kb_ref, vt_ref, m_ref, acc_ref):
    s_len = q_ref.shape[0]
    blk = MOBA_BLOCK
    n_blk = s_len // blk
    lane = lax.broadcasted_iota(I32, (blk, LANES), 1)
    head0 = lane < HEAD_DIM

    kb_ref[...] = k_ref[...].astype(BF16)

    k_mean = jnp.concatenate(
        [jnp.sum(k_ref[j * blk:(j + 1) * blk, :], axis=0, keepdims=True) for j in range(n_blk)],
        axis=0) * (1.0 / blk)
    q_hi, q_lo = _split_bf16(q_ref[...])

    blk_id = lax.broadcasted_iota(I32, (n_blk, s_len), 0)
    n_past = _shr(lax.broadcasted_iota(I32, (n_blk, s_len), 1), blk.bit_length() - 1)
    past = blk_id < n_past
    blk_f = blk_id.astype(F32)
    pad_rows = LANES - n_blk

    for h in range(2):
        hm = (lax.broadcasted_iota(I32, (n_blk, LANES), 1) < HEAD_DIM) == (h == 0)
        km_hi, km_lo = _split_bf16(jnp.where(hm, k_mean, 0.0))
        gate = (lax.dot_general(km_hi, q_hi, _NT, preferred_element_type=F32)
                + lax.dot_general(km_hi, q_lo, _NT, preferred_element_type=F32)
                + lax.dot_general(km_lo, q_hi, _NT, preferred_element_type=F32))
        g = jnp.where(past, gate, NEG_INF)
        sel = jnp.zeros((n_blk, s_len), F32)
        for _ in range(MOBA_TOPK):
            m = jnp.max(g, axis=0, keepdims=True)
            first = jnp.min(jnp.where(g == m, blk_f, float(n_blk)), axis=0, keepdims=True)
            pick = blk_f == first
            sel = jnp.where(pick, 1.0, sel)
            g = jnp.where(pick, -jnp.inf, g)
        not_sel = jnp.where(past, 1.0 - sel, jnp.where(blk_id == n_past, 0.0, 1.0))
        not_sel = jnp.concatenate([not_sel, jnp.zeros((pad_rows, s_len), F32)], axis=0)
        for i in range(n_blk):
            rows = slice(i * blk, (i + 1) * blk)
            qh = jnp.where(head0 if h == 0 else ~head0, q_ref[rows, :], 0.0) * SCALE
            qa_ref[h, rows, 0:LANES] = qh.astype(BF16)
            qa_ref[h, rows, LANES:] = not_sel[:, rows].T.astype(BF16) + qx_ref[h, rows, :]

    top_rows = lax.broadcasted_iota(I32, (LANES, blk), 0) < HEAD_DIM
    for j in range(n_blk):
        vt = v_ref[j * blk:(j + 1) * blk, :].T
        vt_ref[0, :, j * blk:(j + 1) * blk] = jnp.where(top_rows, vt, 1.0).astype(BF16)
        vt_ref[1, :, j * blk:(j + 1) * blk] = jnp.where(top_rows, 1.0, vt).astype(BF16)

    key_i = lax.broadcasted_iota(I32, (blk, blk), 0)
    qry_i = lax.broadcasted_iota(I32, (blk, blk), 1)
    causal = key_i <= qry_i

    def q_aug(i, h, tiles=1):
        rows = pl.ds(pl.multiple_of(i * blk, blk), tiles * blk)
        return jnp.concatenate([qa_ref[h, rows, 0:LANES], qa_ref[h, rows, LANES:]], axis=1)

    def q_cols(i, tiles=1):
        return pl.ds(pl.multiple_of(i * blk, blk), tiles * blk)

    def k_aug(j, h):
        c0 = pl.multiple_of(j * blk, blk)
        return jnp.concatenate([kb_ref[pl.ds(c0, blk), :], kx_ref[h, pl.ds(c0, blk), :]], axis=1)

    def v_t(j, h):
        return vt_ref[h, :, pl.ds(pl.multiple_of(j * blk, blk), blk)]

    def own_block(i, carry):
        for h in range(2):
            s = lax.dot_general(k_aug(i, h), q_aug(i, h), _NT, preferred_element_type=F32)
            s = jnp.where(causal, s, NEG_INF)
            m0 = jnp.max(s, axis=0, keepdims=True)
            m_ref[h, :, q_cols(i)] = m0
            acc_ref[h, :, q_cols(i)] = jnp.dot(v_t(i, h), jnp.exp(s - m0).astype(BF16),
                                               preferred_element_type=F32)
        return carry

    lax.fori_loop(0, n_blk, own_block, 0, unroll=2)

    def key_block(j, carry):
        ks = [k_aug(j, h) for h in range(2)]
        vs = [v_t(j, h) for h in range(2)]

        def update(i, count):
            cols = q_cols(i, count)
            for h in range(2):
                sj = lax.dot_general(ks[h], q_aug(i, h, count), _NT, preferred_element_type=F32)
                m = m_ref[h, :, cols]
                m_new = jnp.maximum(m, jnp.max(sj, axis=0, keepdims=True))
                acc_ref[h, :, cols] = jnp.exp(m - m_new) * acc_ref[h, :, cols] + jnp.dot(
                    vs[h], jnp.exp(sj - m_new).astype(BF16), preferred_element_type=F32)
                m_ref[h, :, cols] = m_new

        def tiles(first, count):
            def body(g, c2):
                update(first + count * g, count)
                return c2
            return body

        first = j + 1
        for size in MOBA_GROUPS:
            count = (n_blk - first) // size
            lax.fori_loop(0, count, tiles(first, size), 0)
            first = first + size * count
        return carry

    lax.fori_loop(0, n_blk - 1, key_block, 0)

    def finish(i, carry):
        a0, a1 = acc_ref[0, :, q_cols(i)], acc_ref[1, :, q_cols(i)]
        out_t = jnp.where(top_rows, a0 / a0[HEAD_DIM:HEAD_DIM + 1, :], a1 / a1[0:1, :])
        o_ref[pl.ds(pl.multiple_of(i * blk, blk), blk), :] = out_t.T
        return carry

    lax.fori_loop(0, n_blk, finish, 0, unroll=2)


def _moba(q, k, v, slopes):
    b, s_len, w = q.shape
    n_pair = w // LANES
    slab = pl.BlockSpec((None, s_len, LANES), lambda i, p: (i, 0, p))
    table = pl.BlockSpec((None, 2, s_len, LANES), lambda i, p: (p, 0, 0, 0))
    qx, kx = _moba_bias_lanes(slopes, s_len)
    return pl.pallas_call(
        _moba_kernel,
        grid=(b, n_pair),
        in_specs=[slab, slab, slab, table, table],
        out_specs=slab,
        out_shape=jax.ShapeDtypeStruct((b, s_len, w), F32),
        scratch_shapes=[pltpu.VMEM((2, s_len, 2 * LANES), BF16),
                        pltpu.VMEM((s_len, LANES), BF16),
                        pltpu.VMEM((2, LANES, s_len), BF16),
                        pltpu.VMEM((2, 1, s_len), F32),
                        pltpu.VMEM((2, LANES, s_len), F32)],
        compiler_params=_cparams(("parallel", "parallel")),
        name="moba",
    )(q, k, v, qx, kx)


MEMATTN_TQ = 512


def _memattn_kernel(q_ref, k_ref, v_ref, o_ref):
    q = q_ref[...]
    kb = k_ref[...].astype(BF16)
    vb = v_ref[...].astype(BF16)
    head = _shr(lax.broadcasted_iota(I32, q.shape, 1), HEAD_DIM.bit_length() - 1)
    out = jnp.zeros(q.shape, F32)
    for h in range(N_HEADS_MEM):
        qh = jnp.where(head == h, q, 0.0).astype(BF16)
        s = lax.dot_general(qh, kb, _NT, preferred_element_type=F32) * SCALE
        m = jnp.max(s, axis=-1, keepdims=True)
        p = jnp.exp(s - m)
        den = jnp.sum(p, axis=-1, keepdims=True)
        oh = jnp.dot(p.astype(BF16), vb, preferred_element_type=F32) / den
        out = jnp.where(head == h, oh, out)
    o_ref[...] = out


def _memattn(q, k, v):
    b, s_len, w = q.shape
    tq = MEMATTN_TQ
    return pl.pallas_call(
        _memattn_kernel,
        grid=(b, s_len // tq),
        in_specs=[pl.BlockSpec((None, tq, w), lambda i, j: (i, j, 0)),
                  pl.BlockSpec((None, MEM_LEN, w), lambda i, j: (i, 0, 0)),
                  pl.BlockSpec((None, MEM_LEN, w), lambda i, j: (i, 0, 0))],
        out_specs=pl.BlockSpec((None, tq, w), lambda i, j: (i, j, 0)),
        out_shape=jax.ShapeDtypeStruct((b, s_len, w), F32),
        compiler_params=_cparams(("parallel", "parallel")),
        name="memattn",
    )(q, k, v)


OUTPROJ_TM = 512


def _outproj_kernel(od_ref, ob_ref, om_ref, x_ref, gd, gb, gm, wo_ref, gf, wq_ref, k1_ref, k2_ref,
                    x1_ref, hn_ref, s1_ref, s2_ref):
    y = jnp.dot(_row_rms(od_ref[...], gd[...]).astype(BF16), wo_ref[0:W_DIL, :],
                preferred_element_type=F32)
    y += jnp.dot(_row_rms(ob_ref[...], gb[...]).astype(BF16), wo_ref[W_DIL:W_DIL + W_MOBA, :],
                 preferred_element_type=F32)
    y += jnp.dot(_row_rms(om_ref[...], gm[...]).astype(BF16), wo_ref[W_DIL + W_MOBA:, :],
                 preferred_element_type=F32)
    x1 = x_ref[...] + y
    x1_ref[...] = x1
    hb = _row_rms(x1, gf[...]).astype(BF16)
    hn_ref[...] = hb
    half = PEER_DKEY // 2
    qry = jnp.dot(hb, wq_ref[...], preferred_element_type=F32).astype(BF16)
    for h in range(PEER_HEADS):
        qh = qry[:, h * PEER_DKEY:(h + 1) * PEER_DKEY]
        s1_ref[h] = lax.dot_general(k1_ref[h], qh[:, :half], _NT, preferred_element_type=F32)
        s2_ref[h] = lax.dot_general(k2_ref[h], qh[:, half:], _NT, preferred_element_type=F32)


def _outproj(o_dil, o_moba, o_mem, x2, og_dil, og_moba, og_mem, w_out, g_ffn, w_q, sub1, sub2):
    t = x2.shape[0]
    tm = OUTPROJ_TM
    row = lambda w: pl.BlockSpec((1, w), lambda i: (0, 0))
    tile = lambda w: pl.BlockSpec((tm, w), lambda i: (i, 0))
    full = lambda shape: pl.BlockSpec(shape, lambda i: (0,) * len(shape))
    score = pl.BlockSpec((PEER_HEADS, PEER_NKEYS, tm), lambda i: (0, 0, i))
    half = PEER_DKEY // 2
    return pl.pallas_call(
        _outproj_kernel,
        grid=(t // tm,),
        in_specs=[tile(W_DIL), tile(W_MOBA), tile(W_MEM), tile(D_MODEL),
                  row(W_DIL), row(W_MOBA), row(W_MEM), full((D_MODEL, D_MODEL)), row(D_MODEL),
                  full((D_MODEL, PEER_HEADS * PEER_DKEY)),
                  full((PEER_HEADS, PEER_NKEYS, half)), full((PEER_HEADS, PEER_NKEYS, half))],
        out_specs=[tile(D_MODEL), tile(D_MODEL), score, score],
        out_shape=[jax.ShapeDtypeStruct((t, D_MODEL), F32),
                   jax.ShapeDtypeStruct((t, D_MODEL), BF16),
                   jax.ShapeDtypeStruct((PEER_HEADS, PEER_NKEYS, t), F32),
                   jax.ShapeDtypeStruct((PEER_HEADS, PEER_NKEYS, t), F32)],
        compiler_params=_cparams(("parallel",)),
        name="outproj",
    )(o_dil, o_moba, o_mem, x2, og_dil, og_moba, og_mem, w_out, g_ffn, w_q, sub1, sub2)


TOPK_TM = 512
TOPK_COLS_PER_ITER = 8
_ID_LIMIT = 1024.0


def _top_rows(s, k, ids=None, rows_at=None):
    n_rows = s.shape[0]
    if ids is None:
        ids = lax.broadcasted_iota(I32, s.shape, 0).astype(F32)
    vals, idxs = [], []
    for t in range(k):
        r = n_rows if rows_at is None else rows_at[t]
        head, head_ids = s[:r, :], ids[:r, :]
        m = jnp.max(head, axis=0, keepdims=True)
        first = jnp.min(jnp.where(head == m, head_ids, _ID_LIMIT), axis=0, keepdims=True)
        vals.append(m)
        idxs.append(first)
        head = jnp.where(head_ids == first, -jnp.inf, head)
        s = head if r == n_rows else jnp.concatenate([head, s[r:, :]], axis=0)
    return jnp.concatenate(vals, axis=0), jnp.concatenate(idxs, axis=0)


def _pick_rows(table, sel):
    out = jnp.zeros(sel.shape, table.dtype)
    for j in range(table.shape[0]):
        out = jnp.where(sel == j, table[j:j + 1, :], out)
    return out


def _candidates(v1, v2):
    n = v1.shape[1]
    sub = lax.broadcasted_iota(I32, (8, n), 0)
    sub_f = sub.astype(F32)
    sums, ids = [], []
    for j1 in range(8):
        both = v1[j1:j1 + 1, :] + v2[0:8, :]
        sums.append(both if j1 == 0 else jnp.where(sub < PEER_TOPK // (j1 + 1), both, -jnp.inf))
        ids.append(sub_f + float(j1 * PEER_TOPK))
    sums += [v1[0:1, :] + v2[8:16, :], v1[8:16, :] + v2[0:1, :]]
    ids += [sub_f + 8.0, sub_f * float(PEER_TOPK) + float(8 * PEER_TOPK)]
    rows_at = [8 * min(t, 8) if t <= 8 else 80 for t in range(1, PEER_TOPK + 1)]
    return jnp.concatenate(sums, axis=0), jnp.concatenate(ids, axis=0), rows_at


def _retrieve(s1, s2):
    v1, i1 = _top_rows(s1, PEER_TOPK)
    v2, i2 = _top_rows(s2, PEER_TOPK)
    sums, ids, rows_at = _candidates(v1, v2)
    top_s, pair = _top_rows(sums, PEER_TOPK, ids, rows_at)
    e = jnp.exp(top_s - top_s[0:1, :])
    pair = pair.astype(I32)
    bits = PEER_TOPK.bit_length() - 1
    return (_pick_rows(i1, _shr(pair, bits)), _pick_rows(i2, pair & (PEER_TOPK - 1)),
            e / jnp.sum(e, axis=0, keepdims=True))


def _peertopk_kernel(s1_ref, s2_ref, i1_ref, i2_ref, g_ref):
    n_col = s1_ref.shape[2] // LANES

    def body(i, carry):
        for c in range(TOPK_COLS_PER_ITER):
            item = i * TOPK_COLS_PER_ITER + c
            h = _shr_scalar(item, n_col.bit_length() - 1)
            cols = pl.ds(pl.multiple_of((item & (n_col - 1)) * LANES, LANES), LANES)
            i1_ref[h, :, cols], i2_ref[h, :, cols], g_ref[h, :, cols] = _retrieve(
                s1_ref[h, :, cols], s2_ref[h, :, cols])
        return carry

    lax.fori_loop(0, PEER_HEADS * n_col // TOPK_COLS_PER_ITER, body, 0)


def _peertopk(s1, s2):
    t = s1.shape[2]
    tm = TOPK_TM
    score = pl.BlockSpec((PEER_HEADS, PEER_NKEYS, tm), lambda i: (0, 0, i))
    slot = pl.BlockSpec((PEER_HEADS, PEER_TOPK, tm), lambda i: (0, 0, i))
    return pl.pallas_call(
        _peertopk_kernel,
        grid=(t // tm,),
        in_specs=[score, score],
        out_specs=[slot, slot, slot],
        out_shape=[jax.ShapeDtypeStruct((PEER_HEADS, PEER_TOPK, t), F32)] * 3,
        compiler_params=_cparams(("parallel",)),
        name="peertopk",
    )(s1, s2)


PEER_TM = 512
PEER_TE = 2048
PEER_KEY_PAIRS = PEER_NKEYS // 2
PEER_PITCH = PEER_TM + 8
PEER_BUILD_UNROLL = 64
BF16_ROWS = 16


def _rows_bf16(row, n_rows):
    tile = jnp.broadcast_to(row, (BF16_ROWS, row.shape[1])).astype(BF16)
    return jnp.concatenate([tile] * (n_rows // BF16_ROWS), axis=0)


def _peerffn_kernel(h_ref, x1_ref, i1_ref, i2_ref, g_ref, u_ref, v_ref, o_ref, gate_ref):
    c = pl.program_id(1)
    slabs = PEER_TE // PEER_NKEYS

    @pl.when(c == 0)
    def _build_gate_matrix():
        key_id = lax.broadcasted_iota(I32, (PEER_NKEYS, PEER_SLOTS), 0).astype(BF16)
        one = jnp.ones((PEER_NKEYS, PEER_SLOTS), BF16)
        zero = jnp.zeros((PEER_NKEYS, PEER_SLOTS), BF16)

        def token(t, carry):
            i1 = _rows_bf16(i1_ref[pl.ds(t, 1), :], PEER_NKEYS)
            i2 = _rows_bf16(i2_ref[pl.ds(t, 1), :], PEER_NKEYS)
            g = _rows_bf16(g_ref[pl.ds(t, 1), :], PEER_NKEYS)
            lhs = jnp.where(key_id == i1, g, zero)
            rhs = jnp.where(key_id == i2, one, zero)
            gt = lax.dot_general(lhs, rhs, _NT, preferred_element_type=F32)
            gate_ref[pl.ds(t, PEER_KEY_PAIRS, stride=PEER_PITCH), :] = pltpu.bitcast(gt.astype(BF16), U32)
            return carry

        lax.fori_loop(0, PEER_TM, token, 0, unroll=PEER_BUILD_UNROLL)
        o_ref[...] = x1_ref[...]

    a = lax.dot_general(h_ref[...], u_ref[...], _NT, preferred_element_type=F32)
    ws = []
    for r in range(slabs // 2):
        start = pl.multiple_of((c * (slabs // 2) + r) * PEER_PITCH, 8)
        word = gate_ref[pl.ds(start, PEER_TM), :]
        g_even = lax.bitcast_convert_type(lax.shift_left(word, jnp.uint32(16)), F32)
        g_odd = lax.bitcast_convert_type(word & jnp.uint32(0xFFFF0000), F32)
        for j, gj in ((2 * r, g_even), (2 * r + 1, g_odd)):
            aj = a[:, j * PEER_NKEYS:(j + 1) * PEER_NKEYS]
            ws.append((0.5 * aj * (1.0 + lax.erf(aj * SQRT_HALF)) * gj).astype(BF16))
    o_ref[...] += jnp.dot(jnp.concatenate(ws, axis=1), v_ref[...], preferred_element_type=F32)


def _peerffn(hn, x1, i1, i2, gate, u, v):
    t = hn.shape[0]
    tm, te = PEER_TM, PEER_TE
    tile = lambda w: pl.BlockSpec((tm, w), lambda i, c: (i, 0))
    chunk = pl.BlockSpec((te, D_MODEL), lambda i, c: (c, 0))
    return pl.pallas_call(
        _peerffn_kernel,
        grid=(t // tm, PEER_EXPERTS // te),
        in_specs=[tile(D_MODEL), tile(D_MODEL), tile(PEER_SLOTS), tile(PEER_SLOTS), tile(PEER_SLOTS),
                  chunk, chunk],
        out_specs=tile(D_MODEL),
        out_shape=jax.ShapeDtypeStruct((t, D_MODEL), F32),
        scratch_shapes=[pltpu.VMEM((PEER_KEY_PAIRS * PEER_PITCH, LANES), U32)],
        compiler_params=_cparams(("parallel", "arbitrary")),
        name="peerffn",
    )(hn, x1, i1, i2, gate, u, v)


def _neg_slope_rows(slopes, width):
    n = slopes.shape[0]
    return jnp.broadcast_to((-slopes).reshape(n // 2, 2, 1, 1), (n // 2, 2, 1, width))


def _tiled_gain(g, reps):
    return jnp.tile(g, reps)[None, :]


def _layer(x, mem, g_mix, w_in, qg_dil, kg_dil, qg_moba, kg_moba, qg_mem, kg_mem, g_memtok, w_mem_kv,
           og_dil, og_moba, og_mem, w_out, g_ffn, w_peer_q, sub1, sub2, peer_u, peer_v):
    b, s_len, d = x.shape
    t = b * s_len
    n_mix = N_HEADS_DIL + N_HEADS_MOBA
    slopes = jnp.exp2(-8.0 * jnp.arange(1, n_mix + 1, dtype=F32) / n_mix)
    ns_dil = _neg_slope_rows(slopes[0::2], 2 * BAND_BLOCK)

    x2 = x.reshape(t, d)
    k_m, v_m = _memkv(mem, g_memtok[None, :], w_mem_kv.astype(BF16), _tiled_gain(kg_mem, N_HEADS_MEM))
    q_d, k_d, v_d, q_b, k_b, v_b, q_m = _inproj(
        x2, g_mix[None, :], w_in.astype(BF16),
        _tiled_gain(qg_dil, N_HEADS_DIL), _tiled_gain(kg_dil, N_HEADS_DIL),
        _tiled_gain(qg_moba, N_HEADS_MOBA), _tiled_gain(kg_moba, N_HEADS_MOBA),
        _tiled_gain(qg_mem, N_HEADS_MEM))
    seq = lambda a: a.reshape(b, s_len, a.shape[-1])
    o_dil = _dilated(seq(q_d), seq(k_d), seq(v_d), ns_dil)
    o_moba = _moba(seq(q_b), seq(k_b), seq(v_b), slopes[1::2])
    o_mem = _memattn(seq(q_m), k_m, v_m)
    x1, hn, s1, s2 = _outproj(
        o_dil.reshape(t, W_DIL), o_moba.reshape(t, W_MOBA), o_mem.reshape(t, W_MEM), x2,
        og_dil[None, :], og_moba[None, :], og_mem[None, :], w_out.astype(BF16), g_ffn[None, :],
        w_peer_q.astype(BF16), sub1.astype(BF16), sub2.astype(BF16))
    i1, i2, gate = _peertopk(s1, s2)
    slots = lambda a: a.reshape(PEER_SLOTS, t).T
    out = _peerffn(hn, x1, slots(i1), slots(i2), slots(gate), peer_u.astype(BF16), peer_v.astype(BF16))
    return out.reshape(b, s_len, d)


def kernel(x, mem, g_mix, w_in, qg_dil, kg_dil, qg_moba, kg_moba, qg_mem, kg_mem, g_memtok, w_mem_kv,
           og_dil, og_moba, og_mem, w_out, g_ffn, w_peer_q, peer_subkeys_1, peer_subkeys_2, peer_u,
           peer_v):
    h = x
    for layer in range(g_mix.shape[0]):
        h = _layer(h, mem, g_mix[layer], w_in[layer], qg_dil[layer], kg_dil[layer], qg_moba[layer],
                   kg_moba[layer], qg_mem[layer], kg_mem[layer], g_memtok[layer], w_mem_kv[layer],
                   og_dil[layer], og_moba[layer], og_mem[layer], w_out[layer], g_ffn[layer],
                   w_peer_q[layer], peer_subkeys_1[layer], peer_subkeys_2[layer], peer_u[layer],
                   peer_v[layer])
    return h
```

```python
import math

import jax
import jax.numpy as jnp
from jax import lax
from jax.experimental import pallas as pl
from jax.experimental.pallas import tpu as pltpu

F32 = jnp.float32
BF16 = jnp.bfloat16
I32 = jnp.int32
U32 = jnp.uint32

LANES = 128
D_MODEL = 1024
N_HEADS_DIL = 6
N_HEADS_MOBA = 6
N_HEADS_MEM = 4
HEAD_DIM = 64
W_DIL = N_HEADS_DIL * HEAD_DIM
W_MOBA = N_HEADS_MOBA * HEAD_DIM
W_MEM = N_HEADS_MEM * HEAD_DIM
IN_WIDTH = 3 * W_DIL + 3 * W_MOBA + W_MEM
DIL_CONFIGS = ((128, 1), (512, 4), (2048, 16))
BAND_BLOCK = 128
MOBA_BLOCK = 256
MOBA_TOPK = 3
MEM_LEN = 256
PEER_HEADS = 8
PEER_NKEYS = 128
PEER_EXPERTS = PEER_NKEYS * PEER_NKEYS
PEER_TOPK = 16
PEER_DKEY = 256
PEER_SLOTS = PEER_HEADS * PEER_TOPK
RMS_EPS = 1e-6
NEG_INF = -1e30
SCALE = 1.0 / math.sqrt(HEAD_DIM)
SQRT_HALF = math.sqrt(0.5)

VMEM_LIMIT = 56 * 1024 * 1024

_NT = (((1,), (1,)), ((), ()))


def _shr(a, bits):
    return lax.shift_right_logical(a, jnp.full(a.shape, bits, a.dtype))


def _shr_scalar(a, bits):
    return lax.shift_right_logical(a, jnp.int32(bits))


def _cparams(sem):
    return pltpu.CompilerParams(dimension_semantics=sem, vmem_limit_bytes=VMEM_LIMIT)


def _row_rms(a, gain):
    return a * lax.rsqrt(jnp.mean(a * a, axis=-1, keepdims=True) + RMS_EPS) * gain


def _group_mean_sq(p, group):
    w = p.shape[-1]
    bits = group.bit_length() - 1
    gi = _shr(lax.broadcasted_iota(I32, (w, w), 0), bits)
    gj = _shr(lax.broadcasted_iota(I32, (w, w), 1), bits)
    ones_bd = jnp.where(gi == gj, 1.0, 0.0).astype(BF16)
    p2 = p * p
    hi = p2.astype(BF16)
    lo = (p2 - hi.astype(F32)).astype(BF16)
    ss = (jnp.dot(hi, ones_bd, preferred_element_type=F32)
          + jnp.dot(lo, ones_bd, preferred_element_type=F32))
    return ss * (1.0 / group)


def _head_rms(p, gain):
    return p * lax.rsqrt(_group_mean_sq(p, HEAD_DIM) + RMS_EPS) * gain


def _memkv_kernel(mem_ref, g_ref, w_ref, kg_ref, k_ref, v_ref):
    hn = _row_rms(mem_ref[...], g_ref[...]).astype(BF16)
    kv = jnp.dot(hn, w_ref[...], preferred_element_type=F32)
    k_ref[...] = _head_rms(kv[:, :W_MEM], kg_ref[...])
    v_ref[...] = kv[:, W_MEM:]


def _memkv(mem, g_memtok, w_kv, kg_mem):
    b = mem.shape[0]
    return pl.pallas_call(
        _memkv_kernel,
        grid=(b,),
        in_specs=[
            pl.BlockSpec((None, MEM_LEN, D_MODEL), lambda i: (i, 0, 0)),
            pl.BlockSpec((1, D_MODEL), lambda i: (0, 0)),
            pl.BlockSpec((D_MODEL, 2 * W_MEM), lambda i: (0, 0)),
            pl.BlockSpec((1, W_MEM), lambda i: (0, 0)),
        ],
        out_specs=[pl.BlockSpec((None, MEM_LEN, W_MEM), lambda i: (i, 0, 0))] * 2,
        out_shape=[jax.ShapeDtypeStruct((b, MEM_LEN, W_MEM), F32)] * 2,
        compiler_params=_cparams(("parallel",)),
        name="memkv",
    )(mem, g_memtok, w_kv, kg_mem)


INPROJ_TM = 512


def _inproj_kernel(x_ref, g_ref, w_ref, gqd, gkd, gqb, gkb, gqm, qd, kd, vd, qb, kb, vb, qm):
    hb = _row_rms(x_ref[...], g_ref[...]).astype(BF16)
    proj = jnp.dot(hb, w_ref[...], preferred_element_type=F32)

    def seg(lo, width):
        return proj[:, lo:lo + width]

    qd[...] = _head_rms(seg(0, W_DIL), gqd[...])
    kd[...] = _head_rms(seg(W_DIL, W_DIL), gkd[...])
    vd[...] = seg(2 * W_DIL, W_DIL)
    base = 3 * W_DIL
    qb[...] = _head_rms(seg(base, W_MOBA), gqb[...])
    kb[...] = _head_rms(seg(base + W_MOBA, W_MOBA), gkb[...])
    vb[...] = seg(base + 2 * W_MOBA, W_MOBA)
    qm[...] = _head_rms(seg(base + 3 * W_MOBA, W_MEM), gqm[...])


def _inproj(x2, g_mix, w_in, gqd, gkd, gqb, gkb, gqm):
    t = x2.shape[0]
    tm = INPROJ_TM
    row = lambda w: pl.BlockSpec((1, w), lambda i: (0, 0))
    tile = lambda w: pl.BlockSpec((tm, w), lambda i: (i, 0))
    widths = (W_DIL, W_DIL, W_DIL, W_MOBA, W_MOBA, W_MOBA, W_MEM)
    return pl.pallas_call(
        _inproj_kernel,
        grid=(t // tm,),
        in_specs=[tile(D_MODEL), row(D_MODEL), pl.BlockSpec((D_MODEL, IN_WIDTH), lambda i: (0, 0)),
                  row(W_DIL), row(W_DIL), row(W_MOBA), row(W_MOBA), row(W_MEM)],
        out_specs=[tile(w) for w in widths],
        out_shape=[jax.ShapeDtypeStruct((t, w), F32) for w in widths],
        compiler_params=_cparams(("parallel",)),
        name="inproj",
    )(x2, g_mix, w_in, gqd, gkd, gqb, gkb, gqm)


DIL_PAD = BAND_BLOCK * max(d for _, d in DIL_CONFIGS)
DIL_MIX_ROWS = 256
DIL_UNROLL = 16


def _dil_kernel(q_ref, k_ref, v_ref, ns_ref, o_ref, kpad, vpad, ob0, ob1, ob2, lb0, lb1, lb2, bias_ref):
    s_len = q_ref.shape[0]
    zeros = jnp.zeros((DIL_PAD, LANES), F32)
    kpad[0:DIL_PAD, :] = zeros
    vpad[0:DIL_PAD, :] = zeros
    kpad[DIL_PAD:, :] = k_ref[...]
    vpad[DIL_PAD:, :] = v_ref[...]

    blk = BAND_BLOCK
    head0 = lax.broadcasted_iota(I32, (blk, LANES), 1) < HEAD_DIM
    ql = lax.broadcasted_iota(I32, (blk, 2 * blk), 0)
    kl = lax.broadcasted_iota(I32, (blk, 2 * blk), 1)
    delta = blk + ql - kl
    obs = (ob0, ob1, ob2)
    lbs = (lb0, lb1, lb2)

    for c, (window, dil) in enumerate(DIL_CONFIGS):
        reach = window // dil
        n_blk = s_len // dil // blk
        in_band = (delta >= 0) & (delta <= reach)
        dist = (delta * dil).astype(F32)
        ob, lb = obs[c], lbs[c]
        for h in range(2):
            bias = ns_ref[h] * dist
            bias_ref[h, 0] = jnp.where(in_band & (kl >= blk), bias, NEG_INF)
            bias_ref[h, 1] = jnp.where(in_band, bias, NEG_INF)

        def rows(start, size, dil=dil):
            return pl.ds(start, size) if dil == 1 else pl.ds(start, size, stride=dil)

        def block_body(n, r, dil=dil, ob=ob, lb=lb, rows=rows):
            q_start = r + dil * blk * n
            k_start = DIL_PAD + r + dil * blk * (n - 1)
            qv = q_ref[rows(q_start, blk), :] * SCALE
            kv = kpad[rows(k_start, 2 * blk), :].astype(BF16)
            vv = vpad[rows(k_start, 2 * blk), :].astype(BF16)
            variant = jnp.minimum(n, 1)
            outs, lses = [], []
            for h in range(2):
                qh = jnp.where(head0 if h == 0 else ~head0, qv, 0.0).astype(BF16)
                s = lax.dot_general(qh, kv, _NT, preferred_element_type=F32) + bias_ref[h, variant]
                m = jnp.max(s, axis=-1, keepdims=True)
                p = jnp.exp(s - m)
                den = jnp.sum(p, axis=-1, keepdims=True)
                lses.append(jnp.broadcast_to(m + jnp.log(den), (blk, LANES)))
                outs.append(jnp.dot(p.astype(BF16), vv, preferred_element_type=F32) / den)
            ob[rows(q_start, blk), :] = jnp.where(head0, outs[0], outs[1])
            lb[rows(q_start, blk), :] = jnp.where(head0, lses[0], lses[1])

        def flat_body(idx, carry, n_blk=n_blk, block_body=block_body):
            block_body(idx & (n_blk - 1), _shr_scalar(idx, n_blk.bit_length() - 1))
            return carry

        lax.fori_loop(0, dil * n_blk, flat_body, 0, unroll=DIL_UNROLL)

    def mix(i, carry):
        sl = pl.ds(pl.multiple_of(i * DIL_MIX_ROWS, DIL_MIX_ROWS), DIL_MIX_ROWS)
        l0, l1, l2 = lb0[sl, :], lb1[sl, :], lb2[sl, :]
        mx = jnp.maximum(jnp.maximum(l0, l1), l2)
        e0, e1, e2 = jnp.exp(l0 - mx), jnp.exp(l1 - mx), jnp.exp(l2 - mx)
        tot = e0 + e1 + e2
        o_ref[sl, :] = (e0 / tot) * ob0[sl, :] + (e1 / tot) * ob1[sl, :] + (e2 / tot) * ob2[sl, :]
        return carry

    lax.fori_loop(0, s_len // DIL_MIX_ROWS, mix, 0)


def _dilated(q, k, v, neg_slopes):
    b, s_len, w = q.shape
    n_pair = w // LANES
    slab = pl.BlockSpec((None, s_len, LANES), lambda i, p: (i, 0, p))
    buf = lambda rows: pltpu.VMEM((rows, LANES), F32)
    return pl.pallas_call(
        _dil_kernel,
        grid=(b, n_pair),
        in_specs=[slab, slab, slab,
                  pl.BlockSpec((None, 2, 1, 2 * BAND_BLOCK), lambda i, p: (p, 0, 0, 0))],
        out_specs=slab,
        out_shape=jax.ShapeDtypeStruct((b, s_len, w), F32),
        scratch_shapes=([buf(DIL_PAD + s_len), buf(DIL_PAD + s_len)] + [buf(s_len)] * 6
                        + [pltpu.VMEM((2, 2, BAND_BLOCK, 2 * BAND_BLOCK), F32)]),
        compiler_params=_cparams(("parallel", "parallel")),
        name="dilated",
    )(q, k, v, neg_slopes)


def _split_bf16(a):
    hi = a.astype(BF16)
    lo = (a - hi.astype(F32)).astype(BF16)
    return hi, lo


MOBA_QB = 16
MOBA_KB = 19
MOBA_GROUPS = (8, 4, 2, 1)


def _split3_masked(a):
    def keep_high_bits(x):
        bits = lax.bitcast_convert_type(x, U32) & jnp.uint32(0xFFFF0000)
        return lax.bitcast_convert_type(bits, F32)

    t0 = keep_high_bits(a)
    t1 = keep_high_bits(a - t0)
    t2 = keep_high_bits(a - t0 - t1)
    return t0, t1, t2


def _moba_bias_lanes(slopes, s_len):
    t_pos = jnp.arange(s_len, dtype=F32)
    lane = jnp.arange(LANES)
    neg = (-slopes).reshape(-1, 2, 1) * t_pos
    q_terms = _split3_masked(neg)
    k_terms = _split3_masked(-neg)
    shape = neg.shape + (LANES,)
    qx = jnp.zeros(shape, F32)
    kx = jnp.broadcast_to(jnp.where(lane == (t_pos[:, None] // MOBA_BLOCK), NEG_INF, 0.0), shape)
    for d in range(3):
        qx = jnp.where(lane == MOBA_QB + d, q_terms[d][..., None], jnp.where(lane == MOBA_KB + d, 1.0, qx))
        kx = jnp.where(lane == MOBA_QB + d, 1.0, jnp.where(lane == MOBA_KB + d, k_terms[d][..., None], kx))
    return qx.astype(BF16), kx.astype(BF16)


def _moba_kernel(q_ref, k_ref, v_ref, qx_ref, kx_ref, o_ref, qa_ref, kb_ref, vt_ref, m_ref, acc_ref):
    s_len = q_ref.shape[0]
    blk = MOBA_BLOCK
    n_blk = s_len // blk
    lane = lax.broadcasted_iota(I32, (blk, LANES), 1)
    head0 = lane < HEAD_DIM

    kb_ref[...] = k_ref[...].astype(BF16)

    k_mean = jnp.concatenate(
        [jnp.sum(k_ref[j * blk:(j + 1) * blk, :], axis=0, keepdims=True) for j in range(n_blk)],
        axis=0) * (1.0 / blk)
    q_hi, q_lo = _split_bf16(q_ref[...])

    blk_id = lax.broadcasted_iota(I32, (n_blk, s_len), 0)
    n_past = _shr(lax.broadcasted_iota(I32, (n_blk, s_len), 1), blk.bit_length() - 1)
    past = blk_id < n_past
    blk_f = blk_id.astype(F32)
    pad_rows = LANES - n_blk

    for h in range(2):
        hm = (lax.broadcasted_iota(I32, (n_blk, LANES), 1) < HEAD_DIM) == (h == 0)
        km_hi, km_lo = _split_bf16(jnp.where(hm, k_mean, 0.0))
        gate = (lax.dot_general(km_hi, q_hi, _NT, preferred_element_type=F32)
                + lax.dot_general(km_hi, q_lo, _NT, preferred_element_type=F32)
                + lax.dot_general(km_lo, q_hi, _NT, preferred_element_type=F32))
        g = jnp.where(past, gate, NEG_INF)
        sel = jnp.zeros((n_blk, s_len), F32)
        for _ in range(MOBA_TOPK):
            m = jnp.max(g, axis=0, keepdims=True)
            first = jnp.min(jnp.where(g == m, blk_f, float(n_blk)), axis=0, keepdims=True)
            pick = blk_f == first
            sel = jnp.where(pick, 1.0, sel)
            g = jnp.where(pick, -jnp.inf, g)
        not_sel = jnp.where(past, 1.0 - sel, jnp.where(blk_id == n_past, 0.0, 1.0))
        not_sel = jnp.concatenate([not_sel, jnp.zeros((pad_rows, s_len), F32)], axis=0)
        for i in range(n_blk):
            rows = slice(i * blk, (i + 1) * blk)
            qh = jnp.where(head0 if h == 0 else ~head0, q_ref[rows, :], 0.0) * SCALE
            qa_ref[h, rows, 0:LANES] = qh.astype(BF16)
            qa_ref[h, rows, LANES:] = not_sel[:, rows].T.astype(BF16) + qx_ref[h, rows, :]

    top_rows = lax.broadcasted_iota(I32, (LANES, blk), 0) < HEAD_DIM
    for j in range(n_blk):
        vt = v_ref[j * blk:(j + 1) * blk, :].T
        vt_ref[0, :, j * blk:(j + 1) * blk] = jnp.where(top_rows, vt, 1.0).astype(BF16)
        vt_ref[1, :, j * blk:(j + 1) * blk] = jnp.where(top_rows, 1.0, vt).astype(BF16)

    key_i = lax.broadcasted_iota(I32, (blk, blk), 0)
    qry_i = lax.broadcasted_iota(I32, (blk, blk), 1)
    causal = key_i <= qry_i

    def q_aug(i, h, tiles=1):
        rows = pl.ds(pl.multiple_of(i * blk, blk), tiles * blk)
        return jnp.concatenate([qa_ref[h, rows, 0:LANES], qa_ref[h, rows, LANES:]], axis=1)

    def q_cols(i, tiles=1):
        return pl.ds(pl.multiple_of(i * blk, blk), tiles * blk)

    def k_aug(j, h):
        c0 = pl.multiple_of(j * blk, blk)
        return jnp.concatenate([kb_ref[pl.ds(c0, blk), :], kx_ref[h, pl.ds(c0, blk), :]], axis=1)

    def v_t(j, h):
        return vt_ref[h, :, pl.ds(pl.multiple_of(j * blk, blk), blk)]

    def own_block(i, carry):
        for h in range(2):
            s = lax.dot_general(k_aug(i, h), q_aug(i, h), _NT, preferred_element_type=F32)
            s = jnp.where(causal, s, NEG_INF)
            m0 = jnp.max(s, axis=0, keepdims=True)
            m_ref[h, :, q_cols(i)] = m0
            acc_ref[h, :, q_cols(i)] = jnp.dot(v_t(i, h), jnp.exp(s - m0).astype(BF16),
                                               preferred_element_type=F32)
        return carry

    lax.fori_loop(0, n_blk, own_block, 0, unroll=2)

    def key_block(j, carry):
        ks = [k_aug(j, h) for h in range(2)]
        vs = [v_t(j, h) for h in range(2)]

        def update(i, count):
            cols = q_cols(i, count)
            for h in range(2):
                sj = lax.dot_general(ks[h], q_aug(i, h, count), _NT, preferred_element_type=F32)
                m = m_ref[h, :, cols]
                m_new = jnp.maximum(m, jnp.max(sj, axis=0, keepdims=True))
                acc_ref[h, :, cols] = jnp.exp(m - m_new) * acc_ref[h, :, cols] + jnp.dot(
                    vs[h], jnp.exp(sj - m_new).astype(BF16), preferred_element_type=F32)
                m_ref[h, :, cols] = m_new

        def tiles(first, count):
            def body(g, c2):
                update(first + count * g, count)
                return c2
            return body

        first = j + 1
        for size in MOBA_GROUPS:
            count = (n_blk - first) // size
            lax.fori_loop(0, count, tiles(first, size), 0)
            first = first + size * count
        return carry

    lax.fori_loop(0, n_blk - 1, key_block, 0)

    def finish(i, carry):
        a0, a1 = acc_ref[0, :, q_cols(i)], acc_ref[1, :, q_cols(i)]
        out_t = jnp.where(top_rows, a0 / a0[HEAD_DIM:HEAD_DIM + 1, :], a1 / a1[0:1, :])
        o_ref[pl.ds(pl.multiple_of(i * blk, blk), blk), :] = out_t.T
        return carry

    lax.fori_loop(0, n_blk, finish, 0, unroll=2)


def _moba(q, k, v, slopes):
    b, s_len, w = q.shape
    n_pair = w // LANES
    slab = pl.BlockSpec((None, s_len, LANES), lambda i, p: (i, 0, p))
    table = pl.BlockSpec((None, 2, s_len, LANES), lambda i, p: (p, 0, 0, 0))
    qx, kx = _moba_bias_lanes(slopes, s_len)
    return pl.pallas_call(
        _moba_kernel,
        grid=(b, n_pair),
        in_specs=[slab, slab, slab, table, table],
        out_specs=slab,
        out_shape=jax.ShapeDtypeStruct((b, s_len, w), F32),
        scratch_shapes=[pltpu.VMEM((2, s_len, 2 * LANES), BF16),
                        pltpu.VMEM((s_len, LANES), BF16),
                        pltpu.VMEM((2, LANES, s_len), BF16),
                        pltpu.VMEM((2, 1, s_len), F32),
                        pltpu.VMEM((2, LANES, s_len), F32)],
        compiler_params=_cparams(("parallel", "parallel")),
        name="moba",
    )(q, k, v, qx, kx)


MEMATTN_TQ = 512


def _memattn_kernel(q_ref, k_ref, v_ref, o_ref):
    q = q_ref[...]
    kb = k_ref[...].astype(BF16)
    vb = v_ref[...].astype(BF16)
    head = _shr(lax.broadcasted_iota(I32, q.shape, 1), HEAD_DIM.bit_length() - 1)
    out = jnp.zeros(q.shape, F32)
    for h in range(N_HEADS_MEM):
        qh = jnp.where(head == h, q, 0.0).astype(BF16)
        s = lax.dot_general(qh, kb, _NT, preferred_element_type=F32) * SCALE
        m = jnp.max(s, axis=-1, keepdims=True)
        p = jnp.exp(s - m)
        den = jnp.sum(p, axis=-1, keepdims=True)
        oh = jnp.dot(p.astype(BF16), vb, preferred_element_type=F32) / den
        out = jnp.where(head == h, oh, out)
    o_ref[...] = out


def _memattn(q, k, v):
    b, s_len, w = q.shape
    tq = MEMATTN_TQ
    return pl.pallas_call(
        _memattn_kernel,
        grid=(b, s_len // tq),
        in_specs=[pl.BlockSpec((None, tq, w), lambda i, j: (i, j, 0)),
                  pl.BlockSpec((None, MEM_LEN, w), lambda i, j: (i, 0, 0)),
                  pl.BlockSpec((None, MEM_LEN, w), lambda i, j: (i, 0, 0))],
        out_specs=pl.BlockSpec((None, tq, w), lambda i, j: (i, j, 0)),
        out_shape=jax.ShapeDtypeStruct((b, s_len, w), F32),
        compiler_params=_cparams(("parallel", "parallel")),
        name="memattn",
    )(q, k, v)


OUTPROJ_TM = 512


def _outproj_kernel(od_ref, ob_ref, om_ref, x_ref, gd, gb, gm, wo_ref, gf, wq_ref, k1_ref, k2_ref,
                    x1_ref, hn_ref, s1_ref, s2_ref):
    y = jnp.dot(_row_rms(od_ref[...], gd[...]).astype(BF16), wo_ref[0:W_DIL, :],
                preferred_element_type=F32)
    y += jnp.dot(_row_rms(ob_ref[...], gb[...]).astype(BF16), wo_ref[W_DIL:W_DIL + W_MOBA, :],
                 preferred_element_type=F32)
    y += jnp.dot(_row_rms(om_ref[...], gm[...]).astype(BF16), wo_ref[W_DIL + W_MOBA:, :],
                 preferred_element_type=F32)
    x1 = x_ref[...] + y
    x1_ref[...] = x1
    hb = _row_rms(x1, gf[...]).astype(BF16)
    hn_ref[...] = hb
    half = PEER_DKEY // 2
    qry = jnp.dot(hb, wq_ref[...], preferred_element_type=F32).astype(BF16)
    for h in range(PEER_HEADS):
        qh = qry[:, h * PEER_DKEY:(h + 1) * PEER_DKEY]
        s1_ref[h] = lax.dot_general(k1_ref[h], qh[:, :half], _NT, preferred_element_type=F32)
        s2_ref[h] = lax.dot_general(k2_ref[h], qh[:, half:], _NT, preferred_element_type=F32)


def _outproj(o_dil, o_moba, o_mem, x2, og_dil, og_moba, og_mem, w_out, g_ffn, w_q, sub1, sub2):
    t = x2.shape[0]
    tm = OUTPROJ_TM
    row = lambda w: pl.BlockSpec((1, w), lambda i: (0, 0))
    tile = lambda w: pl.BlockSpec((tm, w), lambda i: (i, 0))
    full = lambda shape: pl.BlockSpec(shape, lambda i: (0,) * len(shape))
    score = pl.BlockSpec((PEER_HEADS, PEER_NKEYS, tm), lambda i: (0, 0, i))
    half = PEER_DKEY // 2
    return pl.pallas_call(
        _outproj_kernel,
        grid=(t // tm,),
        in_specs=[tile(W_DIL), tile(W_MOBA), tile(W_MEM), tile(D_MODEL),
                  row(W_DIL), row(W_MOBA), row(W_MEM), full((D_MODEL, D_MODEL)), row(D_MODEL),
                  full((D_MODEL, PEER_HEADS * PEER_DKEY)),
                  full((PEER_HEADS, PEER_NKEYS, half)), full((PEER_HEADS, PEER_NKEYS, half))],
        out_specs=[tile(D_MODEL), tile(D_MODEL), score, score],
        out_shape=[jax.ShapeDtypeStruct((t, D_MODEL), F32),
                   jax.ShapeDtypeStruct((t, D_MODEL), BF16),
                   jax.ShapeDtypeStruct((PEER_HEADS, PEER_NKEYS, t), F32),
                   jax.ShapeDtypeStruct((PEER_HEADS, PEER_NKEYS, t), F32)],
        compiler_params=_cparams(("parallel",)),
        name="outproj",
    )(o_dil, o_moba, o_mem, x2, og_dil, og_moba, og_mem, w_out, g_ffn, w_q, sub1, sub2)


TOPK_TM = 512
TOPK_COLS_PER_ITER = 8
_ID_LIMIT = 1024.0


def _top_rows(s, k, ids=None, rows_at=None):
    n_rows = s.shape[0]
    if ids is None:
        ids = lax.broadcasted_iota(I32, s.shape, 0).astype(F32)
    vals, idxs = [], []
    for t in range(k):
        r = n_rows if rows_at is None else rows_at[t]
        head, head_ids = s[:r, :], ids[:r, :]
        m = jnp.max(head, axis=0, keepdims=True)
        first = jnp.min(jnp.where(head == m, head_ids, _ID_LIMIT), axis=0, keepdims=True)
        vals.append(m)
        idxs.append(first)
        head = jnp.where(head_ids == first, -jnp.inf, head)
        s = head if r == n_rows else jnp.concatenate([head, s[r:, :]], axis=0)
    return jnp.concatenate(vals, axis=0), jnp.concatenate(idxs, axis=0)


def _pick_rows(table, sel):
    out = jnp.zeros(sel.shape, table.dtype)
    for j in range(table.shape[0]):
        out = jnp.where(sel == j, table[j:j + 1, :], out)
    return out


def _candidates(v1, v2):
    n = v1.shape[1]
    sub = lax.broadcasted_iota(I32, (8, n), 0)
    sub_f = sub.astype(F32)
    sums, ids = [], []
    for j1 in range(8):
        both = v1[j1:j1 + 1, :] + v2[0:8, :]
        sums.append(both if j1 == 0 else jnp.where(sub < PEER_TOPK // (j1 + 1), both, -jnp.inf))
        ids.append(sub_f + float(j1 * PEER_TOPK))
    sums += [v1[0:1, :] + v2[8:16, :], v1[8:16, :] + v2[0:1, :]]
    ids += [sub_f + 8.0, sub_f * float(PEER_TOPK) + float(8 * PEER_TOPK)]
    rows_at = [8 * min(t, 8) if t <= 8 else 80 for t in range(1, PEER_TOPK + 1)]
    return jnp.concatenate(sums, axis=0), jnp.concatenate(ids, axis=0), rows_at


def _retrieve(s1, s2):
    v1, i1 = _top_rows(s1, PEER_TOPK)
    v2, i2 = _top_rows(s2, PEER_TOPK)
    sums, ids, rows_at = _candidates(v1, v2)
    top_s, pair = _top_rows(sums, PEER_TOPK, ids, rows_at)
    e = jnp.exp(top_s - top_s[0:1, :])
    pair = pair.astype(I32)
    bits = PEER_TOPK.bit_length() - 1
    return (_pick_rows(i1, _shr(pair, bits)), _pick_rows(i2, pair & (PEER_TOPK - 1)),
            e / jnp.sum(e, axis=0, keepdims=True))


def _peertopk_kernel(s1_ref, s2_ref, i1_ref, i2_ref, g_ref):
    n_col = s1_ref.shape[2] // LANES

    def body(i, carry):
        for c in range(TOPK_COLS_PER_ITER):
            item = i * TOPK_COLS_PER_ITER + c
            h = _shr_scalar(item, n_col.bit_length() - 1)
            cols = pl.ds(pl.multiple_of((item & (n_col - 1)) * LANES, LANES), LANES)
            i1_ref[h, :, cols], i2_ref[h, :, cols], g_ref[h, :, cols] = _retrieve(
                s1_ref[h, :, cols], s2_ref[h, :, cols])
        return carry

    lax.fori_loop(0, PEER_HEADS * n_col // TOPK_COLS_PER_ITER, body, 0)


def _peertopk(s1, s2):
    t = s1.shape[2]
    tm = TOPK_TM
    score = pl.BlockSpec((PEER_HEADS, PEER_NKEYS, tm), lambda i: (0, 0, i))
    slot = pl.BlockSpec((PEER_HEADS, PEER_TOPK, tm), lambda i: (0, 0, i))
    return pl.pallas_call(
        _peertopk_kernel,
        grid=(t // tm,),
        in_specs=[score, score],
        out_specs=[slot, slot, slot],
        out_shape=[jax.ShapeDtypeStruct((PEER_HEADS, PEER_TOPK, t), F32)] * 3,
        compiler_params=_cparams(("parallel",)),
        name="peertopk",
    )(s1, s2)


PEER_TM = 512
PEER_TE = 2048
PEER_KEY_PAIRS = PEER_NKEYS // 2
PEER_PITCH = PEER_TM + 8
PEER_BUILD_UNROLL = 128
BF16_ROWS = 16


def _rows_bf16(row, n_rows):
    tile = jnp.broadcast_to(row, (BF16_ROWS, row.shape[1])).astype(BF16)
    return jnp.concatenate([tile] * (n_rows // BF16_ROWS), axis=0)


def _peerffn_kernel(h_ref, x1_ref, i1_ref, i2_ref, g_ref, u_ref, v_ref, o_ref, gate_ref):
    c = pl.program_id(1)
    slabs = PEER_TE // PEER_NKEYS

    @pl.when(c == 0)
    def _build_gate_matrix():
        key_id = lax.broadcasted_iota(I32, (PEER_NKEYS, PEER_SLOTS), 0).astype(BF16)
        one = jnp.ones((PEER_NKEYS, PEER_SLOTS), BF16)
        zero = jnp.zeros((PEER_NKEYS, PEER_SLOTS), BF16)

        def token(t, carry):
            i1 = _rows_bf16(i1_ref[pl.ds(t, 1), :], PEER_NKEYS)
            i2 = _rows_bf16(i2_ref[pl.ds(t, 1), :], PEER_NKEYS)
            g = _rows_bf16(g_ref[pl.ds(t, 1), :], PEER_NKEYS)
            lhs = jnp.where(key_id == i1, g, zero)
            rhs = jnp.where(key_id == i2, one, zero)
            gt = lax.dot_general(lhs, rhs, _NT, preferred_element_type=F32)
            gate_ref[pl.ds(t, PEER_KEY_PAIRS, stride=PEER_PITCH), :] = pltpu.bitcast(gt.astype(BF16), U32)
            return carry

        lax.fori_loop(0, PEER_TM, token, 0, unroll=PEER_BUILD_UNROLL)
        o_ref[...] = x1_ref[...]

    a = lax.dot_general(h_ref[...], u_ref[...], _NT, preferred_element_type=F32)
    ws = []
    for r in range(slabs // 2):
        start = pl.multiple_of((c * (slabs // 2) + r) * PEER_PITCH, 8)
        word = gate_ref[pl.ds(start, PEER_TM), :]
        g_even = lax.bitcast_convert_type(lax.shift_left(word, jnp.uint32(16)), F32)
        g_odd = lax.bitcast_convert_type(word & jnp.uint32(0xFFFF0000), F32)
        for j, gj in ((2 * r, g_even), (2 * r + 1, g_odd)):
            aj = a[:, j * PEER_NKEYS:(j + 1) * PEER_NKEYS]
            ws.append((0.5 * aj * (1.0 + lax.erf(aj * SQRT_HALF)) * gj).astype(BF16))
    o_ref[...] += jnp.dot(jnp.concatenate(ws, axis=1), v_ref[...], preferred_element_type=F32)


def _peerffn(hn, x1, i1, i2, gate, u, v):
    t = hn.shape[0]
    tm, te = PEER_TM, PEER_TE
    tile = lambda w: pl.BlockSpec((tm, w), lambda i, c: (i, 0))
    chunk = pl.BlockSpec((te, D_MODEL), lambda i, c: (c, 0))
    return pl.pallas_call(
        _peerffn_kernel,
        grid=(t // tm, PEER_EXPERTS // te),
        in_specs=[tile(D_MODEL), tile(D_MODEL), tile(PEER_SLOTS), tile(PEER_SLOTS), tile(PEER_SLOTS),
                  chunk, chunk],
        out_specs=tile(D_MODEL),
        out_shape=jax.ShapeDtypeStruct((t, D_MODEL), F32),
        scratch_shapes=[pltpu.VMEM((PEER_KEY_PAIRS * PEER_PITCH, LANES), U32)],
        compiler_params=_cparams(("parallel", "arbitrary")),
        name="peerffn",
    )(hn, x1, i1, i2, gate, u, v)


def _neg_slope_rows(slopes, width):
    n = slopes.shape[0]
    return jnp.broadcast_to((-slopes).reshape(n // 2, 2, 1, 1), (n // 2, 2, 1, width))


def _tiled_gain(g, reps):
    return jnp.tile(g, reps)[None, :]


def _layer(x, mem, g_mix, w_in, qg_dil, kg_dil, qg_moba, kg_moba, qg_mem, kg_mem, g_memtok, w_mem_kv,
           og_dil, og_moba, og_mem, w_out, g_ffn, w_peer_q, sub1, sub2, peer_u, peer_v):
    b, s_len, d = x.shape
    t = b * s_len
    n_mix = N_HEADS_DIL + N_HEADS_MOBA
    slopes = jnp.exp2(-8.0 * jnp.arange(1, n_mix + 1, dtype=F32) / n_mix)
    ns_dil = _neg_slope_rows(slopes[0::2], 2 * BAND_BLOCK)

    x2 = x.reshape(t, d)
    k_m, v_m = _memkv(mem, g_memtok[None, :], w_mem_kv.astype(BF16), _tiled_gain(kg_mem, N_HEADS_MEM))
    q_d, k_d, v_d, q_b, k_b, v_b, q_m = _inproj(
        x2, g_mix[None, :], w_in.astype(BF16),
        _tiled_gain(qg_dil, N_HEADS_DIL), _tiled_gain(kg_dil, N_HEADS_DIL),
        _tiled_gain(qg_moba, N_HEADS_MOBA), _tiled_gain(kg_moba, N_HEADS_MOBA),
        _tiled_gain(qg_mem, N_HEADS_MEM))
    seq = lambda a: a.reshape(b, s_len, a.shape[-1])
    o_dil = _dilated(seq(q_d), seq(k_d), seq(v_d), ns_dil)
    o_moba = _moba(seq(q_b), seq(k_b), seq(v_b), slopes[1::2])
    o_mem = _memattn(seq(q_m), k_m, v_m)
    x1, hn, s1, s2 = _outproj(
        o_dil.reshape(t, W_DIL), o_moba.reshape(t, W_MOBA), o_mem.reshape(t, W_MEM), x2,
        og_dil[None, :], og_moba[None, :], og_mem[None, :], w_out.astype(BF16), g_ffn[None, :],
        w_peer_q.astype(BF16), sub1.astype(BF16), sub2.astype(BF16))
    i1, i2, gate = _peertopk(s1, s2)
    slots = lambda a: a.reshape(PEER_SLOTS, t).T
    out = _peerffn(hn, x1, slots(i1), slots(i2), slots(gate), peer_u.astype(BF16), peer_v.astype(BF16))
    return out.reshape(b, s_len, d)


def kernel(x, mem, g_mix, w_in, qg_dil, kg_dil, qg_moba, kg_moba, qg_mem, kg_mem, g_memtok, w_mem_kv,
           og_dil, og_moba, og_mem, w_out, g_ffn, w_peer_q, peer_subkeys_1, peer_subkeys_2, peer_u,
           peer_v):
    h = x
    for layer in range(g_mix.shape[0]):
        h = _layer(h, mem, g_mix[layer], w_in[layer], qg_dil[layer], kg_dil[layer], qg_moba[layer],
                   kg_moba[layer], qg_mem[layer], kg_mem[layer], g_memtok[layer], w_mem_kv[layer],
                   og_dil[layer], og_moba[layer], og_mem[layer], w_out[layer], g_ffn[layer],
                   w_peer_q[layer], peer_subkeys_1[layer], peer_subkeys_2[layer], peer_u[layer],
                   peer_v[layer])
    return h
```

```python
import math

import jax
import jax.numpy as jnp
from jax import lax
from jax.experimental import pallas as pl
from jax.experimental.pallas import tpu as pltpu

F32 = jnp.float32
BF16 = jnp.bfloat16
I32 = jnp.int32
U32 = jnp.uint32

LANES = 128
D_MODEL = 1024
N_HEADS_DIL = 6
N_HEADS_MOBA = 6
N_HEADS_MEM = 4
HEAD_DIM = 64
W_DIL = N_HEADS_DIL * HEAD_DIM
W_MOBA = N_HEADS_MOBA * HEAD_DIM
W_MEM = N_HEADS_MEM * HEAD_DIM
IN_WIDTH = 3 * W_DIL + 3 * W_MOBA + W_MEM
DIL_CONFIGS = ((128, 1), (512, 4), (2048, 16))
BAND_BLOCK = 128
MOBA_BLOCK = 256
MOBA_TOPK = 3
MEM_LEN = 256
PEER_HEADS = 8
PEER_NKEYS = 128
PEER_EXPERTS = PEER_NKEYS * PEER_NKEYS
PEER_TOPK = 16
PEER_DKEY = 256
PEER_SLOTS = PEER_HEADS * PEER_TOPK
RMS_EPS = 1e-6
NEG_INF = -1e30
SCALE = 1.0 / math.sqrt(HEAD_DIM)
SQRT_HALF = math.sqrt(0.5)

VMEM_LIMIT = 56 * 1024 * 1024

_NT = (((1,), (1,)), ((), ()))


def _shr(a, bits):
    return lax.shift_right_logical(a, jnp.full(a.shape, bits, a.dtype))


def _shr_scalar(a, bits):
    return lax.shift_right_logical(a, jnp.int32(bits))


def _cparams(sem):
    return pltpu.CompilerParams(dimension_semantics=sem, vmem_limit_bytes=VMEM_LIMIT)


def _row_rms(a, gain):
    return a * lax.rsqrt(jnp.mean(a * a, axis=-1, keepdims=True) + RMS_EPS) * gain


def _group_mean_sq(p, group):
    w = p.shape[-1]
    bits = group.bit_length() - 1
    gi = _shr(lax.broadcasted_iota(I32, (w, w), 0), bits)
    gj = _shr(lax.broadcasted_iota(I32, (w, w), 1), bits)
    ones_bd = jnp.where(gi == gj, 1.0, 0.0).astype(BF16)
    p2 = p * p
    hi = p2.astype(BF16)
    lo = (p2 - hi.astype(F32)).astype(BF16)
    ss = (jnp.dot(hi, ones_bd, preferred_element_type=F32)
          + jnp.dot(lo, ones_bd, preferred_element_type=F32))
    return ss * (1.0 / group)


def _head_rms(p, gain):
    return p * lax.rsqrt(_group_mean_sq(p, HEAD_DIM) + RMS_EPS) * gain


def _memkv_kernel(mem_ref, g_ref, w_ref, kg_ref, k_ref, v_ref):
    hn = _row_rms(mem_ref[...], g_ref[...]).astype(BF16)
    kv = jnp.dot(hn, w_ref[...], preferred_element_type=F32)
    k_ref[...] = _head_rms(kv[:, :W_MEM], kg_ref[...])
    v_ref[...] = kv[:, W_MEM:]


def _memkv(mem, g_memtok, w_kv, kg_mem):
    b = mem.shape[0]
    return pl.pallas_call(
        _memkv_kernel,
        grid=(b,),
        in_specs=[
            pl.BlockSpec((None, MEM_LEN, D_MODEL), lambda i: (i, 0, 0)),
            pl.BlockSpec((1, D_MODEL), lambda i: (0, 0)),
            pl.BlockSpec((D_MODEL, 2 * W_MEM), lambda i: (0, 0)),
            pl.BlockSpec((1, W_MEM), lambda i: (0, 0)),
        ],
        out_specs=[pl.BlockSpec((None, MEM_LEN, W_MEM), lambda i: (i, 0, 0))] * 2,
        out_shape=[jax.ShapeDtypeStruct((b, MEM_LEN, W_MEM), F32)] * 2,
        compiler_params=_cparams(("parallel",)),
        name="memkv",
    )(mem, g_memtok, w_kv, kg_mem)


INPROJ_TM = 512


def _inproj_kernel(x_ref, g_ref, w_ref, gqd, gkd, gqb, gkb, gqm, qd, kd, vd, qb, kb, vb, qm):
    hb = _row_rms(x_ref[...], g_ref[...]).astype(BF16)
    proj = jnp.dot(hb, w_ref[...], preferred_element_type=F32)

    def seg(lo, width):
        return proj[:, lo:lo + width]

    qd[...] = _head_rms(seg(0, W_DIL), gqd[...])
    kd[...] = _head_rms(seg(W_DIL, W_DIL), gkd[...])
    vd[...] = seg(2 * W_DIL, W_DIL)
    base = 3 * W_DIL
    qb[...] = _head_rms(seg(base, W_MOBA), gqb[...])
    kb[...] = _head_rms(seg(base + W_MOBA, W_MOBA), gkb[...])
    vb[...] = seg(base + 2 * W_MOBA, W_MOBA)
    qm[...] = _head_rms(seg(base + 3 * W_MOBA, W_MEM), gqm[...])


def _inproj(x2, g_mix, w_in, gqd, gkd, gqb, gkb, gqm):
    t = x2.shape[0]
    tm = INPROJ_TM
    row = lambda w: pl.BlockSpec((1, w), lambda i: (0, 0))
    tile = lambda w: pl.BlockSpec((tm, w), lambda i: (i, 0))
    widths = (W_DIL, W_DIL, W_DIL, W_MOBA, W_MOBA, W_MOBA, W_MEM)
    return pl.pallas_call(
        _inproj_kernel,
        grid=(t // tm,),
        in_specs=[tile(D_MODEL), row(D_MODEL), pl.BlockSpec((D_MODEL, IN_WIDTH), lambda i: (0, 0)),
                  row(W_DIL), row(W_DIL), row(W_MOBA), row(W_MOBA), row(W_MEM)],
        out_specs=[tile(w) for w in widths],
        out_shape=[jax.ShapeDtypeStruct((t, w), F32) for w in widths],
        compiler_params=_cparams(("parallel",)),
        name="inproj",
    )(x2, g_mix, w_in, gqd, gkd, gqb, gkb, gqm)


DIL_PAD = BAND_BLOCK * max(d for _, d in DIL_CONFIGS)
DIL_MIX_ROWS = 256
DIL_UNROLL = 16


def _dil_kernel(q_ref, k_ref, v_ref, ns_ref, o_ref, kpad, vpad, ob0, ob1, ob2, lb0, lb1, lb2, bias_ref):
    s_len = q_ref.shape[0]
    zeros = jnp.zeros((DIL_PAD, LANES), F32)
    kpad[0:DIL_PAD, :] = zeros
    vpad[0:DIL_PAD, :] = zeros
    kpad[DIL_PAD:, :] = k_ref[...]
    vpad[DIL_PAD:, :] = v_ref[...]

    blk = BAND_BLOCK
    head0 = lax.broadcasted_iota(I32, (blk, LANES), 1) < HEAD_DIM
    ql = lax.broadcasted_iota(I32, (blk, 2 * blk), 0)
    kl = lax.broadcasted_iota(I32, (blk, 2 * blk), 1)
    delta = blk + ql - kl
    obs = (ob0, ob1, ob2)
    lbs = (lb0, lb1, lb2)

    for c, (window, dil) in enumerate(DIL_CONFIGS):
        reach = window // dil
        n_blk = s_len // dil // blk
        in_band = (delta >= 0) & (delta <= reach)
        dist = (delta * dil).astype(F32)
        ob, lb = obs[c], lbs[c]
        for h in range(2):
            bias = ns_ref[h] * dist
            bias_ref[h, 0] = jnp.where(in_band & (kl >= blk), bias, NEG_INF)
            bias_ref[h, 1] = jnp.where(in_band, bias, NEG_INF)

        def rows(start, size, dil=dil):
            return pl.ds(start, size) if dil == 1 else pl.ds(start, size, stride=dil)

        def block_body(n, r, dil=dil, ob=ob, lb=lb, rows=rows):
            q_start = r + dil * blk * n
            k_start = DIL_PAD + r + dil * blk * (n - 1)
            qv = q_ref[rows(q_start, blk), :] * SCALE
            kv = kpad[rows(k_start, 2 * blk), :].astype(BF16)
            vv = vpad[rows(k_start, 2 * blk), :].astype(BF16)
            variant = jnp.minimum(n, 1)
            outs, lses = [], []
            for h in range(2):
                qh = jnp.where(head0 if h == 0 else ~head0, qv, 0.0).astype(BF16)
                s = lax.dot_general(qh, kv, _NT, preferred_element_type=F32) + bias_ref[h, variant]
                m = jnp.max(s, axis=-1, keepdims=True)
                p = jnp.exp(s - m)
                den = jnp.sum(p, axis=-1, keepdims=True)
                lses.append(jnp.broadcast_to(m + jnp.log(den), (blk, LANES)))
                outs.append(jnp.dot(p.astype(BF16), vv, preferred_element_type=F32) / den)
            ob[rows(q_start, blk), :] = jnp.where(head0, outs[0], outs[1])
            lb[rows(q_start, blk), :] = jnp.where(head0, lses[0], lses[1])

        def flat_body(idx, carry, n_blk=n_blk, block_body=block_body):
            block_body(idx & (n_blk - 1), _shr_scalar(idx, n_blk.bit_length() - 1))
            return carry

        lax.fori_loop(0, dil * n_blk, flat_body, 0, unroll=DIL_UNROLL)

    def mix(i, carry):
        sl = pl.ds(pl.multiple_of(i * DIL_MIX_ROWS, DIL_MIX_ROWS), DIL_MIX_ROWS)
        l0, l1, l2 = lb0[sl, :], lb1[sl, :], lb2[sl, :]
        mx = jnp.maximum(jnp.maximum(l0, l1), l2)
        e0, e1, e2 = jnp.exp(l0 - mx), jnp.exp(l1 - mx), jnp.exp(l2 - mx)
        tot = e0 + e1 + e2
        o_ref[sl, :] = (e0 / tot) * ob0[sl, :] + (e1 / tot) * ob1[sl, :] + (e2 / tot) * ob2[sl, :]
        return carry

    lax.fori_loop(0, s_len // DIL_MIX_ROWS, mix, 0)


def _dilated(q, k, v, neg_slopes):
    b, s_len, w = q.shape
    n_pair = w // LANES
    slab = pl.BlockSpec((None, s_len, LANES), lambda i, p: (i, 0, p))
    buf = lambda rows: pltpu.VMEM((rows, LANES), F32)
    return pl.pallas_call(
        _dil_kernel,
        grid=(b, n_pair),
        in_specs=[slab, slab, slab,
                  pl.BlockSpec((None, 2, 1, 2 * BAND_BLOCK), lambda i, p: (p, 0, 0, 0))],
        out_specs=slab,
        out_shape=jax.ShapeDtypeStruct((b, s_len, w), F32),
        scratch_shapes=([buf(DIL_PAD + s_len), buf(DIL_PAD + s_len)] + [buf(s_len)] * 6
                        + [pltpu.VMEM((2, 2, BAND_BLOCK, 2 * BAND_BLOCK), F32)]),
        compiler_params=_cparams(("parallel", "parallel")),
        name="dilated",
    )(q, k, v, neg_slopes)


def _split_bf16(a):
    hi = a.astype(BF16)
    lo = (a - hi.astype(F32)).astype(BF16)
    return hi, lo


MOBA_QB = 16
MOBA_KB = 19
MOBA_GROUPS = (8, 4, 2, 1)


def _split3_masked(a):
    def keep_high_bits(x):
        bits = lax.bitcast_convert_type(x, U32) & jnp.uint32(0xFFFF0000)
        return lax.bitcast_convert_type(bits, F32)

    t0 = keep_high_bits(a)
    t1 = keep_high_bits(a - t0)
    t2 = keep_high_bits(a - t0 - t1)
    return t0, t1, t2


def _moba_bias_lanes(slopes, s_len):
    t_pos = jnp.arange(s_len, dtype=F32)
    lane = jnp.arange(LANES)
    neg = (-slopes).reshape(-1, 2, 1) * t_pos
    q_terms = _split3_masked(neg)
    k_terms = _split3_masked(-neg)
    shape = neg.shape + (LANES,)
    qx = jnp.zeros(shape, F32)
    kx = jnp.broadcast_to(jnp.where(lane == (t_pos[:, None] // MOBA_BLOCK), NEG_INF, 0.0), shape)
    for d in range(3):
        qx = jnp.where(lane == MOBA_QB + d, q_terms[d][..., None], jnp.where(lane == MOBA_KB + d, 1.0, qx))
        kx = jnp.where(lane == MOBA_QB + d, 1.0, jnp.where(lane == MOBA_KB + d, k_terms[d][..., None], kx))
    return qx.astype(BF16), kx.astype(BF16)


def _moba_kernel(q_ref, k_ref, v_ref, qx_ref, kx_ref, o_ref, qa_ref, kb_ref, vt_ref, m_ref, acc_ref):
    s_len = q_ref.shape[0]
    blk = MOBA_BLOCK
    n_blk = s_len // blk
    lane = lax.broadcasted_iota(I32, (blk, LANES), 1)
    head0 = lane < HEAD_DIM

    kb_ref[...] = k_ref[...].astype(BF16)

    k_mean = jnp.concatenate(
        [jnp.sum(k_ref[j * blk:(j + 1) * blk, :], axis=0, keepdims=True) for j in range(n_blk)],
        axis=0) * (1.0 / blk)
    q_hi, q_lo = _split_bf16(q_ref[...])

    blk_id = lax.broadcasted_iota(I32, (n_blk, s_len), 0)
    n_past = _shr(lax.broadcasted_iota(I32, (n_blk, s_len), 1), blk.bit_length() - 1)
    past = blk_id < n_past
    blk_f = blk_id.astype(F32)
    pad_rows = LANES - n_blk

    for h in range(2):
        hm = (lax.broadcasted_iota(I32, (n_blk, LANES), 1) < HEAD_DIM) == (h == 0)
        km_hi, km_lo = _split_bf16(jnp.where(hm, k_mean, 0.0))
        gate = (lax.dot_general(km_hi, q_hi, _NT, preferred_element_type=F32)
                + lax.dot_general(km_hi, q_lo, _NT, preferred_element_type=F32)
                + lax.dot_general(km_lo, q_hi, _NT, preferred_element_type=F32))
        g = jnp.where(past, gate, NEG_INF)
        sel = jnp.zeros((n_blk, s_len), F32)
        for _ in range(MOBA_TOPK):
            m = jnp.max(g, axis=0, keepdims=True)
            first = jnp.min(jnp.where(g == m, blk_f, float(n_blk)), axis=0, keepdims=True)
            pick = blk_f == first
            sel = jnp.where(pick, 1.0, sel)
            g = jnp.where(pick, -jnp.inf, g)
        not_sel = jnp.where(past, 1.0 - sel, jnp.where(blk_id == n_past, 0.0, 1.0))
        not_sel = jnp.concatenate([not_sel, jnp.zeros((pad_rows, s_len), F32)], axis=0)
        for i in range(n_blk):
            rows = slice(i * blk, (i + 1) * blk)
            qh = jnp.where(head0 if h == 0 else ~head0, q_ref[rows, :], 0.0) * SCALE
            qa_ref[h, rows, 0:LANES] = qh.astype(BF16)
            qa_ref[h, rows, LANES:] = not_sel[:, rows].T.astype(BF16) + qx_ref[h, rows, :]

    top_rows = lax.broadcasted_iota(I32, (LANES, blk), 0) < HEAD_DIM
    for j in range(n_blk):
        vt = v_ref[j * blk:(j + 1) * blk, :].T
        vt_ref[0, :, j * blk:(j + 1) * blk] = jnp.where(top_rows, vt, 1.0).astype(BF16)
        vt_ref[1, :, j * blk:(j + 1) * blk] = jnp.where(top_rows, 1.0, vt).astype(BF16)

    key_i = lax.broadcasted_iota(I32, (blk, blk), 0)
    qry_i = lax.broadcasted_iota(I32, (blk, blk), 1)

    def q_aug(i, h, tiles=1):
        rows = pl.ds(pl.multiple_of(i * blk, blk), tiles * blk)
        return jnp.concatenate([qa_ref[h, rows, 0:LANES], qa_ref[h, rows, LANES:]], axis=1)

    def q_cols(i, tiles=1):
        return pl.ds(pl.multiple_of(i * blk, blk), tiles * blk)

    def k_aug(j, h):
        c0 = pl.multiple_of(j * blk, blk)
        return jnp.concatenate([kb_ref[pl.ds(c0, blk), :], kx_ref[h, pl.ds(c0, blk), :]], axis=1)

    def v_t(j, h):
        return vt_ref[h, :, pl.ds(pl.multiple_of(j * blk, blk), blk)]

    m_ref[...] = jnp.full(m_ref.shape, NEG_INF, F32)
    acc_ref[...] = jnp.zeros(acc_ref.shape, F32)

    def key_block(j, carry):
        ks = [k_aug(j, h) for h in range(2)]
        vs = [v_t(j, h) for h in range(2)]

        def update(i, count):
            cols = q_cols(i, count)
            own_mask = key_i <= qry_i + jnp.where(i == j, 0, blk)
            for h in range(2):
                sj = lax.dot_general(ks[h], q_aug(i, h, count), _NT, preferred_element_type=F32)
                own = jnp.where(own_mask, sj[:, 0:blk], NEG_INF)
                sj = own if count == 1 else jnp.concatenate([own, sj[:, blk:]], axis=1)
                m = m_ref[h, :, cols]
                m_new = jnp.maximum(m, jnp.max(sj, axis=0, keepdims=True))
                acc_ref[h, :, cols] = jnp.exp(m - m_new) * acc_ref[h, :, cols] + jnp.dot(
                    vs[h], jnp.exp(sj - m_new).astype(BF16), preferred_element_type=F32)
                m_ref[h, :, cols] = m_new

        def tiles(first, count):
            def body(g, c2):
                update(first + count * g, count)
                return c2
            return body

        first = j
        for size in MOBA_GROUPS:
            count = (n_blk - first) // size
            lax.fori_loop(0, count, tiles(first, size), 0)
            first = first + size * count
        return carry

    lax.fori_loop(0, n_blk, key_block, 0)

    def finish(i, carry):
        a0, a1 = acc_ref[0, :, q_cols(i)], acc_ref[1, :, q_cols(i)]
        out_t = jnp.where(top_rows, a0 / a0[HEAD_DIM:HEAD_DIM + 1, :], a1 / a1[0:1, :])
        o_ref[pl.ds(pl.multiple_of(i * blk, blk), blk), :] = out_t.T
        return carry

    lax.fori_loop(0, n_blk, finish, 0, unroll=2)


def _moba(q, k, v, slopes):
    b, s_len, w = q.shape
    n_pair = w // LANES
    slab = pl.BlockSpec((None, s_len, LANES), lambda i, p: (i, 0, p))
    table = pl.BlockSpec((None, 2, s_len, LANES), lambda i, p: (p, 0, 0, 0))
    qx, kx = _moba_bias_lanes(slopes, s_len)
    return pl.pallas_call(
        _moba_kernel,
        grid=(b, n_pair),
        in_specs=[slab, slab, slab, table, table],
        out_specs=slab,
        out_shape=jax.ShapeDtypeStruct((b, s_len, w), F32),
        scratch_shapes=[pltpu.VMEM((2, s_len, 2 * LANES), BF16),
                        pltpu.VMEM((s_len, LANES), BF16),
                        pltpu.VMEM((2, LANES, s_len), BF16),
                        pltpu.VMEM((2, 1, s_len), F32),
                        pltpu.VMEM((2, LANES, s_len), F32)],
        compiler_params=_cparams(("parallel", "parallel")),
        name="moba",
    )(q, k, v, qx, kx)


MEMATTN_TQ = 512


def _memattn_kernel(q_ref, k_ref, v_ref, o_ref):
    q = q_ref[...]
    kb = k_ref[...].astype(BF16)
    vb = v_ref[...].astype(BF16)
    head = _shr(lax.broadcasted_iota(I32, q.shape, 1), HEAD_DIM.bit_length() - 1)
    out = jnp.zeros(q.shape, F32)
    for h in range(N_HEADS_MEM):
        qh = jnp.where(head == h, q, 0.0).astype(BF16)
        s = lax.dot_general(qh, kb, _NT, preferred_element_type=F32) * SCALE
        m = jnp.max(s, axis=-1, keepdims=True)
        p = jnp.exp(s - m)
        den = jnp.sum(p, axis=-1, keepdims=True)
        oh = jnp.dot(p.astype(BF16), vb, preferred_element_type=F32) / den
        out = jnp.where(head == h, oh, out)
    o_ref[...] = out


def _memattn(q, k, v):
    b, s_len, w = q.shape
    tq = MEMATTN_TQ
    return pl.pallas_call(
        _memattn_kernel,
        grid=(b, s_len // tq),
        in_specs=[pl.BlockSpec((None, tq, w), lambda i, j: (i, j, 0)),
                  pl.BlockSpec((None, MEM_LEN, w), lambda i, j: (i, 0, 0)),
                  pl.BlockSpec((None, MEM_LEN, w), lambda i, j: (i, 0, 0))],
        out_specs=pl.BlockSpec((None, tq, w), lambda i, j: (i, j, 0)),
        out_shape=jax.ShapeDtypeStruct((b, s_len, w), F32),
        compiler_params=_cparams(("parallel", "parallel")),
        name="memattn",
    )(q, k, v)


OUTPROJ_TM = 512


def _outproj_kernel(od_ref, ob_ref, om_ref, x_ref, gd, gb, gm, wo_ref, gf, wq_ref, k1_ref, k2_ref,
                    x1_ref, hn_ref, s1_ref, s2_ref):
    y = jnp.dot(_row_rms(od_ref[...], gd[...]).astype(BF16), wo_ref[0:W_DIL, :],
                preferred_element_type=F32)
    y += jnp.dot(_row_rms(ob_ref[...], gb[...]).astype(BF16), wo_ref[W_DIL:W_DIL + W_MOBA, :],
                 preferred_element_type=F32)
    y += jnp.dot(_row_rms(om_ref[...], gm[...]).astype(BF16), wo_ref[W_DIL + W_MOBA:, :],
                 preferred_element_type=F32)
    x1 = x_ref[...] + y
    x1_ref[...] = x1
    hb = _row_rms(x1, gf[...]).astype(BF16)
    hn_ref[...] = hb
    half = PEER_DKEY // 2
    qry = jnp.dot(hb, wq_ref[...], preferred_element_type=F32).astype(BF16)
    for h in range(PEER_HEADS):
        qh = qry[:, h * PEER_DKEY:(h + 1) * PEER_DKEY]
        s1_ref[h] = lax.dot_general(k1_ref[h], qh[:, :half], _NT, preferred_element_type=F32)
        s2_ref[h] = lax.dot_general(k2_ref[h], qh[:, half:], _NT, preferred_element_type=F32)


def _outproj(o_dil, o_moba, o_mem, x2, og_dil, og_moba, og_mem, w_out, g_ffn, w_q, sub1, sub2):
    t = x2.shape[0]
    tm = OUTPROJ_TM
    row = lambda w: pl.BlockSpec((1, w), lambda i: (0, 0))
    tile = lambda w: pl.BlockSpec((tm, w), lambda i: (i, 0))
    full = lambda shape: pl.BlockSpec(shape, lambda i: (0,) * len(shape))
    score = pl.BlockSpec((PEER_HEADS, PEER_NKEYS, tm), lambda i: (0, 0, i))
    half = PEER_DKEY // 2
    return pl.pallas_call(
        _outproj_kernel,
        grid=(t // tm,),
        in_specs=[tile(W_DIL), tile(W_MOBA), tile(W_MEM), tile(D_MODEL),
                  row(W_DIL), row(W_MOBA), row(W_MEM), full((D_MODEL, D_MODEL)), row(D_MODEL),
                  full((D_MODEL, PEER_HEADS * PEER_DKEY)),
                  full((PEER_HEADS, PEER_NKEYS, half)), full((PEER_HEADS, PEER_NKEYS, half))],
        out_specs=[tile(D_MODEL), tile(D_MODEL), score, score],
        out_shape=[jax.ShapeDtypeStruct((t, D_MODEL), F32),
                   jax.ShapeDtypeStruct((t, D_MODEL), BF16),
                   jax.ShapeDtypeStruct((PEER_HEADS, PEER_NKEYS, t), F32),
                   jax.ShapeDtypeStruct((PEER_HEADS, PEER_NKEYS, t), F32)],
        compiler_params=_cparams(("parallel",)),
        name="outproj",
    )(o_dil, o_moba, o_mem, x2, og_dil, og_moba, og_mem, w_out, g_ffn, w_q, sub1, sub2)


TOPK_TM = 512
TOPK_COLS_PER_ITER = 8
_ID_LIMIT = 1024.0


def _top_rows(s, k, ids=None, rows_at=None):
    n_rows = s.shape[0]
    if ids is None:
        ids = lax.broadcasted_iota(I32, s.shape, 0).astype(F32)
    vals, idxs = [], []
    for t in range(k):
        r = n_rows if rows_at is None else rows_at[t]
        head, head_ids = s[:r, :], ids[:r, :]
        m = jnp.max(head, axis=0, keepdims=True)
        first = jnp.min(jnp.where(head == m, head_ids, _ID_LIMIT), axis=0, keepdims=True)
        vals.append(m)
        idxs.append(first)
        head = jnp.where(head_ids == first, -jnp.inf, head)
        s = head if r == n_rows else jnp.concatenate([head, s[r:, :]], axis=0)
    return jnp.concatenate(vals, axis=0), jnp.concatenate(idxs, axis=0)


def _pick_rows(table, sel):
    out = jnp.zeros(sel.shape, table.dtype)
    for j in range(table.shape[0]):
        out = jnp.where(sel == j, table[j:j + 1, :], out)
    return out


def _candidates(v1, v2):
    n = v1.shape[1]
    sub = lax.broadcasted_iota(I32, (8, n), 0)
    sub_f = sub.astype(F32)
    sums, ids = [], []
    for j1 in range(8):
        both = v1[j1:j1 + 1, :] + v2[0:8, :]
        sums.append(both if j1 == 0 else jnp.where(sub < PEER_TOPK // (j1 + 1), both, -jnp.inf))
        ids.append(sub_f + float(j1 * PEER_TOPK))
    sums += [v1[0:1, :] + v2[8:16, :], v1[8:16, :] + v2[0:1, :]]
    ids += [sub_f + 8.0, sub_f * float(PEER_TOPK) + float(8 * PEER_TOPK)]
    rows_at = [8 * min(t, 8) if t <= 8 else 80 for t in range(1, PEER_TOPK + 1)]
    return jnp.concatenate(sums, axis=0), jnp.concatenate(ids, axis=0), rows_at


def _retrieve(s1, s2):
    v1, i1 = _top_rows(s1, PEER_TOPK)
    v2, i2 = _top_rows(s2, PEER_TOPK)
    sums, ids, rows_at = _candidates(v1, v2)
    top_s, pair = _top_rows(sums, PEER_TOPK, ids, rows_at)
    e = jnp.exp(top_s - top_s[0:1, :])
    pair = pair.astype(I32)
    bits = PEER_TOPK.bit_length() - 1
    return (_pick_rows(i1, _shr(pair, bits)), _pick_rows(i2, pair & (PEER_TOPK - 1)),
            e / jnp.sum(e, axis=0, keepdims=True))


def _peertopk_kernel(s1_ref, s2_ref, i1_ref, i2_ref, g_ref):
    n_col = s1_ref.shape[2] // LANES

    def body(i, carry):
        for c in range(TOPK_COLS_PER_ITER):
            item = i * TOPK_COLS_PER_ITER + c
            h = _shr_scalar(item, n_col.bit_length() - 1)
            cols = pl.ds(pl.multiple_of((item & (n_col - 1)) * LANES, LANES), LANES)
            i1_ref[h, :, cols], i2_ref[h, :, cols], g_ref[h, :, cols] = _retrieve(
                s1_ref[h, :, cols], s2_ref[h, :, cols])
        return carry

    lax.fori_loop(0, PEER_HEADS * n_col // TOPK_COLS_PER_ITER, body, 0)


def _peertopk(s1, s2):
    t = s1.shape[2]
    tm = TOPK_TM
    score = pl.BlockSpec((PEER_HEADS, PEER_NKEYS, tm), lambda i: (0, 0, i))
    slot = pl.BlockSpec((PEER_HEADS, PEER_TOPK, tm), lambda i: (0, 0, i))
    return pl.pallas_call(
        _peertopk_kernel,
        grid=(t // tm,),
        in_specs=[score, score],
        out_specs=[slot, slot, slot],
        out_shape=[jax.ShapeDtypeStruct((PEER_HEADS, PEER_TOPK, t), F32)] * 3,
        compiler_params=_cparams(("parallel",)),
        name="peertopk",
    )(s1, s2)


PEER_TM = 512
PEER_TE = 2048
PEER_KEY_PAIRS = PEER_NKEYS // 2
PEER_PITCH = PEER_TM + 8
PEER_BUILD_UNROLL = 128
BF16_ROWS = 16


def _rows_bf16(row, n_rows):
    tile = jnp.broadcast_to(row, (BF16_ROWS, row.shape[1])).astype(BF16)
    return jnp.concatenate([tile] * (n_rows // BF16_ROWS), axis=0)


def _peerffn_kernel(h_ref, x1_ref, i1_ref, i2_ref, g_ref, u_ref, v_ref, o_ref, gate_ref):
    c = pl.program_id(1)
    slabs = PEER_TE // PEER_NKEYS

    @pl.when(c == 0)
    def _build_gate_matrix():
        key_id = lax.broadcasted_iota(I32, (PEER_NKEYS, PEER_SLOTS), 0).astype(BF16)
        one = jnp.ones((PEER_NKEYS, PEER_SLOTS), BF16)
        zero = jnp.zeros((PEER_NKEYS, PEER_SLOTS), BF16)

        def token(t, carry):
            i1 = _rows_bf16(i1_ref[pl.ds(t, 1), :], PEER_NKEYS)
            i2 = _rows_bf16(i2_ref[pl.ds(t, 1), :], PEER_NKEYS)
            g = _rows_bf16(g_ref[pl.ds(t, 1), :], PEER_NKEYS)
            lhs = jnp.where(key_id == i1, g, zero)
            rhs = jnp.where(key_id == i2, one, zero)
            gt = lax.dot_general(lhs, rhs, _NT, preferred_element_type=F32)
            gate_ref[pl.ds(t, PEER_KEY_PAIRS, stride=PEER_PITCH), :] = pltpu.bitcast(gt.astype(BF16), U32)
            return carry

        lax.fori_loop(0, PEER_TM, token, 0, unroll=PEER_BUILD_UNROLL)
        o_ref[...] = x1_ref[...]

    a = lax.dot_general(h_ref[...], u_ref[...], _NT, preferred_element_type=F32)
    ws = []
    for r in range(slabs // 2):
        start = pl.multiple_of((c * (slabs // 2) + r) * PEER_PITCH, 8)
        word = gate_ref[pl.ds(start, PEER_TM), :]
        g_even = lax.bitcast_convert_type(lax.shift_left(word, jnp.uint32(16)), F32)
        g_odd = lax.bitcast_convert_type(word & jnp.uint32(0xFFFF0000), F32)
        for j, gj in ((2 * r, g_even), (2 * r + 1, g_odd)):
            aj = a[:, j * PEER_NKEYS:(j + 1) * PEER_NKEYS]
            ws.append((0.5 * aj * (1.0 + lax.erf(aj * SQRT_HALF)) * gj).astype(BF16))
    o_ref[...] += jnp.dot(jnp.concatenate(ws, axis=1), v_ref[...], preferred_element_type=F32)


def _peerffn(hn, x1, i1, i2, gate, u, v):
    t = hn.shape[0]
    tm, te = PEER_TM, PEER_TE
    tile = lambda w: pl.BlockSpec((tm, w), lambda i, c: (i, 0))
    chunk = pl.BlockSpec((te, D_MODEL), lambda i, c: (c, 0))
    return pl.pallas_call(
        _peerffn_kernel,
        grid=(t // tm, PEER_EXPERTS // te),
        in_specs=[tile(D_MODEL), tile(D_MODEL), tile(PEER_SLOTS), tile(PEER_SLOTS), tile(PEER_SLOTS),
                  chunk, chunk],
        out_specs=tile(D_MODEL),
        out_shape=jax.ShapeDtypeStruct((t, D_MODEL), F32),
        scratch_shapes=[pltpu.VMEM((PEER_KEY_PAIRS * PEER_PITCH, LANES), U32)],
        compiler_params=_cparams(("parallel", "arbitrary")),
        name="peerffn",
    )(hn, x1, i1, i2, gate, u, v)


def _neg_slope_rows(slopes, width):
    n = slopes.shape[0]
    return jnp.broadcast_to((-slopes).reshape(n // 2, 2, 1, 1), (n // 2, 2, 1, width))


def _tiled_gain(g, reps):
    return jnp.tile(g, reps)[None, :]


def _layer(x, mem, g_mix, w_in, qg_dil, kg_dil, qg_moba, kg_moba, qg_mem, kg_mem, g_memtok, w_mem_kv,
           og_dil, og_moba, og_mem, w_out, g_ffn, w_peer_q, sub1, sub2, peer_u, peer_v):
    b, s_len, d = x.shape
    t = b * s_len
    n_mix = N_HEADS_DIL + N_HEADS_MOBA
    slopes = jnp.exp2(-8.0 * jnp.arange(1, n_mix + 1, dtype=F32) / n_mix)
    ns_dil = _neg_slope_rows(slopes[0::2], 2 * BAND_BLOCK)

    x2 = x.reshape(t, d)
    k_m, v_m = _memkv(mem, g_memtok[None, :], w_mem_kv.astype(BF16), _tiled_gain(kg_mem, N_HEADS_MEM))
    q_d, k_d, v_d, q_b, k_b, v_b, q_m = _inproj(
        x2, g_mix[None, :], w_in.astype(BF16),
        _tiled_gain(qg_dil, N_HEADS_DIL), _tiled_gain(kg_dil, N_HEADS_DIL),
        _tiled_gain(qg_moba, N_HEADS_MOBA), _tiled_gain(kg_moba, N_HEADS_MOBA),
        _tiled_gain(qg_mem, N_HEADS_MEM))
    seq = lambda a: a.reshape(b, s_len, a.shape[-1])
    o_dil = _dilated(seq(q_d), seq(k_d), seq(v_d), ns_dil)
    o_moba = _moba(seq(q_b), seq(k_b), seq(v_b), slopes[1::2])
    o_mem = _memattn(seq(q_m), k_m, v_m)
    x1, hn, s1, s2 = _outproj(
        o_dil.reshape(t, W_DIL), o_moba.reshape(t, W_MOBA), o_mem.reshape(t, W_MEM), x2,
        og_dil[None, :], og_moba[None, :], og_mem[None, :], w_out.astype(BF16), g_ffn[None, :],
        w_peer_q.astype(BF16), sub1.astype(BF16), sub2.astype(BF16))
    i1, i2, gate = _peertopk(s1, s2)
    slots = lambda a: a.reshape(PEER_SLOTS, t).T
    out = _peerffn(hn, x1, slots(i1), slots(i2), slots(gate), peer_u.astype(BF16), peer_v.astype(BF16))
    return out.reshape(b, s_len, d)


def kernel(x, mem, g_mix, w_in, qg_dil, kg_dil, qg_moba, kg_moba, qg_mem, kg_mem, g_memtok, w_mem_kv,
           og_dil, og_moba, og_mem, w_out, g_ffn, w_peer_q, peer_subkeys_1, peer_subkeys_2, peer_u,
           peer_v):
    h = x
    for layer in range(g_mix.shape[0]):
        h = _layer(h, mem, g_mix[layer], w_in[layer], qg_dil[layer], kg_dil[layer], qg_moba[layer],
                   kg_moba[layer], qg_mem[layer], kg_mem[layer], g_memtok[layer], w_mem_kv[layer],
                   og_dil[layer], og_moba[layer], og_mem[layer], w_out[layer], g_ffn[layer],
                   w_peer_q[layer], peer_subkeys_1[layer], peer_subkeys_2[layer], peer_u[layer],
                   peer_v[layer])
    return h
```
